```python
import math
import jax
import jax.numpy as jnp
from jax import lax
import numpy as np

D_MODEL = 1024
BATCH = 8
SEQ = 8192
DEPTH = 1

DIFF_HEADS = 4
DIFF_HEAD_DIM = 64
DIFF_V_DIM = 2 * DIFF_HEAD_DIM
DSA_HEADS = 8
DSA_HEAD_DIM = 64
IDX_HEADS = 8
IDX_DIM = 64
DSA_TOPK = 256
Q_BLOCK = 128
N_EXPERTS = 256
TOP_K = 8
N_GROUPS = 8
TOPK_GROUPS = 4
D_EXPERT = 256
D_SHARED = 256
ROUTED_SCALE = 2.5
MOE_BLOCK = 64
EPS = 1e-6

DIFF_QK_W = DIFF_HEADS * 2 * DIFF_HEAD_DIM
DIFF_V_W = DIFF_HEADS * DIFF_V_DIM
DSA_W = DSA_HEADS * DSA_HEAD_DIM
IDX_Q_W = IDX_HEADS * IDX_DIM
IN_COLS = 2 * DIFF_QK_W + DIFF_V_W + 3 * DSA_W + IDX_Q_W + IDX_DIM + IDX_HEADS
SPLITS = (DIFF_QK_W, 2 * DIFF_QK_W, 2 * DIFF_QK_W + DIFF_V_W,
          2 * DIFF_QK_W + DIFF_V_W + DSA_W, 2 * DIFF_QK_W + DIFF_V_W + 2 * DSA_W,
          2 * DIFF_QK_W + DIFF_V_W + 3 * DSA_W, 2 * DIFF_QK_W + DIFF_V_W + 3 * DSA_W + IDX_Q_W,
          2 * DIFF_QK_W + DIFF_V_W + 3 * DSA_W + IDX_Q_W + IDX_DIM)
MIX_W = DIFF_V_W + DSA_W

kernel_name = "hymba_diffattn_dsa_moe_alibi_adaln"


def rmsnorm(x, g):
    xf = x.astype(jnp.float32)
    y = xf * lax.rsqrt(jnp.mean(xf * xf, axis=-1, keepdims=True) + EPS)
    return (y * g.astype(jnp.float32)).astype(x.dtype)


def alibi_slopes(n):
    return 2.0 ** (-8.0 * jnp.arange(1, n + 1, dtype=jnp.float32) / n)


def diff_attention(q, k, v, lam, lam_init, subln_g):
    B, S, H, _, Dh = q.shape
    Dv = v.shape[-1]
    slopes = alibi_slopes(H)
    scale = Dh ** -0.5
    kpos = jnp.arange(S)

    def block(i):
        q0 = i * Q_BLOCK
        qb = lax.dynamic_slice_in_dim(q, q0, Q_BLOCK, axis=1)
        s = jnp.einsum('bqhmd,bkhmd->bhmqk', qb, k).astype(jnp.float32) * scale
        dist = ((q0 + jnp.arange(Q_BLOCK))[:, None] - kpos[None, :]).astype(jnp.float32)
        s = s - (slopes[:, None, None] * dist)[None, :, None]
        s = jnp.where((dist >= 0)[None, None, None], s, -jnp.inf)
        p = jax.nn.softmax(s, axis=-1)
        a = p[:, :, 0] - lam * p[:, :, 1]
        return jnp.einsum('bhqk,bkhe->bqhe', a.astype(v.dtype), v)

    out = lax.map(block, jnp.arange(S // Q_BLOCK))
    out = out.transpose(1, 0, 2, 3, 4).reshape(B, S, H, Dv)
    out = rmsnorm(out, subln_g) * (1.0 - lam_init)
    return out.reshape(B, S, H * Dv)


def dsa_attention(q, k, v, iq, ik, iw, topk):
    B, S, H, Dh = q.shape
    slopes = alibi_slopes(H)
    scale = Dh ** -0.5
    idx_scale = IDX_DIM ** -0.5
    kpos = jnp.arange(S)
    gather = jax.vmap(lambda a, j: a[j])

    def block(i):
        q0 = i * Q_BLOCK
        qpos = q0 + jnp.arange(Q_BLOCK)
        iqb = lax.dynamic_slice_in_dim(iq, q0, Q_BLOCK, axis=1)
        iwb = lax.dynamic_slice_in_dim(iw, q0, Q_BLOCK, axis=1).astype(jnp.float32)
        logits = jnp.einsum('bqhd,bkd->bqhk', iqb, ik).astype(jnp.float32) * idx_scale
        score = jnp.einsum('bqh,bqhk->bqk', iwb, jax.nn.relu(logits))
        score = jnp.where((kpos[None, :] <= qpos[:, None])[None], score, -jnp.inf)
        _, sel = lax.top_k(score, topk)
        kg = gather(k, sel)
        vg = gather(v, sel)
        qb = lax.dynamic_slice_in_dim(q, q0, Q_BLOCK, axis=1)
        s = jnp.einsum('bqhd,bqkhd->bqhk', qb, kg).astype(jnp.float32) * scale
        dist = (qpos[None, :, None] - sel).astype(jnp.float32)
        s = s - slopes[None, None, :, None] * dist[:, :, None, :]
        s = jnp.where((dist >= 0)[:, :, None, :], s, -jnp.inf)
        p = jax.nn.softmax(s, axis=-1)
        return jnp.einsum('bqhk,bqkhd->bqhd', p.astype(v.dtype), vg)

    out = lax.map(block, jnp.arange(S // Q_BLOCK))
    return out.transpose(1, 0, 2, 3, 4).reshape(B, S, H * Dh)


def swiglu(h, wg, wu, wd):
    return (jax.nn.silu(h @ wg) * (h @ wu)) @ wd


def route(h, router_w, router_bias):
    T = h.shape[0]
    scores = jax.nn.sigmoid((h @ router_w).astype(jnp.float32))
    biased = scores + router_bias.astype(jnp.float32)
    grp = biased.reshape(T, N_GROUPS, N_EXPERTS // N_GROUPS)
    grp_score = lax.top_k(grp, 2)[0].sum(-1)
    _, grp_idx = lax.top_k(grp_score, TOPK_GROUPS)
    grp_mask = (grp_idx[..., None] == jnp.arange(N_GROUPS)).any(axis=-2)
    allowed = jnp.repeat(grp_mask, N_EXPERTS // N_GROUPS, axis=-1)
    _, eidx = lax.top_k(jnp.where(allowed, biased, -jnp.inf), TOP_K)
    w = jnp.take_along_axis(scores, eidx, axis=-1)
    w = w / jnp.sum(w, axis=-1, keepdims=True) * ROUTED_SCALE
    return eidx, w


def routed_experts(h, router_w, router_bias, w_gate, w_up, w_down):
    T, D = h.shape
    eidx, gates = route(h, router_w, router_bias)
    P = T * TOP_K
    n_blocks = P // MOE_BLOCK + N_EXPERTS
    e_flat = eidx.reshape(P)
    tok_flat = jnp.arange(P, dtype=jnp.int32) // TOP_K
    g_flat = gates.reshape(P).astype(h.dtype)
    order = jnp.argsort(e_flat)
    e_s, t_s, g_s = e_flat[order], tok_flat[order], g_flat[order]
    counts = jnp.bincount(e_flat, length=N_EXPERTS)
    padded = (counts + MOE_BLOCK - 1) // MOE_BLOCK * MOE_BLOCK
    start = jnp.cumsum(counts) - counts
    pstart = jnp.cumsum(padded) - padded
    pend = pstart + padded
    dest = pstart[e_s] + (jnp.arange(P) - start[e_s])
    slot_tok = jnp.full((n_blocks * MOE_BLOCK,), T, jnp.int32).at[dest].set(t_s)
    slot_gate = jnp.zeros((n_blocks * MOE_BLOCK,), h.dtype).at[dest].set(g_s)
    block_exp = jnp.minimum(jnp.searchsorted(pend, jnp.arange(n_blocks) * MOE_BLOCK, side='right'),
                            N_EXPERTS - 1)
    h_pad = jnp.concatenate([h, jnp.zeros((1, D), h.dtype)], axis=0)

    def blk(args):
        toks, g, e = args
        xb = h_pad[toks]
        return swiglu(xb, w_gate[e], w_up[e], w_down[e]) * g[:, None]

    y = lax.map(blk, (slot_tok.reshape(n_blocks, MOE_BLOCK),
                      slot_gate.reshape(n_blocks, MOE_BLOCK), block_exp))
    return jax.ops.segment_sum(y.reshape(-1, D), slot_tok, num_segments=T + 1)[:T]


def setup_inputs(seed: int = 0) -> dict:
    key = jax.random.key(seed)
    ks = iter(jax.random.split(key, 32))
    D, L = D_MODEL, DEPTH
    nrm = lambda shape, s: jax.random.normal(next(ks), shape, jnp.float32) * s
    gain = lambda shape: 1.0 + 0.02 * jax.random.normal(next(ks), shape, jnp.float32)
    return {
        "x": nrm((BATCH, SEQ, D), 1.0),
        "c": nrm((BATCH, D), 1.0),
        "ada_w": nrm((L, D, 6 * D), 0.5 * D ** -0.5),
        "ada_b": nrm((L, 6 * D), 0.02),
        "norm1_g": gain((L, D)),
        "norm2_g": gain((L, D)),
        "w_in": nrm((L, D, IN_COLS), D ** -0.5),
        "diff_q_norm_g": gain((L, DIFF_HEAD_DIM)),
        "diff_k_norm_g": gain((L, DIFF_HEAD_DIM)),
        "lam_q1": nrm((L, DIFF_HEAD_DIM), 0.1),
        "lam_k1": nrm((L, DIFF_HEAD_DIM), 0.1),
        "lam_q2": nrm((L, DIFF_HEAD_DIM), 0.1),
        "lam_k2": nrm((L, DIFF_HEAD_DIM), 0.1),
        "diff_subln_g": gain((L, DIFF_V_DIM)),
        "dsa_q_norm_g": gain((L, DSA_HEAD_DIM)),
        "dsa_k_norm_g": gain((L, DSA_HEAD_DIM)),
        "idx_k_norm_g": gain((L, IDX_DIM)),
        "w_out": nrm((L, MIX_W, D), MIX_W ** -0.5),
        "router_w": nrm((L, D, N_EXPERTS), D ** -0.5),
        "router_bias": nrm((L, N_EXPERTS), 0.01),
        "exp_w_gate": nrm((L, N_EXPERTS, D, D_EXPERT), D ** -0.5),
        "exp_w_up": nrm((L, N_EXPERTS, D, D_EXPERT), D ** -0.5),
        "exp_w_down": nrm((L, N_EXPERTS, D_EXPERT, D), D_EXPERT ** -0.5),
        "shared_w_gate": nrm((L, D, D_SHARED), D ** -0.5),
        "shared_w_up": nrm((L, D, D_SHARED), D ** -0.5),
        "shared_w_down": nrm((L, D_SHARED, D), D_SHARED ** -0.5),
    }


def reference(x, c, ada_w, ada_b, norm1_g, norm2_g, w_in, diff_q_norm_g, diff_k_norm_g,
              lam_q1, lam_k1, lam_q2, lam_k2, diff_subln_g, dsa_q_norm_g, dsa_k_norm_g,
              idx_k_norm_g, w_out, router_w, router_bias, exp_w_gate, exp_w_up, exp_w_down,
              shared_w_gate, shared_w_up, shared_w_down):
    B, S, D = x.shape
    topk = min(DSA_TOPK, S // 4)
    cond = jax.nn.silu(c)
    for l in range(DEPTH):
        mod = cond @ ada_w[l] + ada_b[l]
        sh1, sc1, g1, sh2, sc2, g2 = [m[:, None, :] for m in jnp.split(mod, 6, axis=-1)]

        h = rmsnorm(x, norm1_g[l]) * (1.0 + sc1) + sh1
        proj = h @ w_in[l]
        dq, dk, dv, sq, sk, sv, iq, ik, iw = jnp.split(proj, SPLITS, axis=-1)

        dq = rmsnorm(dq.reshape(B, S, DIFF_HEADS, 2, DIFF_HEAD_DIM), diff_q_norm_g[l])
        dk = rmsnorm(dk.reshape(B, S, DIFF_HEADS, 2, DIFF_HEAD_DIM), diff_k_norm_g[l])
        dv = dv.reshape(B, S, DIFF_HEADS, DIFF_V_DIM)
        lam_init = 0.8 - 0.6 * math.exp(-0.3 * l)
        lam = (jnp.exp(jnp.sum(lam_q1[l] * lam_k1[l])) - jnp.exp(jnp.sum(lam_q2[l] * lam_k2[l]))
               + lam_init)
        diff_out = diff_attention(dq, dk, dv, lam, lam_init, diff_subln_g[l])

        sq = rmsnorm(sq.reshape(B, S, DSA_HEADS, DSA_HEAD_DIM), dsa_q_norm_g[l])
        sk = rmsnorm(sk.reshape(B, S, DSA_HEADS, DSA_HEAD_DIM), dsa_k_norm_g[l])
        sv = sv.reshape(B, S, DSA_HEADS, DSA_HEAD_DIM)
        iq = iq.reshape(B, S, IDX_HEADS, IDX_DIM)
        ik = rmsnorm(ik, idx_k_norm_g[l])
        iw = iw * (IDX_HEADS ** -0.5)
        dsa_out = dsa_attention(sq, sk, sv, iq, ik, iw, topk)

        mix = jnp.concatenate([diff_out, dsa_out], axis=-1) @ w_out[l]
        x = x + g1 * mix

        h2 = rmsnorm(x, norm2_g[l]) * (1.0 + sc2) + sh2
        shared = swiglu(h2, shared_w_gate[l], shared_w_up[l], shared_w_down[l])
        routed = lax.map(lambda hs: routed_experts(hs, router_w[l], router_bias[l], exp_w_gate[l],
                                                   exp_w_up[l], exp_w_down[l]), h2)
        x = x + g2 * (shared + routed)
    return x
```

```python
import functools

import jax
import jax.numpy as jnp
from jax import lax
from jax.experimental import pallas as pl
from jax.experimental.pallas import tpu as pltpu

F32 = jnp.float32
BF16 = jnp.bfloat16
I32 = jnp.int32

D_MODEL = 1024
DIFF_HEADS = 4
DIFF_HEAD_DIM = 64
DSA_HEADS = 8
DSA_HEAD_DIM = 64
IDX_HEADS = 8
IDX_DIM = 64
DSA_TOPK = 256
N_EXPERTS = 256
TOP_K = 8
N_GROUPS = 8
GROUP_SIZE = N_EXPERTS // N_GROUPS
TOPK_GROUPS = 4
D_EXPERT = 256
D_SHARED = 256
ROUTED_SCALE = 2.5
EPS = 1e-6
LAM_INIT = 0.2

LANES = 128
SEG = 512
NEG_BIG = -1e30
INT_MIN = -2147483648
VMEM_LIMIT = 56 * 1024 * 1024

NT_DIMS = (((1,), (1,)), ((), ()))


def _nt(a, b):
    return lax.dot_general(a, b, NT_DIMS, preferred_element_type=F32)


def _dot(a, b):
    return jnp.dot(a, b, preferred_element_type=F32)


def _params(sem, vmem=VMEM_LIMIT, **kw):
    return pltpu.CompilerParams(dimension_semantics=sem, vmem_limit_bytes=vmem, **kw)


def _ada_kernel(c_ref, w_ref, b_ref, o_ref):
    c = c_ref[...]
    s = c / (1.0 + jnp.exp(-c))
    o_ref[...] = jnp.dot(s, w_ref[...], preferred_element_type=F32,
                         precision=lax.Precision.HIGHEST) + b_ref[...]


def _ada(c, w, b):
    B, D = c.shape
    N = w.shape[1]
    tn = D
    return pl.pallas_call(
        _ada_kernel,
        grid=(N // tn,),
        in_specs=[pl.BlockSpec((B, D), lambda j: (0, 0)),
                  pl.BlockSpec((D, tn), lambda j: (0, j)),
                  pl.BlockSpec((1, tn), lambda j: (0, j))],
        out_specs=pl.BlockSpec((B, tn), lambda j: (0, j)),
        out_shape=jax.ShapeDtypeStruct((B, N), F32),
        compiler_params=_params(("arbitrary",)),
        name="ada",
    )(c, w, b.reshape(1, N))


def _group_sumsq(z):
    n = z.shape[1]
    r = lax.broadcasted_iota(I32, (n, n), 0) // DIFF_HEAD_DIM
    c = lax.broadcasted_iota(I32, (n, n), 1) // DIFF_HEAD_DIM
    bd = jnp.where(r == c, 1.0, 0.0).astype(BF16)
    zz = z * z
    hi = zz.astype(BF16)
    lo = (zz - hi.astype(F32)).astype(BF16)
    return _dot(hi, bd) + _dot(lo, bd)


def _inproj_kernel(x_ref, mod_ref, g1_ref, wm_ref, wt_ref, gq_ref, gk_ref, gsq_ref, gsk_ref, gik_ref,
                   dq_ref, dk_ref, dv_ref, sq_ref, sk_ref, sv_ref, iq_ref, ikl_ref, ikh_ref, iw_ref):
    x = x_ref[0]
    ms = jnp.mean(x * x, axis=-1, keepdims=True)
    y = x * lax.rsqrt(ms + EPS) * g1_ref[...]
    mod = mod_ref[0]
    h = y * (1.0 + mod[1:2]) + mod[0:1]
    hb = h.astype(BF16)

    def plain(seg_idx, out_ref):
        out_ref[0] = _dot(hb, wm_ref[:, seg_idx * SEG:(seg_idx + 1) * SEG]).astype(BF16)

    def normed(seg_idx, g_ref, scale, out_ref):
        half = SEG // 2
        for i in range(2):
            lo = seg_idx * SEG + i * half
            z = _dot(hb, wm_ref[:, lo:lo + half])
            ss = _group_sumsq(z)
            zn = z * lax.rsqrt(ss * (1.0 / DIFF_HEAD_DIM) + EPS) * g_ref[:, i * half:(i + 1) * half]
            out_ref[0, :, i * half:(i + 1) * half] = (zn * scale).astype(BF16)

    normed(0, gq_ref, DIFF_HEAD_DIM ** -0.5, dq_ref)
    normed(1, gk_ref, 1.0, dk_ref)
    plain(2, dv_ref)
    normed(3, gsq_ref, DSA_HEAD_DIM ** -0.5, sq_ref)
    normed(4, gsk_ref, 1.0, sk_ref)
    plain(5, sv_ref)
    plain(6, iq_ref)

    t = _dot(hb, wt_ref[...])
    lane = lax.broadcasted_iota(I32, t.shape, 1)
    ikraw = jnp.where(lane < IDX_DIM, t, 0.0)
    ss = jnp.sum(ikraw * ikraw, axis=-1, keepdims=True) * (1.0 / IDX_DIM)
    ikn = ikraw * lax.rsqrt(ss + EPS) * gik_ref[...]
    ikl_ref[0] = ikn.astype(BF16)
    ikh_ref[0] = pltpu.roll(ikn, IDX_DIM, 1).astype(BF16)
    iwraw = jnp.where((lane >= IDX_DIM) & (lane < IDX_DIM + IDX_HEADS), t, 0.0)
    iw_ref[0] = pltpu.roll(iwraw * (IDX_HEADS ** -0.5), LANES - IDX_DIM, 1) * (IDX_DIM ** -0.5)


def _inproj(x, mod3, g1, wm, wt, gq, gk, gsq, gsk, gik, tm):
    B, S, D = x.shape
    ns = S // tm
    tok = lambda b, i: (b, i, 0)
    const2 = lambda b, i: (0, 0)
    seg_spec = pl.BlockSpec((1, tm, SEG), tok)
    lane_spec = pl.BlockSpec((1, tm, LANES), tok)
    seg_shape = jax.ShapeDtypeStruct((B, S, SEG), BF16)
    return pl.pallas_call(
        _inproj_kernel,
        grid=(B, ns),
        in_specs=[pl.BlockSpec((1, tm, D), tok),
                  pl.BlockSpec((1, 6, D), lambda b, i: (b, 0, 0)),
                  pl.BlockSpec((1, D), const2),
                  pl.BlockSpec(wm.shape, const2),
                  pl.BlockSpec(wt.shape, const2),
                  pl.BlockSpec((1, SEG), const2), pl.BlockSpec((1, SEG), const2),
                  pl.BlockSpec((1, SEG), const2), pl.BlockSpec((1, SEG), const2),
                  pl.BlockSpec((1, LANES), const2)],
        out_specs=[seg_spec] * 7 + [lane_spec] * 3,
        out_shape=[seg_shape] * 7 + [jax.ShapeDtypeStruct((B, S, LANES), BF16)] * 2
                  + [jax.ShapeDtypeStruct((B, S, LANES), F32)],
        compiler_params=_params(("parallel", "parallel")),
        name="inproj",
    )(x, mod3, g1, wm, wt, gq, gk, gsq, gsk, gik)


def _diff_kernel(q_ref, k_ref, v_ref, bias_ref, lam_ref, g_ref, o_ref, m_ref, l_ref, acc_ref, *, tq):
    qi = pl.program_id(1)
    ki = pl.program_id(2)

    @pl.when(ki == 0)
    def _():
        m_ref[...] = jnp.full(m_ref.shape, NEG_BIG, F32)
        l_ref[...] = jnp.zeros(l_ref.shape, F32)
        acc_ref[...] = jnp.zeros(acc_ref.shape, F32)

    def step(diag):
        lane = lax.broadcasted_iota(I32, (tq, LANES), 1)
        if diag:
            row = lax.broadcasted_iota(I32, (tq, tq), 0)
            col = lax.broadcasted_iota(I32, (tq, tq), 1)
            causal = col <= row
        blk_off = ((ki - qi) * tq).astype(F32)
        for h in range(DIFF_HEADS):
            slope = 2.0 ** (-8.0 * (h + 1) / DIFF_HEADS)
            c = slope * blk_off
            qh = q_ref[0, :, h * LANES:(h + 1) * LANES]
            kh = k_ref[0, :, h * LANES:(h + 1) * LANES]
            vh = v_ref[0, :, h * LANES:(h + 1) * LANES]
            for mp in range(2):
                keep = (lane < DIFF_HEAD_DIM) if mp == 0 else (lane >= DIFF_HEAD_DIM)
                qm = jnp.where(keep, qh, jnp.zeros_like(qh))
                s = _nt(qm, kh) + bias_ref[h]
                if diag:
                    s = jnp.where(causal, s, NEG_BIG)
                idx = 2 * h + mp
                m_old = m_ref[idx]
                m_new = jnp.maximum(m_old, jnp.max(s, axis=-1, keepdims=True) + c)
                p = jnp.exp(s - (m_new - c))
                alpha = jnp.exp(m_old - m_new)
                l_ref[idx] = alpha * l_ref[idx] + jnp.sum(p, axis=-1, keepdims=True)
                acc_ref[idx] = alpha * acc_ref[idx] + _dot(p.astype(BF16), vh)
                m_ref[idx] = m_new

    @pl.when(ki < qi)
    def _():
        step(False)

    @pl.when(ki == qi)
    def _():
        step(True)
        lv = lam_ref[...]
        lam = (jnp.exp(jnp.sum(lv[0:1] * lv[1:2], axis=-1, keepdims=True))
               - jnp.exp(jnp.sum(lv[2:3] * lv[3:4], axis=-1, keepdims=True)) + LAM_INIT)
        for h in range(DIFF_HEADS):
            o1 = acc_ref[2 * h] / l_ref[2 * h]
            o2 = acc_ref[2 * h + 1] / l_ref[2 * h + 1]
            o = o1 - lam * o2
            ms = jnp.mean(o * o, axis=-1, keepdims=True)
            on = o * lax.rsqrt(ms + EPS) * g_ref[...]
            o_ref[0, :, h * LANES:(h + 1) * LANES] = (on * (1.0 - LAM_INIT)).astype(BF16)


def _diff_attention(dq, dk, dv, bias, lamv, subln_g, tq):
    B, S, W = dq.shape
    nq = S // tq
    return pl.pallas_call(
        functools.partial(_diff_kernel, tq=tq),
        grid=(B, nq, nq),
        in_specs=[pl.BlockSpec((1, tq, W), lambda b, i, j: (b, i, 0)),
                  pl.BlockSpec((1, tq, W), lambda b, i, j: (b, jnp.minimum(i, j), 0)),
                  pl.BlockSpec((1, tq, W), lambda b, i, j: (b, jnp.minimum(i, j), 0)),
                  pl.BlockSpec(bias.shape, lambda b, i, j: (0, 0, 0)),
                  pl.BlockSpec(lamv.shape, lambda b, i, j: (0, 0)),
                  pl.BlockSpec((1, LANES), lambda b, i, j: (0, 0))],
        out_specs=pl.BlockSpec((1, tq, W), lambda b, i, j: (b, i, 0)),
        out_shape=jax.ShapeDtypeStruct((B, S, W), BF16),
        scratch_shapes=[pltpu.VMEM((2 * DIFF_HEADS, tq, 1), F32),
                        pltpu.VMEM((2 * DIFF_HEADS, tq, 1), F32),
                        pltpu.VMEM((2 * DIFF_HEADS, tq, LANES), F32)],
        compiler_params=_params(("parallel", "parallel", "arbitrary")),
        name="diff_attention",
    )(dq, dk, dv, bias, lamv, subln_g)


def _dsa_kernel(iq_ref, iw_ref, ikl_ref, ikh_ref, q_ref, k_ref, v_ref, o_ref,
                key_ref, m_ref, l_ref, acc_ref, *, tq, tk, topk):
    qi = pl.program_id(1)
    q0 = qi * tq
    nkc = (q0 + tq + tk - 1) // tk
    row = q0 + lax.broadcasted_iota(I32, (tq, tk), 0)
    col0 = lax.broadcasted_iota(I32, (tq, tk), 1)
    lane = lax.broadcasted_iota(I32, (tq, LANES), 1)
    low = lane < DSA_HEAD_DIM
    iw = iw_ref[0]

    def score_chunk(kc, carry):
        k0 = pl.multiple_of(kc * tk, tk)
        ikl = ikl_ref[0, pl.ds(k0, tk), :]
        ikh = ikh_ref[0, pl.ds(k0, tk), :]
        sc = jnp.zeros((tq, tk), F32)
        for j in range(IDX_HEADS // 2):
            iqp = iq_ref[0, :, j * LANES:(j + 1) * LANES]
            sc = sc + iw[:, 2 * j:2 * j + 1] * jnp.maximum(_nt(iqp, ikl), 0.0)
            sc = sc + iw[:, 2 * j + 1:2 * j + 2] * jnp.maximum(_nt(iqp, ikh), 0.0)
        sc = jnp.where(sc == 0.0, 0.0, sc)
        sc = jnp.where(col0 + k0 <= row, sc, -jnp.inf)
        bits = lax.bitcast_convert_type(sc, I32)
        key_ref[:, pl.ds(k0, tk)] = bits ^ ((bits >> 31) & 0x7FFFFFFF)
        return carry

    lax.fori_loop(0, nkc, score_chunk, 0)

    def count_ge(thr, strict):
        def body(kc, acc):
            k0 = pl.multiple_of(kc * tk, tk)
            keyc = key_ref[:, pl.ds(k0, tk)]
            hit = jnp.where((keyc > thr) if strict else (keyc >= thr), 1.0, 0.0)
            for g in range(tk // LANES):
                acc = acc + hit[:, g * LANES:(g + 1) * LANES]
            return acc
        acc = lax.fori_loop(0, nkc, body, jnp.zeros((tq, LANES), F32))
        return jnp.sum(acc, axis=-1, keepdims=True)

    def bit_step(i, tu):
        cand_u = tu | jnp.left_shift(jnp.int32(1), 31 - i)
        cnt = count_ge(cand_u ^ INT_MIN, False)
        return jnp.where(cnt >= float(topk), cand_u, tu)

    thr = lax.fori_loop(0, 32, bit_step, jnp.zeros((tq, 1), I32)) ^ INT_MIN
    n_ties = float(topk) - count_ge(thr, True)

    m_ref[...] = jnp.full(m_ref.shape, NEG_BIG, F32)
    l_ref[...] = jnp.zeros(l_ref.shape, F32)
    acc_ref[...] = jnp.zeros(acc_ref.shape, F32)
    tri = jnp.where(lax.broadcasted_iota(I32, (tk, tk), 0) < lax.broadcasted_iota(I32, (tk, tk), 1),
                    1.0, 0.0).astype(BF16)

    def attend_chunk(kc, ties_before):
        k0 = pl.multiple_of(kc * tk, tk)
        keyc = key_ref[:, pl.ds(k0, tk)]
        eq = jnp.where(keyc == thr, 1.0, 0.0)
        rank = _dot(eq.astype(BF16), tri) + ties_before
        take = jnp.where(keyc > thr, 1.0, jnp.where(rank < n_ties, eq, 0.0))
        take = jnp.where(col0 + k0 <= row, take, 0.0)
        sel = take > 0.5
        dist = (row - (col0 + k0)).astype(F32)
        for j in range(DSA_HEADS // 2):
            qp = q_ref[0, :, j * LANES:(j + 1) * LANES]
            kp = k_ref[0, pl.ds(k0, tk), j * LANES:(j + 1) * LANES]
            vp = v_ref[0, pl.ds(k0, tk), j * LANES:(j + 1) * LANES]
            vlane = lax.broadcasted_iota(I32, vp.shape, 1) < DSA_HEAD_DIM
            pv = jnp.zeros((tq, LANES), F32)
            alphas = []
            for a in range(2):
                hd = 2 * j + a
                slope = 2.0 ** (-8.0 * (hd + 1) / DSA_HEADS)
                keep = low if a == 0 else jnp.logical_not(low)
                qa = jnp.where(keep, qp, jnp.zeros_like(qp))
                s = _nt(qa, kp) - slope * dist
                s = jnp.where(sel, s, NEG_BIG)
                m_old = m_ref[hd]
                m_new = jnp.maximum(m_old, jnp.max(s, axis=-1, keepdims=True))
                p = jnp.exp(s - m_new)
                alpha = jnp.exp(m_old - m_new)
                l_ref[hd] = alpha * l_ref[hd] + jnp.sum(p, axis=-1, keepdims=True)
                m_ref[hd] = m_new
                va = jnp.where(vlane if a == 0 else jnp.logical_not(vlane), vp, jnp.zeros_like(vp))
                pv = pv + _dot(p.astype(BF16), va)
                alphas.append(alpha)
            acc_ref[j] = acc_ref[j] * jnp.where(low, alphas[0], alphas[1]) + pv
        return ties_before + jnp.sum(eq, axis=-1, keepdims=True)

    lax.fori_loop(0, nkc, attend_chunk, jnp.zeros((tq, 1), F32))

    for j in range(DSA_HEADS // 2):
        inv = jnp.where(low, 1.0 / l_ref[2 * j], 1.0 / l_ref[2 * j + 1])
        o_ref[0, :, j * LANES:(j + 1) * LANES] = (acc_ref[j] * inv).astype(BF16)


def _dsa_attention(iq, iw, ikl, ikh, sq, sk, sv, tq, tk, topk):
    B, S, W = sq.shape
    nq = S // tq
    qblk = lambda b, i: (b, i, 0)
    full = lambda b, i: (b, 0, 0)
    once = pl.Buffered(1)
    return pl.pallas_call(
        functools.partial(_dsa_kernel, tq=tq, tk=tk, topk=topk),
        grid=(B, nq),
        in_specs=[pl.BlockSpec((1, tq, W), qblk),
                  pl.BlockSpec((1, tq, LANES), qblk),
                  pl.BlockSpec((1, S, LANES), full, pipeline_mode=once),
                  pl.BlockSpec((1, S, LANES), full, pipeline_mode=once),
                  pl.BlockSpec((1, tq, W), qblk),
                  pl.BlockSpec((1, S, W), full, pipeline_mode=once),
                  pl.BlockSpec((1, S, W), full, pipeline_mode=once)],
        out_specs=pl.BlockSpec((1, tq, W), qblk),
        out_shape=jax.ShapeDtypeStruct((B, S, W), BF16),
        scratch_shapes=[pltpu.VMEM((tq, S), I32),
                        pltpu.VMEM((DSA_HEADS, tq, 1), F32),
                        pltpu.VMEM((DSA_HEADS, tq, 1), F32),
                        pltpu.VMEM((DSA_HEADS // 2, tq, LANES), F32)],
        compiler_params=_params(("parallel", "arbitrary")),
        name="dsa_attention",
    )(iq, iw, ikl, ikh, sq, sk, sv)


def _mix_kernel(do_ref, so_ref, x_ref, mod_ref, g2_ref, wo1_ref, wo2_ref, wsg_ref, wsu_ref, wsd_ref,
                rwt_ref, base_ref, h2_ref, lg_ref):
    mix = _dot(do_ref[0], wo1_ref[...]) + _dot(so_ref[0], wo2_ref[...])
    mod = mod_ref[0]
    x1 = x_ref[0] + mod[2:3] * mix
    ms = jnp.mean(x1 * x1, axis=-1, keepdims=True)
    h2 = x1 * lax.rsqrt(ms + EPS) * g2_ref[...] * (1.0 + mod[4:5]) + mod[3:4]
    hb = h2.astype(BF16)
    gate = _dot(hb, wsg_ref[...])
    up = _dot(hb, wsu_ref[...])
    act = gate / (1.0 + jnp.exp(-gate)) * up
    shared = _dot(act.astype(BF16), wsd_ref[...])
    base_ref[0] = x1 + mod[5:6] * shared
    lg_ref[...] = _nt(rwt_ref[...], hb)
    hf = hb.astype(F32)
    for j in range(D_MODEL // LANES):
        h2_ref[:, j, :] = hf[:, j * LANES:(j + 1) * LANES]


def _mix(diff_out, dsa_out, x, mod3, g2, wo1, wo2, wsg, wsu, wsd, rwt, tm):
    B, S, D = x.shape
    ns = S // tm
    T = B * S
    tok = lambda b, i: (b, i, 0)
    c2 = lambda b, i: (0, 0)
    return pl.pallas_call(
        _mix_kernel,
        grid=(B, ns),
        in_specs=[pl.BlockSpec((1, tm, SEG), tok), pl.BlockSpec((1, tm, SEG), tok),
                  pl.BlockSpec((1, tm, D), tok),
                  pl.BlockSpec((1, 6, D), lambda b, i: (b, 0, 0)),
                  pl.BlockSpec((1, D), c2),
                  pl.BlockSpec(wo1.shape, c2), pl.BlockSpec(wo2.shape, c2),
                  pl.BlockSpec(wsg.shape, c2), pl.BlockSpec(wsu.shape, c2), pl.BlockSpec(wsd.shape, c2),
                  pl.BlockSpec(rwt.shape, c2)],
        out_specs=[pl.BlockSpec((1, tm, D), tok),
                   pl.BlockSpec((tm, D // LANES, LANES), lambda b, i: (b * ns + i, 0, 0)),
                   pl.BlockSpec((N_EXPERTS, tm), lambda b, i: (0, b * ns + i))],
        out_shape=[jax.ShapeDtypeStruct((B, S, D), F32),
                   jax.ShapeDtypeStruct((T, D // LANES, LANES), F32),
                   jax.ShapeDtypeStruct((N_EXPERTS, T), F32)],
        compiler_params=_params(("parallel", "parallel")),
        name="mix_shared_router",
    )(diff_out, dsa_out, x, mod3, g2, wo1, wo2, wsg, wsu, wsd, rwt)


def _first_max(v, idx, sentinel):
    m = jnp.max(v, axis=0, keepdims=True)
    i = jnp.min(jnp.where(v == m, idx, sentinel), axis=0, keepdims=True)
    return m, i


def _route_kernel(lg_ref, bias_ref, eidx_ref, gate_ref):
    lg = lg_ref[...]
    tt = lg.shape[1]
    scores = 1.0 / (1.0 + jnp.exp(-lg))
    biased = scores + bias_ref[...]
    gi = lax.broadcasted_iota(I32, (GROUP_SIZE, tt), 0).astype(F32)
    gscore = []
    for g in range(N_GROUPS):
        blk = biased[g * GROUP_SIZE:(g + 1) * GROUP_SIZE, :]
        m1, i1 = _first_max(blk, gi, float(GROUP_SIZE))
        m2 = jnp.max(jnp.where(gi == i1, -jnp.inf, blk), axis=0, keepdims=True)
        gscore.append(m1 + m2)
    gs = jnp.concatenate(gscore, axis=0)
    gidx = lax.broadcasted_iota(I32, (N_GROUPS, tt), 0).astype(F32)
    chosen = jnp.zeros((N_GROUPS, tt), F32)
    for _ in range(TOPK_GROUPS):
        _, ig = _first_max(gs, gidx, float(N_GROUPS))
        hit = gidx == ig
        chosen = jnp.where(hit, 1.0, chosen)
        gs = jnp.where(hit, -jnp.inf, gs)
    masked = jnp.concatenate(
        [jnp.where(chosen[g:g + 1, :] > 0.5, biased[g * GROUP_SIZE:(g + 1) * GROUP_SIZE, :], -jnp.inf)
         for g in range(N_GROUPS)], axis=0)
    ei = lax.broadcasted_iota(I32, (N_EXPERTS, tt), 0).astype(F32)
    ids, ws = [], []
    for _ in range(TOP_K):
        _, ie = _first_max(masked, ei, float(N_EXPERTS))
        hit = ei == ie
        ws.append(jnp.sum(jnp.where(hit, scores, 0.0), axis=0, keepdims=True))
        ids.append(ie)
        masked = jnp.where(hit, -jnp.inf, masked)
    w = jnp.concatenate(ws, axis=0)
    gate_ref[...] = w / jnp.sum(w, axis=0, keepdims=True) * ROUTED_SCALE
    eidx_ref[...] = jnp.concatenate(ids, axis=0).astype(I32)


def _route(logits_t, bias_col, tt):
    E, T = logits_t.shape
    return pl.pallas_call(
        _route_kernel,
        grid=(T // tt,),
        in_specs=[pl.BlockSpec((E, tt), lambda i: (0, i)),
                  pl.BlockSpec((E, 1), lambda i: (0, 0))],
        out_specs=[pl.BlockSpec((TOP_K, tt), lambda i: (0, i)),
                   pl.BlockSpec((TOP_K, tt), lambda i: (0, i))],
        out_shape=[jax.ShapeDtypeStruct((TOP_K, T), I32),
                   jax.ShapeDtypeStruct((TOP_K, T), F32)],
        compiler_params=_params(("parallel",)),
        name="route",
    )(logits_t, bias_col)


def _plan_kernel(eidx_ref, dest_ref, bexp_ref, nused_ref, cnt_col, cnt_row, slot_base, *, blk, nb_pad):
    ph = pl.program_id(0)
    i = pl.program_id(1)
    tt = eidx_ref.shape[1]
    eidx = eidx_ref[...]
    ei = lax.broadcasted_iota(I32, (N_EXPERTS, tt), 0)
    onehot = jnp.zeros((N_EXPERTS, tt), F32)
    for k in range(TOP_K):
        onehot = onehot + jnp.where(ei == eidx[k:k + 1, :], 1.0, 0.0)
    oh = onehot.astype(BF16)

    @pl.when((ph == 0) & (i == 0))
    def _():
        cnt_col[...] = jnp.zeros(cnt_col.shape, F32)
        cnt_row[...] = jnp.zeros(cnt_row.shape, F32)

    @pl.when(ph == 0)
    def _():
        cnt_col[...] += _dot(oh, jnp.ones((tt, LANES), BF16))
        cnt_row[...] += _nt(jnp.ones((8, tt), BF16), oh)

    @pl.when((ph == 1) & (i == 0))
    def _():
        inv = 1.0 / blk
        nb_col = jnp.floor((cnt_col[:, 0:1] + (blk - 1)) * inv)
        nb_row = jnp.floor((cnt_row[0:1, :] + (blk - 1)) * inv)
        r = lax.broadcasted_iota(I32, (N_EXPERTS, N_EXPERTS), 0)
        c = lax.broadcasted_iota(I32, (N_EXPERTS, N_EXPERTS), 1)
        bstart = jnp.sum(jnp.where(c < r, nb_row, 0.0), axis=-1, keepdims=True)
        bend = bstart + nb_col
        slot_base[...] = bstart * blk
        jb = lax.broadcasted_iota(I32, (N_EXPERTS, nb_pad), 1).astype(F32)
        be = jnp.sum(jnp.where(bend <= jb, 1.0, 0.0), axis=0, keepdims=True)
        bexp_ref[...] = jnp.minimum(be, N_EXPERTS - 1.0).astype(I32)
        nused_ref[...] = jnp.broadcast_to(jnp.sum(nb_row, axis=-1, keepdims=True), nused_ref.shape).astype(I32)

    @pl.when(ph == 1)
    def _():
        tri = jnp.where(lax.broadcasted_iota(I32, (tt, tt), 0) < lax.broadcasted_iota(I32, (tt, tt), 1),
                        1.0, 0.0).astype(BF16)
        slot = _dot(oh, tri) + slot_base[...]
        for k in range(TOP_K):
            dk = jnp.sum(jnp.where(ei == eidx[k:k + 1, :], slot, 0.0), axis=0, keepdims=True)
            dest_ref[k:k + 1, :] = dk.astype(I32)
        slot_base[...] += jnp.sum(onehot, axis=-1, keepdims=True)


def _plan(eidx, tt, blk, nb_pad):
    K, T = eidx.shape
    nt = T // tt
    return pl.pallas_call(
        functools.partial(_plan_kernel, blk=blk, nb_pad=nb_pad),
        grid=(2, nt),
        in_specs=[pl.BlockSpec((K, tt), lambda p, i: (0, i))],
        out_specs=[pl.BlockSpec((K, tt), lambda p, i: (0, i * p)),
                   pl.BlockSpec((1, nb_pad), lambda p, i: (0, 0)),
                   pl.BlockSpec((1, LANES), lambda p, i: (0, 0))],
        out_shape=[jax.ShapeDtypeStruct((K, T), I32),
                   jax.ShapeDtypeStruct((1, nb_pad), I32),
                   jax.ShapeDtypeStruct((1, LANES), I32)],
        scratch_shapes=[pltpu.VMEM((N_EXPERTS, LANES), F32),
                        pltpu.VMEM((8, N_EXPERTS), F32),
                        pltpu.VMEM((N_EXPERTS, 1), F32)],
        compiler_params=_params(("arbitrary", "arbitrary")),
        name="plan",
    )(eidx)


def _dispatch_kernel(dest_ref, h_ref, xs_in_ref, xs_ref, sem):
    del xs_in_ref
    tt = h_ref.shape[0]

    def issue(t, c):
        for k in range(TOP_K):
            pltpu.make_async_copy(h_ref.at[t], xs_ref.at[dest_ref[k, t]], sem).start()
        return c

    lax.fori_loop(0, tt, issue, 0)

    def drain(t, c):
        for k in range(TOP_K):
            pltpu.make_async_copy(h_ref.at[0], xs_ref.at[0], sem).wait()
        return c

    lax.fori_loop(0, tt, drain, 0)


def _dispatch(dest, h2rows, xs_init, tt):
    T = h2rows.shape[0]
    return pl.pallas_call(
        _dispatch_kernel,
        grid=(T // tt,),
        in_specs=[pl.BlockSpec((TOP_K, tt), lambda i: (0, i), memory_space=pltpu.SMEM),
                  pl.BlockSpec((tt,) + h2rows.shape[1:], lambda i: (i, 0, 0)),
                  pl.BlockSpec(memory_space=pl.ANY)],
        out_specs=pl.BlockSpec(memory_space=pl.ANY),
        out_shape=jax.ShapeDtypeStruct(xs_init.shape, xs_init.dtype),
        scratch_shapes=[pltpu.SemaphoreType.DMA(())],
        input_output_aliases={2: 0},
        compiler_params=_params(("arbitrary",), has_side_effects=True),
        name="dispatch",
    )(dest, h2rows, xs_init)


def _experts_kernel(bexp_ref, nused_ref, xs_ref, wg_ref, wu_ref, wd_ref, y_ref, xb_ref):
    del bexp_ref

    @pl.when(pl.program_id(0) < nused_ref[0])
    def _():
        for j in range(D_MODEL // LANES):
            xb_ref[:, j * LANES:(j + 1) * LANES] = xs_ref[:, j, :].astype(BF16)
        xb = xb_ref[...]
        gate = _dot(xb, wg_ref[0])
        up = _dot(xb, wu_ref[0])
        act = gate / (1.0 + jnp.exp(-gate)) * up
        y = _dot(act.astype(BF16), wd_ref[0])
        for j in range(D_MODEL // LANES):
            y_ref[:, j, :] = y[:, j * LANES:(j + 1) * LANES]


def _experts(bexp, nused, xs, wg, wu, wd, blk, n_blocks):
    P = xs.shape[0]
    live = lambda j, be, nu: jnp.minimum(j, nu[0] - 1)
    row_spec = pl.BlockSpec((blk,) + xs.shape[1:], lambda j, be, nu: (live(j, be, nu), 0, 0))
    wspec = lambda w: pl.BlockSpec((1,) + w.shape[1:], lambda j, be, nu: (be[live(j, be, nu)], 0, 0))
    return pl.pallas_call(
        _experts_kernel,
        grid_spec=pltpu.PrefetchScalarGridSpec(
            num_scalar_prefetch=2,
            grid=(n_blocks,),
            in_specs=[row_spec, wspec(wg), wspec(wu), wspec(wd)],
            out_specs=row_spec,
            scratch_shapes=[pltpu.VMEM((blk, D_MODEL), BF16)]),
        out_shape=jax.ShapeDtypeStruct(xs.shape, F32),
        compiler_params=_params(("arbitrary",)),
        name="experts",
    )(bexp, nused, xs, wg, wu, wd)


def _combine_kernel(dest_ref, gate_ref, base_ref, mod_ref, y_ref, o_ref, buf, sem):
    tt = base_ref.shape[1]

    def issue(t, c):
        for k in range(TOP_K):
            pltpu.make_async_copy(y_ref.at[dest_ref[k, t]], buf.at[k * tt + t], sem).start()
        return c

    lax.fori_loop(0, tt, issue, 0)

    def drain(t, c):
        for k in range(TOP_K):
            pltpu.make_async_copy(y_ref.at[0], buf.at[0], sem).wait()
        return c

    lax.fori_loop(0, tt, drain, 0)

    gates = gate_ref[...]
    g2 = mod_ref[0][5:6]
    for j in range(D_MODEL // LANES):
        acc = jnp.zeros((tt, LANES), F32)
        for k in range(TOP_K):
            acc = acc + gates[:, k:k + 1] * buf[k * tt:(k + 1) * tt, j, :]
        cols = slice(j * LANES, (j + 1) * LANES)
        o_ref[0, :, cols] = base_ref[0, :, cols] + g2[:, cols] * acc


def _combine(dest, gates_tk, base, mod3, y, tt):
    B, S, D = base.shape
    ns = S // tt
    return pl.pallas_call(
        _combine_kernel,
        grid=(B, ns),
        in_specs=[pl.BlockSpec((TOP_K, tt), lambda b, i: (0, b * ns + i), memory_space=pltpu.SMEM),
                  pl.BlockSpec((tt, TOP_K), lambda b, i: (b * ns + i, 0)),
                  pl.BlockSpec((1, tt, D), lambda b, i: (b, i, 0)),
                  pl.BlockSpec((1, 6, D), lambda b, i: (b, 0, 0)),
                  pl.BlockSpec(memory_space=pl.ANY)],
        out_specs=pl.BlockSpec((1, tt, D), lambda b, i: (b, i, 0)),
        out_shape=jax.ShapeDtypeStruct((B, S, D), F32),
        scratch_shapes=[pltpu.VMEM((TOP_K * tt,) + y.shape[1:], F32),
                        pltpu.SemaphoreType.DMA(())],
        compiler_params=_params(("arbitrary", "arbitrary")),
        name="combine",
    )(dest, gates_tk, base, mod3, y)


def _alibi_tile_bias(n_heads, t):
    slopes = 2.0 ** (-8.0 * jnp.arange(1, n_heads + 1, dtype=F32) / n_heads)
    d = (jnp.arange(t)[:, None] - jnp.arange(t)[None, :]).astype(F32)
    return -slopes[:, None, None] * d[None]


def kernel(x, c, ada_w, ada_b, norm1_g, norm2_g, w_in, diff_q_norm_g, diff_k_norm_g, lam_q1, lam_k1, lam_q2, lam_k2, diff_subln_g, dsa_q_norm_g, dsa_k_norm_g, idx_k_norm_g, w_out, router_w, router_bias, exp_w_gate, exp_w_up, exp_w_down, shared_w_gate, shared_w_up, shared_w_down):
    B, S, D = x.shape
    assert D == D_MODEL and ada_w.shape[0] == 1
    T = B * S
    topk = min(DSA_TOPK, S // 4)
    tm = min(512, S)
    tq_diff = min(512, S)
    tq_dsa = min(256, S)
    tk_dsa = min(512, S)
    tt_route = min(512, T)
    tt_move = min(128, S)
    blk = 512
    n_blocks = (T * TOP_K) // blk + N_EXPERTS
    nb_pad = -(-n_blocks // LANES) * LANES

    n_main = 7 * SEG
    wm = w_in[0, :, :n_main].astype(BF16)
    wt = jnp.pad(w_in[0, :, n_main:], ((0, 0), (0, LANES - (IDX_DIM + IDX_HEADS)))).astype(BF16)
    tile8 = lambda g: jnp.tile(g[0], SEG // g.shape[1]).reshape(1, SEG)
    gik = jnp.pad(idx_k_norm_g[0], (0, LANES - IDX_DIM)).reshape(1, LANES)
    lamv = jnp.concatenate([lam_q1, lam_k1, lam_q2, lam_k2], axis=0)
    wo1 = w_out[0, :SEG].astype(BF16)
    wo2 = w_out[0, SEG:].astype(BF16)
    rwt = router_w[0].T.astype(BF16)
    wg = exp_w_gate[0].astype(BF16)
    wu = exp_w_up[0].astype(BF16)
    wd = exp_w_down[0].astype(BF16)

    mod3 = _ada(c, ada_w[0], ada_b[0]).reshape(B, 6, D)

    dq, dk, dv, sq, sk, sv, iq, ikl, ikh, iw = _inproj(
        x, mod3, norm1_g, wm, wt, tile8(diff_q_norm_g), tile8(diff_k_norm_g),
        tile8(dsa_q_norm_g), tile8(dsa_k_norm_g), gik, tm)

    diff_out = _diff_attention(dq, dk, dv, _alibi_tile_bias(DIFF_HEADS, tq_diff), lamv, diff_subln_g, tq_diff)
    dsa_out = _dsa_attention(iq, iw, ikl, ikh, sq, sk, sv, tq_dsa, tk_dsa, topk)

    base, h2rows, logits_t = _mix(diff_out, dsa_out, x, mod3, norm2_g, wo1, wo2,
                                  shared_w_gate[0].astype(BF16), shared_w_up[0].astype(BF16),
                                  shared_w_down[0].astype(BF16), rwt, tm)

    eidx, gates = _route(logits_t, router_bias[0].reshape(N_EXPERTS, 1), tt_route)
    dest, bexp, nused = _plan(eidx, tt_route, blk, nb_pad)

    xs = _dispatch(dest, h2rows, jnp.zeros((n_blocks * blk, D // LANES, LANES), F32), tt_move)
    y = _experts(bexp.reshape(nb_pad), nused[0, :1], xs, wg, wu, wd, blk, n_blocks)
    return _combine(dest, gates.T, base, mod3, y, tt_move)
```

```python
import functools
import math

import jax
import jax.numpy as jnp
from jax import lax
from jax.experimental import pallas as pl
from jax.experimental.pallas import tpu as pltpu

F32 = jnp.float32
BF16 = jnp.bfloat16
I32 = jnp.int32

D_MODEL = 1024
DIFF_HEADS = 4
DIFF_HEAD_DIM = 64
DSA_HEADS = 8
DSA_HEAD_DIM = 64
IDX_HEADS = 8
IDX_DIM = 64
DSA_TOPK = 256
N_EXPERTS = 256
TOP_K = 8
N_GROUPS = 8
GROUP_SIZE = N_EXPERTS // N_GROUPS
TOPK_GROUPS = 4
D_EXPERT = 256
D_SHARED = 256
ROUTED_SCALE = 2.5
EPS = 1e-6
LAM_INIT = 0.2

LANES = 128
ROW_TILES = D_MODEL // LANES
SEG = 512
HEAD = 64
N_MAPS = SEG // HEAD
WIDE = N_MAPS * LANES
POS_RADIX = 64
N_SPLIT = 3
NEG_BIG = -1e30
INT_MIN = -2147483648
LOG2E = math.log2(math.e)
VMEM_LIMIT = 56 * 1024 * 1024

NT_DIMS = (((1,), (1,)), ((), ()))


def _nt(a, b):
    return lax.dot_general(a, b, NT_DIMS, preferred_element_type=F32)


def _dot(a, b):
    return jnp.dot(a, b, preferred_element_type=F32)


def _rep(x, reps):
    return jnp.concatenate([x] * reps, axis=1)


def _params(sem, vmem=VMEM_LIMIT, **kw):
    return pltpu.CompilerParams(dimension_semantics=sem, vmem_limit_bytes=vmem, **kw)


def _ada_kernel(c_ref, w_ref, b_ref, o_ref):
    c = c_ref[...]
    s = c / (1.0 + jnp.exp(-c))
    o_ref[...] = jnp.dot(s, w_ref[...], preferred_element_type=F32,
                         precision=lax.Precision.HIGHEST) + b_ref[...]


def _ada(c, w, b):
    B, D = c.shape
    N = w.shape[1]
    tn = D
    return pl.pallas_call(
        _ada_kernel,
        grid=(N // tn,),
        in_specs=[pl.BlockSpec((B, D), lambda j: (0, 0)),
                  pl.BlockSpec((D, tn), lambda j: (0, j)),
                  pl.BlockSpec((1, tn), lambda j: (0, j))],
        out_specs=pl.BlockSpec((B, tn), lambda j: (0, j)),
        out_shape=jax.ShapeDtypeStruct((B, N), F32),
        compiler_params=_params(("arbitrary",)),
        name="ada",
    )(c, w, b.reshape(1, N))


def _group_sumsq(z):
    n = z.shape[1]
    r = lax.broadcasted_iota(I32, (n, n), 0) // HEAD
    c = lax.broadcasted_iota(I32, (n, n), 1) // HEAD
    bd = jnp.where(r == c, 1.0, 0.0).astype(BF16)
    zz = z * z
    hi = zz.astype(BF16)
    lo = (zz - hi.astype(F32)).astype(BF16)
    return _dot(hi, bd) + _dot(lo, bd)


def _inproj_kernel(x_ref, mod_ref, g1_ref, wm_ref, wt_ref, gq_ref, gk_ref, gsq_ref, gsk_ref, gik_ref,
                   fdq_ref, fsq_ref,
                   dq_ref, dk_ref, dv_ref, sq_ref, sk_ref, sv_ref, iq_ref, ikl_ref, ikh_ref, iw_ref):
    x = x_ref[0]
    tm = x.shape[0]
    ms = jnp.mean(x * x, axis=-1, keepdims=True)
    y = x * lax.rsqrt(ms + EPS) * g1_ref[...]
    mod = mod_ref[0]
    h = y * (1.0 + mod[1:2]) + mod[0:1]
    hb = h.astype(BF16)

    lane = lax.broadcasted_iota(I32, (tm, LANES), 1)
    is_head = lane < HEAD
    kpos = pl.program_id(1) * tm + lax.broadcasted_iota(I32, (tm, LANES), 0)
    hi_digit = (kpos // POS_RADIX).astype(F32)
    lo_digit = (kpos % POS_RADIX).astype(F32)
    kfeat = jnp.where(lane < HEAD + N_SPLIT, hi_digit, jnp.where(lane < HEAD + 2 * N_SPLIT, lo_digit, 0.0))

    def plain(seg_idx, out_ref):
        out_ref[0] = _dot(hb, wm_ref[:, seg_idx * SEG:(seg_idx + 1) * SEG]).astype(BF16)

    def normed(seg_idx, g_ref, scale, feat_ref, out_ref):
        half = SEG // 2
        for i in range(2):
            lo = seg_idx * SEG + i * half
            z = _dot(hb, wm_ref[:, lo:lo + half])
            ss = _group_sumsq(z)
            zn = z * lax.rsqrt(ss * (1.0 / HEAD) + EPS) * (g_ref[:, i * half:(i + 1) * half] * scale)
            for g in range(half // LANES):
                zg = zn[:, g * LANES:(g + 1) * LANES]
                for odd in range(2):
                    idx = i * (half // HEAD) + 2 * g + odd
                    src = zg if odd == 0 else pltpu.roll(zg, HEAD, 1)
                    feat = kfeat if feat_ref is None else feat_ref[idx:idx + 1, :]
                    out_ref[0, :, idx * LANES:(idx + 1) * LANES] = jnp.where(is_head, src, feat).astype(BF16)

    normed(0, gq_ref, HEAD ** -0.5 * LOG2E, fdq_ref, dq_ref)
    normed(1, gk_ref, 1.0, None, dk_ref)
    plain(2, dv_ref)
    normed(3, gsq_ref, HEAD ** -0.5 * LOG2E, fsq_ref, sq_ref)
    normed(4, gsk_ref, 1.0, None, sk_ref)
    plain(5, sv_ref)
    plain(6, iq_ref)

    t = _dot(hb, wt_ref[...])
    ikraw = jnp.where(lane < IDX_DIM, t, 0.0)
    ss = jnp.sum(ikraw * ikraw, axis=-1, keepdims=True) * (1.0 / IDX_DIM)
    ikn = ikraw * lax.rsqrt(ss + EPS) * gik_ref[...]
    ikl_ref[0] = ikn.astype(BF16)
    ikh_ref[0] = pltpu.roll(ikn, IDX_DIM, 1).astype(BF16)
    iwraw = jnp.where((lane >= IDX_DIM) & (lane < IDX_DIM + IDX_HEADS), t, 0.0)
    iw_ref[0] = pltpu.roll(iwraw * (IDX_HEADS ** -0.5), LANES - IDX_DIM, 1) * (IDX_DIM ** -0.5)


def _inproj(x, mod3, g1, wm, wt, gq, gk, gsq, gsk, gik, fdq, fsq, tm):
    B, S, D = x.shape
    ns = S // tm
    tok = lambda b, i: (b, i, 0)
    const2 = lambda b, i: (0, 0)
    seg_spec = pl.BlockSpec((1, tm, SEG), tok)
    wide_spec = pl.BlockSpec((1, tm, WIDE), tok)
    lane_spec = pl.BlockSpec((1, tm, LANES), tok)
    seg_shape = jax.ShapeDtypeStruct((B, S, SEG), BF16)
    wide_shape = jax.ShapeDtypeStruct((B, S, WIDE), BF16)
    return pl.pallas_call(
        _inproj_kernel,
        grid=(B, ns),
        in_specs=[pl.BlockSpec((1, tm, D), tok),
                  pl.BlockSpec((1, 6, D), lambda b, i: (b, 0, 0)),
                  pl.BlockSpec((1, D), const2),
                  pl.BlockSpec(wm.shape, const2),
                  pl.BlockSpec(wt.shape, const2),
                  pl.BlockSpec((1, SEG), const2), pl.BlockSpec((1, SEG), const2),
                  pl.BlockSpec((1, SEG), const2), pl.BlockSpec((1, SEG), const2),
                  pl.BlockSpec((1, LANES), const2),
                  pl.BlockSpec((N_MAPS, LANES), const2), pl.BlockSpec((N_MAPS, LANES), const2)],
        out_specs=[wide_spec, wide_spec, seg_spec, wide_spec, wide_spec, seg_spec, seg_spec,
                   lane_spec, lane_spec, lane_spec],
        out_shape=[wide_shape, wide_shape, seg_shape, wide_shape, wide_shape, seg_shape, seg_shape,
                   jax.ShapeDtypeStruct((B, S, LANES), BF16), jax.ShapeDtypeStruct((B, S, LANES), BF16),
                   jax.ShapeDtypeStruct((B, S, LANES), F32)],
        compiler_params=_params(("parallel", "parallel")),
        name="inproj",
    )(x, mod3, g1, wm, wt, gq, gk, gsq, gsk, gik, fdq, fsq)


def _softmax_update(s, v, m_ref, l_ref, idx, rows):
    reps = s.shape[1] // LANES
    m_old = m_ref[idx, rows, :]
    m_new = jnp.maximum(m_old, jnp.max(s, axis=-1, keepdims=True))
    p = jnp.exp2(s - _rep(m_new, reps))
    alpha = jnp.exp2(m_old - m_new)
    psum = p[:, 0:LANES]
    for g in range(1, reps):
        psum = psum + p[:, g * LANES:(g + 1) * LANES]
    l_ref[idx, rows, :] = alpha * l_ref[idx, rows, :] + psum
    m_ref[idx, rows, :] = m_new
    return alpha, _dot(p.astype(BF16), v)


def _diff_kernel(q_ref, k_ref, v_ref, lam_ref, g_ref, o_ref, m_ref, l_ref, acc_ref, *, tq, rq):
    qi = pl.program_id(1)
    ki = pl.program_id(2)

    @pl.when(ki == 0)
    def _():
        m_ref[...] = jnp.full(m_ref.shape, NEG_BIG, F32)
        l_ref[...] = jnp.zeros(l_ref.shape, F32)
        acc_ref[...] = jnp.zeros(acc_ref.shape, F32)

    def step(diag):
        for r0 in range(0, tq, rq):
            rows = slice(r0, r0 + rq)
            if diag:
                row = r0 + lax.broadcasted_iota(I32, (rq, tq), 0)
                col = lax.broadcasted_iota(I32, (rq, tq), 1)
                causal_bias = jnp.where(col <= row, 0.0, NEG_BIG)
            for idx in range(2 * DIFF_HEADS):
                h = idx // 2
                s = _nt(q_ref[0, rows, idx * LANES:(idx + 1) * LANES],
                        k_ref[0, :, idx * LANES:(idx + 1) * LANES])
                if diag:
                    s = s + causal_bias
                alpha, pv = _softmax_update(s, v_ref[0, :, h * LANES:(h + 1) * LANES], m_ref, l_ref, idx, rows)
                acc_ref[idx, rows, :] = alpha * acc_ref[idx, rows, :] + pv

    @pl.when(ki < qi)
    def _():
        step(False)

    @pl.when(ki == qi)
    def _():
        step(True)
        lv = lam_ref[...]
        lam = (jnp.exp(jnp.sum(lv[0:1] * lv[1:2], axis=-1, keepdims=True))
               - jnp.exp(jnp.sum(lv[2:3] * lv[3:4], axis=-1, keepdims=True)) + LAM_INIT)
        for h in range(DIFF_HEADS):
            o1 = acc_ref[2 * h] / jnp.sum(l_ref[2 * h], axis=-1, keepdims=True)
            o2 = acc_ref[2 * h + 1] / jnp.sum(l_ref[2 * h + 1], axis=-1, keepdims=True)
            o = o1 - lam * o2
            ms = jnp.mean(o * o, axis=-1, keepdims=True)
            on = o * lax.rsqrt(ms + EPS) * g_ref[...]
            o_ref[0, :, h * LANES:(h + 1) * LANES] = (on * (1.0 - LAM_INIT)).astype(BF16)


def _diff_attention(dq, dk, dv, lamv, subln_g, tq, rq):
    B, S, W = dq.shape
    V = dv.shape[2]
    nq = S // tq
    n_maps = 2 * DIFF_HEADS
    return pl.pallas_call(
        functools.partial(_diff_kernel, tq=tq, rq=rq),
        grid=(B, nq, nq),
        in_specs=[pl.BlockSpec((1, tq, W), lambda b, i, j: (b, i, 0)),
                  pl.BlockSpec((1, tq, W), lambda b, i, j: (b, jnp.minimum(i, j), 0)),
                  pl.BlockSpec((1, tq, V), lambda b, i, j: (b, jnp.minimum(i, j), 0)),
                  pl.BlockSpec(lamv.shape, lambda b, i, j: (0, 0)),
                  pl.BlockSpec((1, LANES), lambda b, i, j: (0, 0))],
        out_specs=pl.BlockSpec((1, tq, V), lambda b, i, j: (b, i, 0)),
        out_shape=jax.ShapeDtypeStruct((B, S, V), BF16),
        scratch_shapes=[pltpu.VMEM((n_maps, tq, LANES), F32),
                        pltpu.VMEM((n_maps, tq, LANES), F32),
                        pltpu.VMEM((n_maps, tq, LANES), F32)],
        compiler_params=_params(("parallel", "parallel", "arbitrary")),
        name="diff_attention",
    )(dq, dk, dv, lamv, subln_g)


def _dsa_kernel(iq_ref, iw_ref, ikl_ref, ikh_ref, q_ref, k_ref, v_ref, o_ref,
                key_ref, thr_ref, nties_ref, mb_ref, m_ref, l_ref, acc_ref, *, tq, tk, rs, rq, topk):
    qi = pl.program_id(1)
    q0 = qi * tq
    nkc = (q0 + tq + tk - 1) // tk
    row = q0 + lax.broadcasted_iota(I32, (tq, tk), 0)
    col0 = lax.broadcasted_iota(I32, (tq, tk), 1)
    low = lax.broadcasted_iota(I32, (tq, LANES), 1) < HEAD
    reps = tk // LANES
    iw = iw_ref[0]

    def score_chunk(kc, carry):
        k0 = pl.multiple_of(kc * tk, tk)
        ikl = ikl_ref[0, pl.ds(k0, tk), :]
        ikh = ikh_ref[0, pl.ds(k0, tk), :]
        sc = jnp.zeros((tq, tk), F32)
        for j in range(IDX_HEADS // 2):
            iqp = iq_ref[0, :, j * LANES:(j + 1) * LANES]
            sc = sc + iw[:, 2 * j:2 * j + 1] * jnp.maximum(_nt(iqp, ikl), 0.0)
            sc = sc + iw[:, 2 * j + 1:2 * j + 2] * jnp.maximum(_nt(iqp, ikh), 0.0)
        sc = jnp.where(sc == 0.0, 0.0, sc)
        sc = jnp.where(col0 + k0 <= row, sc, -jnp.inf)
        bits = lax.bitcast_convert_type(sc, I32)
        key_ref[:, pl.ds(k0, tk)] = bits ^ ((bits >> 31) & 0x7FFFFFFF)
        return carry

    lax.fori_loop(0, nkc, score_chunk, 0)

    groups = [slice(r0, r0 + rs) for r0 in range(0, tq, rs)]

    def count(thrs, strict):
        accs = []
        for rows, thr in zip(groups, thrs):
            thr_t = _rep(thr, reps)

            def body(kc, acc, rows=rows, thr_t=thr_t):
                k0 = pl.multiple_of(kc * tk, tk)
                keyc = key_ref[rows, pl.ds(k0, tk)]
                hit = jnp.where((keyc > thr_t) if strict else (keyc >= thr_t), 1.0, 0.0)
                for g in range(reps):
                    acc = acc + hit[:, g * LANES:(g + 1) * LANES]
                return acc

            accs.append(lax.fori_loop(0, nkc, body, jnp.zeros((rs, LANES), F32)))
        return [jnp.sum(acc, axis=-1, keepdims=True) for acc in accs]

    def bit_step(i, tus):
        bit = jnp.left_shift(jnp.int32(1), 31 - i)
        cands = [tu | bit for tu in tus]
        cnts = count([c ^ INT_MIN for c in cands], False)
        return tuple(jnp.where(cnt >= float(topk), c, tu) for cnt, c, tu in zip(cnts, cands, tus))

    tus = lax.fori_loop(0, 32, bit_step, tuple(jnp.zeros((rs, LANES), I32) for _ in groups))
    thrs = [tu ^ INT_MIN for tu in tus]
    for rows, thr, n_gt in zip(groups, thrs, count(thrs, True)):
        thr_ref[rows, :] = thr
        nties_ref[rows, :] = jnp.broadcast_to(float(topk) - n_gt, (rs, LANES))

    m_ref[...] = jnp.full(m_ref.shape, NEG_BIG, F32)
    l_ref[...] = jnp.zeros(l_ref.shape, F32)
    acc_ref[...] = jnp.zeros(acc_ref.shape, F32)
    tri = jnp.where(lax.broadcasted_iota(I32, (tk, tk), 0) < lax.broadcasted_iota(I32, (tk, tk), 1),
                    1.0, 0.0).astype(BF16)

    def attend_chunk(kc, ties_before):
        k0 = pl.multiple_of(kc * tk, tk)
        keyc = key_ref[:, pl.ds(k0, tk)]
        thr_t = _rep(thr_ref[...], reps)
        eq = jnp.where(keyc == thr_t, 1.0, 0.0)
        rank = _dot(eq.astype(BF16), tri) + _rep(ties_before, reps)
        take = jnp.where(keyc > thr_t, 1.0, jnp.where(rank < _rep(nties_ref[...], reps), eq, 0.0))
        mb_ref[...] = jnp.where(col0 + k0 <= row, jnp.where(take > 0.5, 0.0, NEG_BIG), NEG_BIG)
        low_r = lax.broadcasted_iota(I32, (rq, LANES), 1) < HEAD
        for r0 in range(0, tq, rq):
            rows = slice(r0, r0 + rq)
            mb = mb_ref[rows, :]
            for j in range(DSA_HEADS // 2):
                vp = v_ref[0, pl.ds(k0, tk), j * LANES:(j + 1) * LANES]
                vlow = lax.broadcasted_iota(I32, vp.shape, 1) < HEAD
                pv = jnp.zeros((rq, LANES), F32)
                alphas = []
                for a in range(2):
                    hd = 2 * j + a
                    s = _nt(q_ref[0, rows, hd * LANES:(hd + 1) * LANES],
                            k_ref[0, pl.ds(k0, tk), hd * LANES:(hd + 1) * LANES]) + mb
                    va = jnp.where(vlow if a == 0 else jnp.logical_not(vlow), vp, jnp.zeros_like(vp))
                    alpha, pva = _softmax_update(s, va, m_ref, l_ref, hd, rows)
                    pv = pv + pva
                    alphas.append(alpha)
                acc_ref[j, rows, :] = acc_ref[j, rows, :] * jnp.where(low_r, alphas[0], alphas[1]) + pv
        psum = eq[:, 0:LANES]
        for g in range(1, reps):
            psum = psum + eq[:, g * LANES:(g + 1) * LANES]
        return ties_before + jnp.sum(psum, axis=-1, keepdims=True)

    lax.fori_loop(0, nkc, attend_chunk, jnp.zeros((tq, LANES), F32))

    for j in range(DSA_HEADS // 2):
        la = jnp.sum(l_ref[2 * j], axis=-1, keepdims=True)
        lb = jnp.sum(l_ref[2 * j + 1], axis=-1, keepdims=True)
        o_ref[0, :, j * LANES:(j + 1) * LANES] = (acc_ref[j] * jnp.where(low, 1.0 / la, 1.0 / lb)).astype(BF16)


def _dsa_attention(iq, iw, ikl, ikh, sq, sk, sv, tq, tk, rs, rq, topk):
    B, S, W = sq.shape
    V = sv.shape[2]
    nq = S // tq
    qblk = lambda b, i: (b, i, 0)
    full = lambda b, i: (b, 0, 0)
    once = pl.Buffered(1)
    return pl.pallas_call(
        functools.partial(_dsa_kernel, tq=tq, tk=tk, rs=rs, rq=rq, topk=topk),
        grid=(B, nq),
        in_specs=[pl.BlockSpec((1, tq, iq.shape[2]), qblk),
                  pl.BlockSpec((1, tq, LANES), qblk),
                  pl.BlockSpec((1, S, LANES), full, pipeline_mode=once),
                  pl.BlockSpec((1, S, LANES), full, pipeline_mode=once),
                  pl.BlockSpec((1, tq, W), qblk),
                  pl.BlockSpec((1, S, W), full, pipeline_mode=once),
                  pl.BlockSpec((1, S, V), full, pipeline_mode=once)],
        out_specs=pl.BlockSpec((1, tq, V), qblk),
        out_shape=jax.ShapeDtypeStruct((B, S, V), BF16),
        scratch_shapes=[pltpu.VMEM((tq, S), I32),
                        pltpu.VMEM((tq, LANES), I32),
                        pltpu.VMEM((tq, LANES), F32),
                        pltpu.VMEM((tq, tk), F32),
                        pltpu.VMEM((DSA_HEADS, tq, LANES), F32),
                        pltpu.VMEM((DSA_HEADS, tq, LANES), F32),
                        pltpu.VMEM((DSA_HEADS // 2, tq, LANES), F32)],
        compiler_params=_params(("parallel", "arbitrary")),
        name="dsa_attention",
    )(iq, iw, ikl, ikh, sq, sk, sv)


def _mix_kernel(do_ref, so_ref, x_ref, mod_ref, g2_ref, wo1_ref, wo2_ref, wsg_ref, wsu_ref, wsd_ref,
                rwt_ref, base_ref, h2_ref, lg_ref):
    mix = _dot(do_ref[0], wo1_ref[...]) + _dot(so_ref[0], wo2_ref[...])
    mod = mod_ref[0]
    x1 = x_ref[0] + mod[2:3] * mix
    ms = jnp.mean(x1 * x1, axis=-1, keepdims=True)
    h2 = x1 * lax.rsqrt(ms + EPS) * g2_ref[...] * (1.0 + mod[4:5]) + mod[3:4]
    hb = h2.astype(BF16)
    gate = _dot(hb, wsg_ref[...])
    up = _dot(hb, wsu_ref[...])
    act = gate / (1.0 + jnp.exp(-gate)) * up
    shared = _dot(act.astype(BF16), wsd_ref[...])
    base_ref[0] = x1 + mod[5:6] * shared
    lg_ref[...] = _nt(rwt_ref[...], hb)
    hf = hb.astype(F32)
    tm = hf.shape[0]
    for j in range(ROW_TILES):
        h2_ref[pl.ds(j, tm, stride=ROW_TILES), :] = hf[:, j * LANES:(j + 1) * LANES]


def _mix(diff_out, dsa_out, x, mod3, g2, wo1, wo2, wsg, wsu, wsd, rwt, tm):
    B, S, D = x.shape
    ns = S // tm
    T = B * S
    tok = lambda b, i: (b, i, 0)
    c2 = lambda b, i: (0, 0)
    return pl.pallas_call(
        _mix_kernel,
        grid=(B, ns),
        in_specs=[pl.BlockSpec((1, tm, SEG), tok), pl.BlockSpec((1, tm, SEG), tok),
                  pl.BlockSpec((1, tm, D), tok),
                  pl.BlockSpec((1, 6, D), lambda b, i: (b, 0, 0)),
                  pl.BlockSpec((1, D), c2),
                  pl.BlockSpec(wo1.shape, c2), pl.BlockSpec(wo2.shape, c2),
                  pl.BlockSpec(wsg.shape, c2), pl.BlockSpec(wsu.shape, c2), pl.BlockSpec(wsd.shape, c2),
                  pl.BlockSpec(rwt.shape, c2)],
        out_specs=[pl.BlockSpec((1, tm, D), tok),
                   pl.BlockSpec((tm * ROW_TILES, LANES), lambda b, i: (b * ns + i, 0)),
                   pl.BlockSpec((N_EXPERTS, tm), lambda b, i: (0, b * ns + i))],
        out_shape=[jax.ShapeDtypeStruct((B, S, D), F32),
                   jax.ShapeDtypeStruct((T * ROW_TILES, LANES), F32),
                   jax.ShapeDtypeStruct((N_EXPERTS, T), F32)],
        compiler_params=_params(("parallel", "parallel")),
        name="mix_shared_router",
    )(diff_out, dsa_out, x, mod3, g2, wo1, wo2, wsg, wsu, wsd, rwt)


def _first_max(v, idx, sentinel):
    m = jnp.max(v, axis=0, keepdims=True)
    i = jnp.min(jnp.where(v == m, idx, sentinel), axis=0, keepdims=True)
    return m, i


def _route_kernel(lg_ref, bias_ref, eidx_ref, gate_ref):
    lg = lg_ref[...]
    tt = lg.shape[1]
    scores = 1.0 / (1.0 + jnp.exp(-lg))
    biased = scores + bias_ref[...]
    gi = lax.broadcasted_iota(I32, (GROUP_SIZE, tt), 0).astype(F32)
    gscore = []
    for g in range(N_GROUPS):
        blk = biased[g * GROUP_SIZE:(g + 1) * GROUP_SIZE, :]
        m1, i1 = _first_max(blk, gi, float(GROUP_SIZE))
        m2 = jnp.max(jnp.where(gi == i1, -jnp.inf, blk), axis=0, keepdims=True)
        gscore.append(m1 + m2)
    gs = jnp.concatenate(gscore, axis=0)
    gidx = lax.broadcasted_iota(I32, (N_GROUPS, tt), 0).astype(F32)
    chosen = jnp.zeros((N_GROUPS, tt), F32)
    for _ in range(TOPK_GROUPS):
        _, ig = _first_max(gs, gidx, float(N_GROUPS))
        hit = gidx == ig
        chosen = jnp.where(hit, 1.0, chosen)
        gs = jnp.where(hit, -jnp.inf, gs)
    masked = jnp.concatenate(
        [jnp.where(chosen[g:g + 1, :] > 0.5, biased[g * GROUP_SIZE:(g + 1) * GROUP_SIZE, :], -jnp.inf)
         for g in range(N_GROUPS)], axis=0)
    ei = lax.broadcasted_iota(I32, (N_EXPERTS, tt), 0).astype(F32)
    ids, ws = [], []
    for _ in range(TOP_K):
        _, ie = _first_max(masked, ei, float(N_EXPERTS))
        hit = ei == ie
        ws.append(jnp.sum(jnp.where(hit, scores, 0.0), axis=0, keepdims=True))
        ids.append(ie)
        masked = jnp.where(hit, -jnp.inf, masked)
    w = jnp.concatenate(ws, axis=0)
    gate_ref[...] = w / jnp.sum(w, axis=0, keepdims=True) * ROUTED_SCALE
    eidx_ref[...] = jnp.concatenate(ids, axis=0).astype(I32)


def _route(logits_t, bias_col, tt):
    E, T = logits_t.shape
    return pl.pallas_call(
        _route_kernel,
        grid=(T // tt,),
        in_specs=[pl.BlockSpec((E, tt), lambda i: (0, i)),
                  pl.BlockSpec((E, 1), lambda i: (0, 0))],
        out_specs=[pl.BlockSpec((TOP_K, tt), lambda i: (0, i)),
                   pl.BlockSpec((TOP_K, tt), lambda i: (0, i))],
        out_shape=[jax.ShapeDtypeStruct((TOP_K, T), I32),
                   jax.ShapeDtypeStruct((TOP_K, T), F32)],
        compiler_params=_params(("parallel",)),
        name="route",
    )(logits_t, bias_col)


def _plan_kernel(eidx_ref, dest_ref, bexp_ref, nused_ref, cnt_col, cnt_row, slot_base, *, blk, nb_pad):
    ph = pl.program_id(0)
    i = pl.program_id(1)
    tt = eidx_ref.shape[1]
    eidx = eidx_ref[...]
    ei = lax.broadcasted_iota(I32, (N_EXPERTS, tt), 0)
    onehot = jnp.zeros((N_EXPERTS, tt), F32)
    for k in range(TOP_K):
        onehot = onehot + jnp.where(ei == eidx[k:k + 1, :], 1.0, 0.0)
    oh = onehot.astype(BF16)

    @pl.when((ph == 0) & (i == 0))
    def _():
        cnt_col[...] = jnp.zeros(cnt_col.shape, F32)
        cnt_row[...] = jnp.zeros(cnt_row.shape, F32)

    @pl.when(ph == 0)
    def _():
        cnt_col[...] += _dot(oh, jnp.ones((tt, LANES), BF16))
        cnt_row[...] += _nt(jnp.ones((8, tt), BF16), oh)

    @pl.when((ph == 1) & (i == 0))
    def _():
        inv = 1.0 / blk
        nb_col = jnp.floor((cnt_col[:, 0:1] + (blk - 1)) * inv)
        nb_row = jnp.floor((cnt_row[0:1, :] + (blk - 1)) * inv)
        r = lax.broadcasted_iota(I32, (N_EXPERTS, N_EXPERTS), 0)
        c = lax.broadcasted_iota(I32, (N_EXPERTS, N_EXPERTS), 1)
        bstart = jnp.sum(jnp.where(c < r, nb_row, 0.0), axis=-1, keepdims=True)
        bend = bstart + nb_col
        slot_base[...] = bstart * blk
        jb = lax.broadcasted_iota(I32, (N_EXPERTS, nb_pad), 1).astype(F32)
        be = jnp.sum(jnp.where(bend <= jb, 1.0, 0.0), axis=0, keepdims=True)
        bexp_ref[...] = jnp.minimum(be, N_EXPERTS - 1.0).astype(I32)
        nused_ref[...] = jnp.broadcast_to(jnp.sum(nb_row, axis=-1, keepdims=True), nused_ref.shape).astype(I32)

    @pl.when(ph == 1)
    def _():
        tri = jnp.where(lax.broadcasted_iota(I32, (tt, tt), 0) < lax.broadcasted_iota(I32, (tt, tt), 1),
                        1.0, 0.0).astype(BF16)
        slot = _dot(oh, tri) + slot_base[...]
        for k in range(TOP_K):
            dk = jnp.sum(jnp.where(ei == eidx[k:k + 1, :], slot, 0.0), axis=0, keepdims=True)
            dest_ref[k:k + 1, :] = dk.astype(I32)
        slot_base[...] += jnp.sum(onehot, axis=-1, keepdims=True)


def _plan(eidx, tt, blk, nb_pad):
    K, T = eidx.shape
    nt = T // tt
    return pl.pallas_call(
        functools.partial(_plan_kernel, blk=blk, nb_pad=nb_pad),
        grid=(2, nt),
        in_specs=[pl.BlockSpec((K, tt), lambda p, i: (0, i))],
        out_specs=[pl.BlockSpec((K, tt), lambda p, i: (0, i * p)),
                   pl.BlockSpec((1, nb_pad), lambda p, i: (0, 0)),
                   pl.BlockSpec((1, LANES), lambda p, i: (0, 0))],
        out_shape=[jax.ShapeDtypeStruct((K, T), I32),
                   jax.ShapeDtypeStruct((1, nb_pad), I32),
                   jax.ShapeDtypeStruct((1, LANES), I32)],
        scratch_shapes=[pltpu.VMEM((N_EXPERTS, LANES), F32),
                        pltpu.VMEM((8, N_EXPERTS), F32),
                        pltpu.VMEM((N_EXPERTS, 1), F32)],
        compiler_params=_params(("arbitrary", "arbitrary")),
        name="plan",
    )(eidx)


def _row(ref, r):
    return ref.at[pl.ds(pl.multiple_of(r * ROW_TILES, ROW_TILES), ROW_TILES), :]


def _dispatch_kernel(dest_ref, h_ref, xs_in_ref, xs_ref, sem):
    del xs_in_ref
    tt = h_ref.shape[0] // ROW_TILES

    def issue(t, c):
        for k in range(TOP_K):
            pltpu.make_async_copy(_row(h_ref, t), _row(xs_ref, dest_ref[k, t]), sem).start()
        return c

    lax.fori_loop(0, tt, issue, 0)

    def drain(t, c):
        for k in range(TOP_K):
            pltpu.make_async_copy(_row(h_ref, 0), _row(xs_ref, 0), sem).wait()
        return c

    lax.fori_loop(0, tt, drain, 0)


def _dispatch(dest, h2rows, xs_init, tt):
    T = h2rows.shape[0] // ROW_TILES
    return pl.pallas_call(
        _dispatch_kernel,
        grid=(T // tt,),
        in_specs=[pl.BlockSpec((TOP_K, tt), lambda i: (0, i), memory_space=pltpu.SMEM),
                  pl.BlockSpec((tt * ROW_TILES, LANES), lambda i: (i, 0)),
                  pl.BlockSpec(memory_space=pl.ANY)],
        out_specs=pl.BlockSpec(memory_space=pl.ANY),
        out_shape=jax.ShapeDtypeStruct(xs_init.shape, xs_init.dtype),
        scratch_shapes=[pltpu.SemaphoreType.DMA(())],
        input_output_aliases={2: 0},
        compiler_params=_params(("arbitrary",), has_side_effects=True),
        name="dispatch",
    )(dest, h2rows, xs_init)


def _experts_kernel(bexp_ref, nused_ref, xs_ref, wg_ref, wu_ref, wd_ref, y_ref, xb_ref):
    del bexp_ref

    @pl.when(pl.program_id(0) < nused_ref[0])
    def _():
        blk = xb_ref.shape[0]
        for j in range(ROW_TILES):
            xb_ref[:, j * LANES:(j + 1) * LANES] = xs_ref[pl.ds(j, blk, stride=ROW_TILES), :].astype(BF16)
        xb = xb_ref[...]
        gate = _dot(xb, wg_ref[0])
        up = _dot(xb, wu_ref[0])
        act = gate / (1.0 + jnp.exp(-gate)) * up
        y = _dot(act.astype(BF16), wd_ref[0])
        for j in range(ROW_TILES):
            y_ref[pl.ds(j, blk, stride=ROW_TILES), :] = y[:, j * LANES:(j + 1) * LANES]


def _experts(bexp, nused, xs, wg, wu, wd, blk, n_blocks):
    live = lambda j, be, nu: jnp.minimum(j, nu[0] - 1)
    row_spec = pl.BlockSpec((blk * ROW_TILES, LANES), lambda j, be, nu: (live(j, be, nu), 0))
    wspec = lambda w: pl.BlockSpec((1,) + w.shape[1:], lambda j, be, nu: (be[live(j, be, nu)], 0, 0))
    return pl.pallas_call(
        _experts_kernel,
        grid_spec=pltpu.PrefetchScalarGridSpec(
            num_scalar_prefetch=2,
            grid=(n_blocks,),
            in_specs=[row_spec, wspec(wg), wspec(wu), wspec(wd)],
            out_specs=row_spec,
            scratch_shapes=[pltpu.VMEM((blk, D_MODEL), BF16)]),
        out_shape=jax.ShapeDtypeStruct(xs.shape, F32),
        compiler_params=_params(("arbitrary",)),
        name="experts",
    )(bexp, nused, xs, wg, wu, wd)


def _combine_kernel(dest_ref, gate_ref, base_ref, mod_ref, y_ref, o_ref, buf, sem):
    tt = base_ref.shape[1]

    def issue(t, c):
        for k in range(TOP_K):
            pltpu.make_async_copy(_row(y_ref, dest_ref[k, t]), _row(buf, k * tt + t), sem).start()
        return c

    lax.fori_loop(0, tt, issue, 0)

    def drain(t, c):
        for k in range(TOP_K):
            pltpu.make_async_copy(_row(y_ref, 0), _row(buf, 0), sem).wait()
        return c

    lax.fori_loop(0, tt, drain, 0)

    gates = gate_ref[...]
    g2 = mod_ref[0][5:6]
    for j in range(D_MODEL // LANES):
        acc = jnp.zeros((tt, LANES), F32)
        for k in range(TOP_K):
            acc = acc + gates[:, k:k + 1] * buf[pl.ds(k * tt * ROW_TILES + j, tt, stride=ROW_TILES), :]
        cols = slice(j * LANES, (j + 1) * LANES)
        o_ref[0, :, cols] = base_ref[0, :, cols] + g2[:, cols] * acc


def _combine(dest, gates_tk, base, mod3, y, tt):
    B, S, D = base.shape
    ns = S // tt
    return pl.pallas_call(
        _combine_kernel,
        grid=(B, ns),
        in_specs=[pl.BlockSpec((TOP_K, tt), lambda b, i: (0, b * ns + i), memory_space=pltpu.SMEM),
                  pl.BlockSpec((tt, TOP_K), lambda b, i: (b * ns + i, 0)),
                  pl.BlockSpec((1, tt, D), lambda b, i: (b, i, 0)),
                  pl.BlockSpec((1, 6, D), lambda b, i: (b, 0, 0)),
                  pl.BlockSpec(memory_space=pl.ANY)],
        out_specs=pl.BlockSpec((1, tt, D), lambda b, i: (b, i, 0)),
        out_shape=jax.ShapeDtypeStruct((B, S, D), F32),
        scratch_shapes=[pltpu.VMEM((TOP_K * tt * ROW_TILES, LANES), F32),
                        pltpu.SemaphoreType.DMA(())],
        compiler_params=_params(("arbitrary", "arbitrary")),
        name="combine",
    )(dest, gates_tk, base, mod3, y)


def _alibi_q_features(n_heads, maps_per_head):
    slopes = 2.0 ** (-8.0 * jnp.arange(1, n_heads + 1, dtype=F32) / n_heads)
    c = jnp.repeat(slopes, maps_per_head) * LOG2E * POS_RADIX
    pieces = []
    rest = c
    for _ in range(N_SPLIT):
        p = rest.astype(BF16).astype(F32)
        pieces.append(p)
        rest = rest - p
    hi = jnp.stack(pieces, axis=1)
    feat = jnp.concatenate([hi, hi / POS_RADIX], axis=1)
    return jnp.pad(feat, ((0, 0), (HEAD, LANES - HEAD - 2 * N_SPLIT)))


def kernel(x, c, ada_w, ada_b, norm1_g, norm2_g, w_in, diff_q_norm_g, diff_k_norm_g, lam_q1, lam_k1, lam_q2, lam_k2, diff_subln_g, dsa_q_norm_g, dsa_k_norm_g, idx_k_norm_g, w_out, router_w, router_bias, exp_w_gate, exp_w_up, exp_w_down, shared_w_gate, shared_w_up, shared_w_down):
    B, S, D = x.shape
    assert D == D_MODEL and ada_w.shape[0] == 1 and S <= POS_RADIX * 128
    T = B * S
    topk = min(DSA_TOPK, S // 4)
    tm = min(512, S)
    tq_diff = min(512, S)
    rq_diff = min(512, S)
    tq_dsa = min(256, S)
    tk_dsa = min(512, S)
    rs_dsa = 64
    rq_dsa = min(256, S)
    tt_route = min(512, T)
    tt_move = min(128, S)
    blk = 512
    n_blocks = (T * TOP_K) // blk + N_EXPERTS
    nb_pad = -(-n_blocks // LANES) * LANES

    n_main = 7 * SEG
    wm = w_in[0, :, :n_main].astype(BF16)
    wt = jnp.pad(w_in[0, :, n_main:], ((0, 0), (0, LANES - (IDX_DIM + IDX_HEADS)))).astype(BF16)
    tile8 = lambda g: jnp.tile(g[0], SEG // g.shape[1]).reshape(1, SEG)
    gik = jnp.pad(idx_k_norm_g[0], (0, LANES - IDX_DIM)).reshape(1, LANES)
    lamv = jnp.concatenate([lam_q1, lam_k1, lam_q2, lam_k2], axis=0)
    wo1 = w_out[0, :SEG].astype(BF16)
    wo2 = w_out[0, SEG:].astype(BF16)
    rwt = router_w[0].T.astype(BF16)
    wg = exp_w_gate[0].astype(BF16)
    wu = exp_w_up[0].astype(BF16)
    wd = exp_w_down[0].astype(BF16)

    mod3 = _ada(c, ada_w[0], ada_b[0]).reshape(B, 6, D)

    dq, dk, dv, sq, sk, sv, iq, ikl, ikh, iw = _inproj(
        x, mod3, norm1_g, wm, wt, tile8(diff_q_norm_g), tile8(diff_k_norm_g),
        tile8(dsa_q_norm_g), tile8(dsa_k_norm_g), gik,
        _alibi_q_features(DIFF_HEADS, 2), _alibi_q_features(DSA_HEADS, 1), tm)

    diff_out = _diff_attention(dq, dk, dv, lamv, diff_subln_g, tq_diff, rq_diff)
    dsa_out = _dsa_attention(iq, iw, ikl, ikh, sq, sk, sv, tq_dsa, tk_dsa, rs_dsa, rq_dsa, topk)

    base, h2rows, logits_t = _mix(diff_out, dsa_out, x, mod3, norm2_g, wo1, wo2,
                                  shared_w_gate[0].astype(BF16), shared_w_up[0].astype(BF16),
                                  shared_w_down[0].astype(BF16), rwt, tm)

    eidx, gates = _route(logits_t, router_bias[0].reshape(N_EXPERTS, 1), tt_route)
    dest, bexp, nused = _plan(eidx, tt_route, blk, nb_pad)

    xs = _dispatch(dest, h2rows, jnp.zeros((n_blocks * blk * ROW_TILES, LANES), F32), tt_move)
    y = _experts(bexp.reshape(nb_pad), nused[0, :1], xs, wg, wu, wd, blk, n_blocks)
    return _combine(dest, gates.T, base, mod3, y, tt_move)
```

```python
import functools
import math

import jax
import jax.numpy as jnp
from jax import lax
from jax.experimental import pallas as pl
from jax.experimental.pallas import tpu as pltpu

F32 = jnp.float32
BF16 = jnp.bfloat16
I32 = jnp.int32

D_MODEL = 1024
DIFF_HEADS = 4
DIFF_HEAD_DIM = 64
DSA_HEADS = 8
DSA_HEAD_DIM = 64
IDX_HEADS = 8
IDX_DIM = 64
DSA_TOPK = 256
N_EXPERTS = 256
TOP_K = 8
N_GROUPS = 8
GROUP_SIZE = N_EXPERTS // N_GROUPS
TOPK_GROUPS = 4
D_EXPERT = 256
D_SHARED = 256
ROUTED_SCALE = 2.5
EPS = 1e-6
LAM_INIT = 0.2

LANES = 128
ROW_TILES = D_MODEL // LANES
SEG = 512
HEAD = 64
N_MAPS = SEG // HEAD
WIDE = N_MAPS * LANES
POS_RADIX = 64
N_SPLIT = 3
NEG_BIG = -1e30
INT_MIN = -2147483648
LOG2E = math.log2(math.e)
VMEM_LIMIT = 56 * 1024 * 1024

NT_DIMS = (((1,), (1,)), ((), ()))


def _nt(a, b):
    return lax.dot_general(a, b, NT_DIMS, preferred_element_type=F32)


def _dot(a, b):
    return jnp.dot(a, b, preferred_element_type=F32)


def _rep(x, reps):
    return jnp.concatenate([x] * reps, axis=1)


def _params(sem, vmem=VMEM_LIMIT, **kw):
    return pltpu.CompilerParams(dimension_semantics=sem, vmem_limit_bytes=vmem, **kw)


def _ada_kernel(c_ref, w_ref, b_ref, o_ref):
    c = c_ref[...]
    s = c / (1.0 + jnp.exp(-c))
    o_ref[...] = jnp.dot(s, w_ref[...], preferred_element_type=F32,
                         precision=lax.Precision.HIGHEST) + b_ref[...]


def _ada(c, w, b):
    B, D = c.shape
    N = w.shape[1]
    tn = D
    return pl.pallas_call(
        _ada_kernel,
        grid=(N // tn,),
        in_specs=[pl.BlockSpec((B, D), lambda j: (0, 0)),
                  pl.BlockSpec((D, tn), lambda j: (0, j)),
                  pl.BlockSpec((1, tn), lambda j: (0, j))],
        out_specs=pl.BlockSpec((B, tn), lambda j: (0, j)),
        out_shape=jax.ShapeDtypeStruct((B, N), F32),
        compiler_params=_params(("arbitrary",)),
        name="ada",
    )(c, w, b.reshape(1, N))


def _group_sumsq(z):
    n = z.shape[1]
    r = lax.broadcasted_iota(I32, (n, n), 0) // HEAD
    c = lax.broadcasted_iota(I32, (n, n), 1) // HEAD
    bd = jnp.where(r == c, 1.0, 0.0).astype(BF16)
    zz = z * z
    hi = zz.astype(BF16)
    lo = (zz - hi.astype(F32)).astype(BF16)
    return _dot(hi, bd) + _dot(lo, bd)


def _inproj_kernel(x_ref, mod_ref, g1_ref, wm_ref, wt_ref, gq_ref, gk_ref, gsq_ref, gsk_ref, gik_ref,
                   fdq_ref, fsq_ref,
                   dq_ref, dk_ref, dv_ref, sq_ref, sk_ref, sv_ref, iq_ref, ikl_ref, ikh_ref, iw_ref):
    x = x_ref[0]
    tm = x.shape[0]
    ms = jnp.mean(x * x, axis=-1, keepdims=True)
    y = x * lax.rsqrt(ms + EPS) * g1_ref[...]
    mod = mod_ref[0]
    h = y * (1.0 + mod[1:2]) + mod[0:1]
    hb = h.astype(BF16)

    lane = lax.broadcasted_iota(I32, (tm, LANES), 1)
    is_head = lane < HEAD
    kpos = pl.program_id(1) * tm + lax.broadcasted_iota(I32, (tm, LANES), 0)
    hi_digit = (kpos // POS_RADIX).astype(F32)
    lo_digit = (kpos % POS_RADIX).astype(F32)
    kfeat = jnp.where(lane < HEAD + N_SPLIT, hi_digit, jnp.where(lane < HEAD + 2 * N_SPLIT, lo_digit, 0.0))

    def plain(seg_idx, out_ref):
        out_ref[0] = _dot(hb, wm_ref[:, seg_idx * SEG:(seg_idx + 1) * SEG]).astype(BF16)

    def normed(seg_idx, g_ref, scale, feat_ref, out_ref):
        half = SEG // 2
        for i in range(2):
            lo = seg_idx * SEG + i * half
            z = _dot(hb, wm_ref[:, lo:lo + half])
            ss = _group_sumsq(z)
            zn = z * lax.rsqrt(ss * (1.0 / HEAD) + EPS) * (g_ref[:, i * half:(i + 1) * half] * scale)
            for g in range(half // LANES):
                zg = zn[:, g * LANES:(g + 1) * LANES]
                for odd in range(2):
                    idx = i * (half // HEAD) + 2 * g + odd
                    src = zg if odd == 0 else pltpu.roll(zg, HEAD, 1)
                    feat = kfeat if feat_ref is None else feat_ref[idx:idx + 1, :]
                    out_ref[0, :, idx * LANES:(idx + 1) * LANES] = jnp.where(is_head, src, feat).astype(BF16)

    normed(0, gq_ref, HEAD ** -0.5 * LOG2E, fdq_ref, dq_ref)
    normed(1, gk_ref, 1.0, None, dk_ref)
    plain(2, dv_ref)
    normed(3, gsq_ref, HEAD ** -0.5 * LOG2E, fsq_ref, sq_ref)
    normed(4, gsk_ref, 1.0, None, sk_ref)
    plain(5, sv_ref)
    plain(6, iq_ref)

    t = _dot(hb, wt_ref[...])
    ikraw = jnp.where(lane < IDX_DIM, t, 0.0)
    ss = jnp.sum(ikraw * ikraw, axis=-1, keepdims=True) * (1.0 / IDX_DIM)
    ikn = ikraw * lax.rsqrt(ss + EPS) * gik_ref[...]
    ikl_ref[0] = ikn.astype(BF16)
    ikh_ref[0] = pltpu.roll(ikn, IDX_DIM, 1).astype(BF16)
    iwraw = jnp.where((lane >= IDX_DIM) & (lane < IDX_DIM + IDX_HEADS), t, 0.0)
    iw_ref[0] = pltpu.roll(iwraw * (IDX_HEADS ** -0.5), LANES - IDX_DIM, 1) * (IDX_DIM ** -0.5)


def _inproj(x, mod3, g1, wm, wt, gq, gk, gsq, gsk, gik, fdq, fsq, tm):
    B, S, D = x.shape
    ns = S // tm
    tok = lambda b, i: (b, i, 0)
    const2 = lambda b, i: (0, 0)
    seg_spec = pl.BlockSpec((1, tm, SEG), tok)
    wide_spec = pl.BlockSpec((1, tm, WIDE), tok)
    lane_spec = pl.BlockSpec((1, tm, LANES), tok)
    seg_shape = jax.ShapeDtypeStruct((B, S, SEG), BF16)
    wide_shape = jax.ShapeDtypeStruct((B, S, WIDE), BF16)
    return pl.pallas_call(
        _inproj_kernel,
        grid=(B, ns),
        in_specs=[pl.BlockSpec((1, tm, D), tok),
                  pl.BlockSpec((1, 6, D), lambda b, i: (b, 0, 0)),
                  pl.BlockSpec((1, D), const2),
                  pl.BlockSpec(wm.shape, const2),
                  pl.BlockSpec(wt.shape, const2),
                  pl.BlockSpec((1, SEG), const2), pl.BlockSpec((1, SEG), const2),
                  pl.BlockSpec((1, SEG), const2), pl.BlockSpec((1, SEG), const2),
                  pl.BlockSpec((1, LANES), const2),
                  pl.BlockSpec((N_MAPS, LANES), const2), pl.BlockSpec((N_MAPS, LANES), const2)],
        out_specs=[wide_spec, wide_spec, seg_spec, wide_spec, wide_spec, seg_spec, seg_spec,
                   lane_spec, lane_spec, lane_spec],
        out_shape=[wide_shape, wide_shape, seg_shape, wide_shape, wide_shape, seg_shape, seg_shape,
                   jax.ShapeDtypeStruct((B, S, LANES), BF16), jax.ShapeDtypeStruct((B, S, LANES), BF16),
                   jax.ShapeDtypeStruct((B, S, LANES), F32)],
        compiler_params=_params(("parallel", "parallel")),
        name="inproj",
    )(x, mod3, g1, wm, wt, gq, gk, gsq, gsk, gik, fdq, fsq)


def _diff_kernel(q_ref, k_ref, v_ref, lam_ref, g_ref, o_ref, s_ref, m_ref, l_ref, acc_ref, *, tq):
    qi = pl.program_id(1)
    ki = pl.program_id(2)
    n_maps = 2 * DIFF_HEADS
    reps = tq // LANES

    @pl.when(ki == 0)
    def _():
        m_ref[...] = jnp.full(m_ref.shape, NEG_BIG, F32)
        l_ref[...] = jnp.zeros(l_ref.shape, F32)
        acc_ref[...] = jnp.zeros(acc_ref.shape, F32)

    def step(diag):
        if diag:
            row = lax.broadcasted_iota(I32, (tq, tq), 0)
            col = lax.broadcasted_iota(I32, (tq, tq), 1)
            causal_bias = jnp.where(col <= row, 0.0, NEG_BIG)
        m_olds, m_news = [], []
        for idx in range(n_maps):
            s = _nt(q_ref[0, :, idx * LANES:(idx + 1) * LANES], k_ref[0, :, idx * LANES:(idx + 1) * LANES])
            if diag:
                s = s + causal_bias
            s_ref[idx] = s
            m_old = m_ref[idx]
            m_olds.append(m_old)
            m_news.append(jnp.maximum(m_old, jnp.max(s, axis=-1, keepdims=True)))
        for idx in range(n_maps):
            h = idx // 2
            p = jnp.exp2(s_ref[idx] - _rep(m_news[idx], reps))
            alpha = jnp.exp2(m_olds[idx] - m_news[idx])
            psum = p[:, 0:LANES]
            for g in range(1, reps):
                psum = psum + p[:, g * LANES:(g + 1) * LANES]
            l_ref[idx] = alpha * l_ref[idx] + psum
            acc_ref[idx] = alpha * acc_ref[idx] + _dot(p.astype(BF16), v_ref[0, :, h * LANES:(h + 1) * LANES])
            m_ref[idx] = m_news[idx]

    @pl.when(ki < qi)
    def _():
        step(False)

    @pl.when(ki == qi)
    def _():
        step(True)
        lv = lam_ref[...]
        lam = (jnp.exp(jnp.sum(lv[0:1] * lv[1:2], axis=-1, keepdims=True))
               - jnp.exp(jnp.sum(lv[2:3] * lv[3:4], axis=-1, keepdims=True)) + LAM_INIT)
        for h in range(DIFF_HEADS):
            o1 = acc_ref[2 * h] / jnp.sum(l_ref[2 * h], axis=-1, keepdims=True)
            o2 = acc_ref[2 * h + 1] / jnp.sum(l_ref[2 * h + 1], axis=-1, keepdims=True)
            o = o1 - lam * o2
            ms = jnp.mean(o * o, axis=-1, keepdims=True)
            on = o * lax.rsqrt(ms + EPS) * g_ref[...]
            o_ref[0, :, h * LANES:(h + 1) * LANES] = (on * (1.0 - LAM_INIT)).astype(BF16)


def _diff_attention(dq, dk, dv, lamv, subln_g, tq):
    B, S, W = dq.shape
    V = dv.shape[2]
    nq = S // tq
    n_maps = 2 * DIFF_HEADS
    return pl.pallas_call(
        functools.partial(_diff_kernel, tq=tq),
        grid=(B, nq, nq),
        in_specs=[pl.BlockSpec((1, tq, W), lambda b, i, j: (b, i, 0)),
                  pl.BlockSpec((1, tq, W), lambda b, i, j: (b, jnp.minimum(i, j), 0)),
                  pl.BlockSpec((1, tq, V), lambda b, i, j: (b, jnp.minimum(i, j), 0)),
                  pl.BlockSpec(lamv.shape, lambda b, i, j: (0, 0)),
                  pl.BlockSpec((1, LANES), lambda b, i, j: (0, 0))],
        out_specs=pl.BlockSpec((1, tq, V), lambda b, i, j: (b, i, 0)),
        out_shape=jax.ShapeDtypeStruct((B, S, V), BF16),
        scratch_shapes=[pltpu.VMEM((n_maps, tq, tq), F32),
                        pltpu.VMEM((n_maps, tq, LANES), F32),
                        pltpu.VMEM((n_maps, tq, LANES), F32),
                        pltpu.VMEM((n_maps, tq, LANES), F32)],
        compiler_params=_params(("parallel", "parallel", "arbitrary")),
        name="diff_attention",
    )(dq, dk, dv, lamv, subln_g)


def _score_key(v):
    bits = lax.bitcast_convert_type(v, I32)
    return bits ^ ((bits >> 31) & 0x7FFFFFFF)


_SUM_LANE = (LANES - 1, 0)


def _dsa_kernel(iq_ref, iw_ref, ikl_ref, ikh_ref, q_ref, k_ref, v_ref, o_ref,
                key_ref, thr_ref, nties_ref, mb_ref, s_ref, m_ref, acc_ref, *, tq, tk, rs, topk):
    qi = pl.program_id(1)
    q0 = qi * tq
    nkc = (q0 + tq + tk - 1) // tk
    row = q0 + lax.broadcasted_iota(I32, (tq, tk), 0)
    col0 = lax.broadcasted_iota(I32, (tq, tk), 1)
    low = lax.broadcasted_iota(I32, (tq, LANES), 1) < HEAD
    reps = tk // LANES
    iw = iw_ref[0]

    def score_chunk(kc, carry):
        k0 = pl.multiple_of(kc * tk, tk)
        ikl = ikl_ref[0, pl.ds(k0, tk), :]
        ikh = ikh_ref[0, pl.ds(k0, tk), :]
        sc = jnp.zeros((tq, tk), F32)
        for j in range(IDX_HEADS // 2):
            iqp = iq_ref[0, :, j * LANES:(j + 1) * LANES]
            sc = sc + iw[:, 2 * j:2 * j + 1] * jnp.maximum(_nt(iqp, ikl), 0.0)
            sc = sc + iw[:, 2 * j + 1:2 * j + 2] * jnp.maximum(_nt(iqp, ikh), 0.0)
        sc = jnp.where(sc == 0.0, 0.0, sc)
        sc = jnp.where(col0 + k0 <= row, sc, -jnp.inf)
        key_ref[:, pl.ds(k0, tk)] = _score_key(sc)
        return carry

    lax.fori_loop(0, nkc, score_chunk, 0)

    groups = [slice(r0, r0 + rs) for r0 in range(0, tq, rs)]

    def count(thrs, strict):
        accs = []
        for rows, thr in zip(groups, thrs):
            thr_t = _rep(thr, reps)

            def body(kc, acc, rows=rows, thr_t=thr_t):
                k0 = pl.multiple_of(kc * tk, tk)
                keyc = key_ref[rows, pl.ds(k0, tk)]
                hit = jnp.where((keyc > thr_t) if strict else (keyc >= thr_t), 1.0, 0.0)
                for g in range(reps):
                    acc = acc + hit[:, g * LANES:(g + 1) * LANES]
                return acc

            accs.append(lax.fori_loop(0, nkc, body, jnp.zeros((rs, LANES), F32)))
        return [jnp.sum(acc, axis=-1, keepdims=True) for acc in accs]

    def bit_step(i, tus):
        bit = jnp.left_shift(jnp.int32(1), 31 - i)
        cands = [tu | bit for tu in tus]
        cnts = count([c ^ INT_MIN for c in cands], False)
        return tuple(jnp.where(cnt >= float(topk), c, tu) for cnt, c, tu in zip(cnts, cands, tus))

    tus = lax.fori_loop(0, 32, bit_step, tuple(jnp.zeros((rs, LANES), I32) for _ in groups))
    thrs = [tu ^ INT_MIN for tu in tus]
    for rows, thr, n_gt in zip(groups, thrs, count(thrs, True)):
        thr_ref[rows, :] = thr
        nties_ref[rows, :] = jnp.broadcast_to(float(topk) - n_gt, (rs, LANES))

    m_ref[...] = jnp.full(m_ref.shape, NEG_BIG, F32)
    acc_ref[...] = jnp.zeros(acc_ref.shape, F32)
    tri =jnp.where(lax.broadcasted_iota(I32, (tk, tk), 0) < lax.broadcasted_iota(I32, (tk, tk), 1),
                    1.0, 0.0).astype(BF16)

    def attend_chunk(kc, ties_before):
        k0 = pl.multiple_of(kc * tk, tk)
        keyc = key_ref[:, pl.ds(k0, tk)]
        thr_t = _rep(thr_ref[...], reps)
        eq = jnp.where(keyc == thr_t, 1.0, 0.0)
        rank = _dot(eq.astype(BF16), tri) + _rep(ties_before, reps)
        take = jnp.where(keyc > thr_t, 1.0, jnp.where(rank < _rep(nties_ref[...], reps), eq, 0.0))
        mb_ref[...] = jnp.where(col0 + k0 <= row, jnp.where(take > 0.5, 0.0, NEG_BIG), NEG_BIG)
        mb = mb_ref[...]
        m_olds, m_news = [], []
        for hd in range(DSA_HEADS):
            s = _nt(q_ref[0, :, hd * LANES:(hd + 1) * LANES],
                    k_ref[0, pl.ds(k0, tk), hd * LANES:(hd + 1) * LANES]) + mb
            s_ref[hd] = s
            m_old = m_ref[hd]
            m_olds.append(m_old)
            m_news.append(jnp.maximum(m_old, jnp.max(s, axis=-1, keepdims=True)))
        for j in range(DSA_HEADS // 2):
            vp = v_ref[0, pl.ds(k0, tk), j * LANES:(j + 1) * LANES]
            vlane = lax.broadcasted_iota(I32, vp.shape, 1)
            for a in range(2):
                hd = 2 * j + a
                keep = (vlane < HEAD) if a == 0 else (vlane >= HEAD)
                va = jnp.where(vlane == _SUM_LANE[a], jnp.ones_like(vp), jnp.where(keep, vp, jnp.zeros_like(vp)))
                p = jnp.exp2((s_ref[hd] - _rep(m_news[hd], reps)).astype(BF16))
                acc_ref[hd] = jnp.exp2(m_olds[hd] - m_news[hd]) * acc_ref[hd] + _dot(p, va)
                m_ref[hd] = m_news[hd]
        psum = eq[:, 0:LANES]
        for g in range(1, reps):
            psum = psum + eq[:, g * LANES:(g + 1) * LANES]
        return ties_before + jnp.sum(psum, axis=-1, keepdims=True)

    lax.fori_loop(0, nkc, attend_chunk, jnp.zeros((tq, LANES), F32))

    lane = lax.broadcasted_iota(I32, (tq, LANES), 1)
    for j in range(DSA_HEADS // 2):
        oa, ob = acc_ref[2 * j], acc_ref[2 * j + 1]
        la = jnp.sum(jnp.where(lane == _SUM_LANE[0], oa, 0.0), axis=-1, keepdims=True)
        lb = jnp.sum(jnp.where(lane == _SUM_LANE[1], ob, 0.0), axis=-1, keepdims=True)
        o_ref[0, :, j * LANES:(j + 1) * LANES] = jnp.where(low, oa / la, ob / lb).astype(BF16)


def _dsa_attention(iq, iw, ikl, ikh, sq, sk, sv, tq, tk, rs, topk):
    B, S, W = sq.shape
    V = sv.shape[2]
    nq = S // tq
    qblk = lambda b, i: (b, i, 0)
    full = lambda b, i: (b, 0, 0)
    once = pl.Buffered(1)
    return pl.pallas_call(
        functools.partial(_dsa_kernel, tq=tq, tk=tk, rs=rs, topk=topk),
        grid=(B, nq),
        in_specs=[pl.BlockSpec((1, tq, iq.shape[2]), qblk),
                  pl.BlockSpec((1, tq, LANES), qblk),
                  pl.BlockSpec((1, S, LANES), full, pipeline_mode=once),
                  pl.BlockSpec((1, S, LANES), full, pipeline_mode=once),
                  pl.BlockSpec((1, tq, W), qblk),
                  pl.BlockSpec((1, S, W), full, pipeline_mode=once),
                  pl.BlockSpec((1, S, V), full, pipeline_mode=once)],
        out_specs=pl.BlockSpec((1, tq, V), qblk),
        out_shape=jax.ShapeDtypeStruct((B, S, V), BF16),
        scratch_shapes=[pltpu.VMEM((tq, S), I32),
                        pltpu.VMEM((tq, LANES), I32),
                        pltpu.VMEM((tq, LANES), F32),
                        pltpu.VMEM((tq, tk), F32),
                        pltpu.VMEM((DSA_HEADS, tq, tk), F32),
                        pltpu.VMEM((DSA_HEADS, tq, LANES), F32),
                        pltpu.VMEM((DSA_HEADS, tq, LANES), F32)],
        compiler_params=_params(("parallel", "arbitrary")),
        name="dsa_attention",
    )(iq, iw, ikl, ikh, sq, sk, sv)


def _mix_kernel(do_ref, so_ref, x_ref, mod_ref, g2_ref, wo1_ref, wo2_ref, wsg_ref, wsu_ref, wsd_ref,
                rwt_ref, base_ref, h2_ref, lg_ref):
    mix = _dot(do_ref[0], wo1_ref[...]) + _dot(so_ref[0], wo2_ref[...])
    mod = mod_ref[0]
    x1 = x_ref[0] + mod[2:3] * mix
    ms = jnp.mean(x1 * x1, axis=-1, keepdims=True)
    h2 = x1 * lax.rsqrt(ms + EPS) * g2_ref[...] * (1.0 + mod[4:5]) + mod[3:4]
    hb = h2.astype(BF16)
    gate = _dot(hb, wsg_ref[...])
    up = _dot(hb, wsu_ref[...])
    act = gate / (1.0 + jnp.exp(-gate)) * up
    shared = _dot(act.astype(BF16), wsd_ref[...])
    base_ref[0] = x1 + mod[5:6] * shared
    lg_ref[...] = _nt(rwt_ref[...], hb)
    hf = hb.astype(F32)
    tm = hf.shape[0]
    for j in range(ROW_TILES):
        h2_ref[pl.ds(j, tm, stride=ROW_TILES), :] = hf[:, j * LANES:(j + 1) * LANES]


def _mix(diff_out, dsa_out, x, mod3, g2, wo1, wo2, wsg, wsu, wsd, rwt, tm):
    B, S, D = x.shape
    ns = S // tm
    T = B * S
    tok = lambda b, i: (b, i, 0)
    c2 = lambda b, i: (0, 0)
    return pl.pallas_call(
        _mix_kernel,
        grid=(B, ns),
        in_specs=[pl.BlockSpec((1, tm, SEG), tok), pl.BlockSpec((1, tm, SEG), tok),
                  pl.BlockSpec((1, tm, D), tok),
                  pl.BlockSpec((1, 6, D), lambda b, i: (b, 0, 0)),
                  pl.BlockSpec((1, D), c2),
                  pl.BlockSpec(wo1.shape, c2), pl.BlockSpec(wo2.shape, c2),
                  pl.BlockSpec(wsg.shape, c2), pl.BlockSpec(wsu.shape, c2), pl.BlockSpec(wsd.shape, c2),
                  pl.BlockSpec(rwt.shape, c2)],
        out_specs=[pl.BlockSpec((1, tm, D), tok),
                   pl.BlockSpec((tm * ROW_TILES, LANES), lambda b, i: (b * ns + i, 0)),
                   pl.BlockSpec((N_EXPERTS, tm), lambda b, i: (0, b * ns + i))],
        out_shape=[jax.ShapeDtypeStruct((B, S, D), F32),
                   jax.ShapeDtypeStruct((T * ROW_TILES, LANES), F32),
                   jax.ShapeDtypeStruct((N_EXPERTS, T), F32)],
        compiler_params=_params(("parallel", "parallel")),
        name="mix_shared_router",
    )(diff_out, dsa_out, x, mod3, g2, wo1, wo2, wsg, wsu, wsd, rwt)


def _first_max(v, idx, sentinel):
    m = jnp.max(v, axis=0, keepdims=True)
    i = jnp.min(jnp.where(v == m, idx, sentinel), axis=0, keepdims=True)
    return m, i


def _route_kernel(lg_ref, bias_ref, eidx_ref, gate_ref):
    lg = lg_ref[...]
    tt = lg.shape[1]
    scores = 1.0 / (1.0 + jnp.exp(-lg))
    biased = scores + bias_ref[...]
    gi = lax.broadcasted_iota(I32, (GROUP_SIZE, tt), 0).astype(F32)
    gscore = []
    for g in range(N_GROUPS):
        blk = biased[g * GROUP_SIZE:(g + 1) * GROUP_SIZE, :]
        m1, i1 = _first_max(blk, gi, float(GROUP_SIZE))
        m2 = jnp.max(jnp.where(gi == i1, -jnp.inf, blk), axis=0, keepdims=True)
        gscore.append(m1 + m2)
    gs = jnp.concatenate(gscore, axis=0)
    gidx = lax.broadcasted_iota(I32, (N_GROUPS, tt), 0).astype(F32)
    chosen = jnp.zeros((N_GROUPS, tt), F32)
    for _ in range(TOPK_GROUPS):
        _, ig = _first_max(gs, gidx, float(N_GROUPS))
        hit = gidx == ig
        chosen = jnp.where(hit, 1.0, chosen)
        gs = jnp.where(hit, -jnp.inf, gs)
    masked = jnp.concatenate(
        [jnp.where(chosen[g:g + 1, :] > 0.5, biased[g * GROUP_SIZE:(g + 1) * GROUP_SIZE, :], -jnp.inf)
         for g in range(N_GROUPS)], axis=0)
    ei = lax.broadcasted_iota(I32, (N_EXPERTS, tt), 0).astype(F32)
    ids, ws = [], []
    for _ in range(TOP_K):
        _, ie = _first_max(masked, ei, float(N_EXPERTS))
        hit = ei == ie
        ws.append(jnp.sum(jnp.where(hit, scores, 0.0), axis=0, keepdims=True))
        ids.append(ie)
        masked = jnp.where(hit, -jnp.inf, masked)
    w = jnp.concatenate(ws, axis=0)
    gate_ref[...] = w / jnp.sum(w, axis=0, keepdims=True) * ROUTED_SCALE
    eidx_ref[...] = jnp.concatenate(ids, axis=0).astype(I32)


def _route(logits_t, bias_col, tt):
    E, T = logits_t.shape
    return pl.pallas_call(
        _route_kernel,
        grid=(T // tt,),
        in_specs=[pl.BlockSpec((E, tt), lambda i: (0, i)),
                  pl.BlockSpec((E, 1), lambda i: (0, 0))],
        out_specs=[pl.BlockSpec((TOP_K, tt), lambda i: (0, i)),
                   pl.BlockSpec((TOP_K, tt), lambda i: (0, i))],
        out_shape=[jax.ShapeDtypeStruct((TOP_K, T), I32),
                   jax.ShapeDtypeStruct((TOP_K, T), F32)],
        compiler_params=_params(("parallel",)),
        name="route",
    )(logits_t, bias_col)


def _plan_kernel(eidx_ref, dest_ref, bexp_ref, nused_ref, cnt_col, cnt_row, slot_base, *, blk, nb_pad):
    ph = pl.program_id(0)
    i = pl.program_id(1)
    tt = eidx_ref.shape[1]
    eidx = eidx_ref[...]
    ei = lax.broadcasted_iota(I32, (N_EXPERTS, tt), 0)
    onehot = jnp.zeros((N_EXPERTS, tt), F32)
    for k in range(TOP_K):
        onehot = onehot + jnp.where(ei == eidx[k:k + 1, :], 1.0, 0.0)
    oh = onehot.astype(BF16)

    @pl.when((ph == 0) & (i == 0))
    def _():
        cnt_col[...] = jnp.zeros(cnt_col.shape, F32)
        cnt_row[...] = jnp.zeros(cnt_row.shape, F32)

    @pl.when(ph == 0)
    def _():
        cnt_col[...] += _dot(oh, jnp.ones((tt, LANES), BF16))
        cnt_row[...] += _nt(jnp.ones((8, tt), BF16), oh)

    @pl.when((ph == 1) & (i == 0))
    def _():
        inv = 1.0 / blk
        nb_col = jnp.floor((cnt_col[:, 0:1] + (blk - 1)) * inv)
        nb_row = jnp.floor((cnt_row[0:1, :] + (blk - 1)) * inv)
        r = lax.broadcasted_iota(I32, (N_EXPERTS, N_EXPERTS), 0)
        c = lax.broadcasted_iota(I32, (N_EXPERTS, N_EXPERTS), 1)
        bstart = jnp.sum(jnp.where(c < r, nb_row, 0.0), axis=-1, keepdims=True)
        bend = bstart + nb_col
        slot_base[...] = bstart * blk
        jb = lax.broadcasted_iota(I32, (N_EXPERTS, nb_pad), 1).astype(F32)
        be = jnp.sum(jnp.where(bend <= jb, 1.0, 0.0), axis=0, keepdims=True)
        bexp_ref[...] = jnp.minimum(be, N_EXPERTS - 1.0).astype(I32)
        nused_ref[...] = jnp.broadcast_to(jnp.sum(nb_row, axis=-1, keepdims=True), nused_ref.shape).astype(I32)

    @pl.when(ph == 1)
    def _():
        tri = jnp.where(lax.broadcasted_iota(I32, (tt, tt), 0) < lax.broadcasted_iota(I32, (tt, tt), 1),
                        1.0, 0.0).astype(BF16)
        slot = _dot(oh, tri) + slot_base[...]
        for k in range(TOP_K):
            dk = jnp.sum(jnp.where(ei == eidx[k:k + 1, :], slot, 0.0), axis=0, keepdims=True)
            dest_ref[k:k + 1, :] = dk.astype(I32)
        slot_base[...] += jnp.sum(onehot, axis=-1, keepdims=True)


def _plan(eidx, tt, blk, nb_pad):
    K, T = eidx.shape
    nt = T // tt
    return pl.pallas_call(
        functools.partial(_plan_kernel, blk=blk, nb_pad=nb_pad),
        grid=(2, nt),
        in_specs=[pl.BlockSpec((K, tt), lambda p, i: (0, i))],
        out_specs=[pl.BlockSpec((K, tt), lambda p, i: (0, i * p)),
                   pl.BlockSpec((1, nb_pad), lambda p, i: (0, 0)),
                   pl.BlockSpec((1, LANES), lambda p, i: (0, 0))],
        out_shape=[jax.ShapeDtypeStruct((K, T), I32),
                   jax.ShapeDtypeStruct((1, nb_pad), I32),
                   jax.ShapeDtypeStruct((1, LANES), I32)],
        scratch_shapes=[pltpu.VMEM((N_EXPERTS, LANES), F32),
                        pltpu.VMEM((8, N_EXPERTS), F32),
                        pltpu.VMEM((N_EXPERTS, 1), F32)],
        compiler_params=_params(("arbitrary", "arbitrary")),
        name="plan",
    )(eidx)


def _row(ref, r):
    return ref.at[pl.ds(pl.multiple_of(r * ROW_TILES, ROW_TILES), ROW_TILES), :]


def _dispatch_kernel(dest_ref, h_ref, xs_in_ref, xs_ref, sem):
    del xs_in_ref
    tt = h_ref.shape[0] // ROW_TILES

    def issue(t, c):
        for k in range(TOP_K):
            pltpu.make_async_copy(_row(h_ref, t), _row(xs_ref, dest_ref[k, t]), sem).start()
        return c

    lax.fori_loop(0, tt, issue, 0)

    def drain(t, c):
        for k in range(TOP_K):
            pltpu.make_async_copy(_row(h_ref, 0), _row(xs_ref, 0), sem).wait()
        return c

    lax.fori_loop(0, tt, drain, 0)


def _dispatch(dest, h2rows, xs_init, tt):
    T = h2rows.shape[0] // ROW_TILES
    return pl.pallas_call(
        _dispatch_kernel,
        grid=(T // tt,),
        in_specs=[pl.BlockSpec((TOP_K, tt), lambda i: (0, i), memory_space=pltpu.SMEM),
                  pl.BlockSpec((tt * ROW_TILES, LANES), lambda i: (i, 0)),
                  pl.BlockSpec(memory_space=pl.ANY)],
        out_specs=pl.BlockSpec(memory_space=pl.ANY),
        out_shape=jax.ShapeDtypeStruct(xs_init.shape, xs_init.dtype),
        scratch_shapes=[pltpu.SemaphoreType.DMA(())],
        input_output_aliases={2: 0},
        compiler_params=_params(("arbitrary",), has_side_effects=True),
        name="dispatch",
    )(dest, h2rows, xs_init)


def _experts_kernel(bexp_ref, nused_ref, xs_ref, wg_ref, wu_ref, wd_ref, y_ref, xb_ref):
    del bexp_ref

    @pl.when(pl.program_id(0) < nused_ref[0])
    def _():
        blk = xb_ref.shape[0]
        for j in range(ROW_TILES):
            xb_ref[:, j * LANES:(j + 1) * LANES] = xs_ref[pl.ds(j, blk, stride=ROW_TILES), :].astype(BF16)
        xb = xb_ref[...]
        gate = _dot(xb, wg_ref[0])
        up = _dot(xb, wu_ref[0])
        act = gate / (1.0 + jnp.exp(-gate)) * up
        y = _dot(act.astype(BF16), wd_ref[0])
        for j in range(ROW_TILES):
            y_ref[pl.ds(j, blk, stride=ROW_TILES), :] = y[:, j * LANES:(j + 1) * LANES]


def _experts(bexp, nused, xs, wg, wu, wd, blk, n_blocks):
    live = lambda j, be, nu: jnp.minimum(j, nu[0] - 1)
    row_spec = pl.BlockSpec((blk * ROW_TILES, LANES), lambda j, be, nu: (live(j, be, nu), 0))
    wspec = lambda w: pl.BlockSpec((1,) + w.shape[1:], lambda j, be, nu: (be[live(j, be, nu)], 0, 0))
    return pl.pallas_call(
        _experts_kernel,
        grid_spec=pltpu.PrefetchScalarGridSpec(
            num_scalar_prefetch=2,
            grid=(n_blocks,),
            in_specs=[row_spec, wspec(wg), wspec(wu), wspec(wd)],
            out_specs=row_spec,
            scratch_shapes=[pltpu.VMEM((blk, D_MODEL), BF16)]),
        out_shape=jax.ShapeDtypeStruct(xs.shape, F32),
        compiler_params=_params(("arbitrary",)),
        name="experts",
    )(bexp, nused, xs, wg, wu, wd)


def _combine_kernel(dest_ref, gate_ref, base_ref, mod_ref, y_ref, o_ref, buf, sem):
    tt = base_ref.shape[1]

    def issue(t, c):
        for k in range(TOP_K):
            pltpu.make_async_copy(_row(y_ref, dest_ref[k, t]), _row(buf, k * tt + t), sem).start()
        return c

    lax.fori_loop(0, tt, issue, 0)

    def drain(t, c):
        for k in range(TOP_K):
            pltpu.make_async_copy(_row(y_ref, 0), _row(buf, 0), sem).wait()
        return c

    lax.fori_loop(0, tt, drain, 0)

    gates = gate_ref[...]
    g2 = mod_ref[0][5:6]
    for j in range(D_MODEL // LANES):
        acc = jnp.zeros((tt, LANES), F32)
        for k in range(TOP_K):
            acc = acc + gates[:, k:k + 1] * buf[pl.ds(k * tt * ROW_TILES + j, tt, stride=ROW_TILES), :]
        cols = slice(j * LANES, (j + 1) * LANES)
        o_ref[0, :, cols] = base_ref[0, :, cols] + g2[:, cols] * acc


def _combine(dest, gates_tk, base, mod3, y, tt):
    B, S, D = base.shape
    ns = S // tt
    return pl.pallas_call(
        _combine_kernel,
        grid=(B, ns),
        in_specs=[pl.BlockSpec((TOP_K, tt), lambda b, i: (0, b * ns + i), memory_space=pltpu.SMEM),
                  pl.BlockSpec((tt, TOP_K), lambda b, i: (b * ns + i, 0)),
                  pl.BlockSpec((1, tt, D), lambda b, i: (b, i, 0)),
                  pl.BlockSpec((1, 6, D), lambda b, i: (b, 0, 0)),
                  pl.BlockSpec(memory_space=pl.ANY)],
        out_specs=pl.BlockSpec((1, tt, D), lambda b, i: (b, i, 0)),
        out_shape=jax.ShapeDtypeStruct((B, S, D), F32),
        scratch_shapes=[pltpu.VMEM((TOP_K * tt * ROW_TILES, LANES), F32),
                        pltpu.SemaphoreType.DMA(())],
        compiler_params=_params(("arbitrary", "arbitrary")),
        name="combine",
    )(dest, gates_tk, base, mod3, y)


def _alibi_q_features(n_heads, maps_per_head):
    slopes = 2.0 ** (-8.0 * jnp.arange(1, n_heads + 1, dtype=F32) / n_heads)
    c = jnp.repeat(slopes, maps_per_head) * LOG2E * POS_RADIX
    pieces = []
    rest = c
    for _ in range(N_SPLIT):
        p = rest.astype(BF16).astype(F32)
        pieces.append(p)
        rest = rest - p
    hi = jnp.stack(pieces, axis=1)
    feat = jnp.concatenate([hi, hi / POS_RADIX], axis=1)
    return jnp.pad(feat, ((0, 0), (HEAD, LANES - HEAD - 2 * N_SPLIT)))


def kernel(x, c, ada_w, ada_b, norm1_g, norm2_g, w_in, diff_q_norm_g, diff_k_norm_g, lam_q1, lam_k1, lam_q2, lam_k2, diff_subln_g, dsa_q_norm_g, dsa_k_norm_g, idx_k_norm_g, w_out, router_w, router_bias, exp_w_gate, exp_w_up, exp_w_down, shared_w_gate, shared_w_up, shared_w_down):
    B, S, D = x.shape
    assert D == D_MODEL and ada_w.shape[0] == 1 and S <= POS_RADIX * 128
    T = B * S
    topk = min(DSA_TOPK, S // 4)
    tm = min(512, S)
    tq_diff = min(512, S)
    tq_dsa = min(256, S)
    tk_dsa = min(512, S)
    rs_dsa = 64
    tt_route = min(512, T)
    tt_move = min(128, S)
    blk = 512
    n_blocks = (T * TOP_K) // blk + N_EXPERTS
    nb_pad = -(-n_blocks // LANES) * LANES

    n_main = 7 * SEG
    wm = w_in[0, :, :n_main].astype(BF16)
    wt = jnp.pad(w_in[0, :, n_main:], ((0, 0), (0, LANES - (IDX_DIM + IDX_HEADS)))).astype(BF16)
    tile8 = lambda g: jnp.tile(g[0], SEG // g.shape[1]).reshape(1, SEG)
    gik = jnp.pad(idx_k_norm_g[0], (0, LANES - IDX_DIM)).reshape(1, LANES)
    lamv = jnp.concatenate([lam_q1, lam_k1, lam_q2, lam_k2], axis=0)
    wo1 = w_out[0, :SEG].astype(BF16)
    wo2 = w_out[0, SEG:].astype(BF16)
    rwt = router_w[0].T.astype(BF16)
    wg = exp_w_gate[0].astype(BF16)
    wu = exp_w_up[0].astype(BF16)
    wd = exp_w_down[0].astype(BF16)

    mod3 = _ada(c, ada_w[0], ada_b[0]).reshape(B, 6, D)

    dq, dk, dv, sq, sk, sv, iq, ikl, ikh, iw = _inproj(
        x, mod3, norm1_g, wm, wt, tile8(diff_q_norm_g), tile8(diff_k_norm_g),
        tile8(dsa_q_norm_g), tile8(dsa_k_norm_g), gik,
        _alibi_q_features(DIFF_HEADS, 2), _alibi_q_features(DSA_HEADS, 1), tm)

    diff_out = _diff_attention(dq, dk, dv, lamv, diff_subln_g, tq_diff)
    dsa_out = _dsa_attention(iq, iw, ikl, ikh, sq, sk, sv, tq_dsa, tk_dsa, rs_dsa, topk)

    base, h2rows, logits_t = _mix(diff_out, dsa_out, x, mod3, norm2_g, wo1, wo2,
                                  shared_w_gate[0].astype(BF16), shared_w_up[0].astype(BF16),
                                  shared_w_down[0].astype(BF16), rwt, tm)

    eidx, gates = _route(logits_t, router_bias[0].reshape(N_EXPERTS, 1), tt_route)
    dest, bexp, nused = _plan(eidx, tt_route, blk, nb_pad)

    xs = _dispatch(dest, h2rows, jnp.zeros((n_blocks * blk * ROW_TILES, LANES), F32), tt_move)
    y = _experts(bexp.reshape(nb_pad), nused[0, :1], xs, wg, wu, wd, blk, n_blocks)
    return _combine(dest, gates.T, base, mod3, y, tt_move)
```

```python
import functools
import math

import jax
import jax.numpy as jnp
from jax import lax
from jax.experimental import pallas as pl
from jax.experimental.pallas import tpu as pltpu

F32 = jnp.float32
BF16 = jnp.bfloat16
I32 = jnp.int32
I16 = jnp.int16

D_MODEL = 1024
DIFF_HEADS = 4
DIFF_HEAD_DIM = 64
DSA_HEADS = 8
DSA_HEAD_DIM = 64
IDX_HEADS = 8
IDX_DIM = 64
DSA_TOPK = 256
N_EXPERTS = 256
TOP_K = 8
N_GROUPS = 8
GROUP_SIZE = N_EXPERTS // N_GROUPS
TOPK_GROUPS = 4
D_EXPERT = 256
D_SHARED = 256
ROUTED_SCALE = 2.5
EPS = 1e-6
LAM_INIT = 0.2

LANES = 128
ROW_TILES = D_MODEL // LANES
SEG = 512
HEAD = 64
N_MAPS = SEG // HEAD
WIDE = N_MAPS * LANES
POS_RADIX = 64
N_SPLIT = 3
NEG_BIG = -1e30
INT_MIN = -2147483648
LOG2E = math.log2(math.e)
VMEM_LIMIT = 56 * 1024 * 1024

NT_DIMS = (((1,), (1,)), ((), ()))


def _nt(a, b):
    return lax.dot_general(a, b, NT_DIMS, preferred_element_type=F32)


def _dot(a, b):
    return jnp.dot(a, b, preferred_element_type=F32)


def _rep(x, reps):
    return jnp.concatenate([x] * reps, axis=1)


def _params(sem, vmem=VMEM_LIMIT, **kw):
    return pltpu.CompilerParams(dimension_semantics=sem, vmem_limit_bytes=vmem, **kw)


def _ada_kernel(c_ref, w_ref, b_ref, o_ref):
    c = c_ref[...]
    s = c / (1.0 + jnp.exp(-c))
    o_ref[...] = jnp.dot(s, w_ref[...], preferred_element_type=F32,
                         precision=lax.Precision.HIGHEST) + b_ref[...]


def _ada(c, w, b):
    B, D = c.shape
    N = w.shape[1]
    tn = D
    return pl.pallas_call(
        _ada_kernel,
        grid=(N // tn,),
        in_specs=[pl.BlockSpec((B, D), lambda j: (0, 0)),
                  pl.BlockSpec((D, tn), lambda j: (0, j)),
                  pl.BlockSpec((1, tn), lambda j: (0, j))],
        out_specs=pl.BlockSpec((B, tn), lambda j: (0, j)),
        out_shape=jax.ShapeDtypeStruct((B, N), F32),
        compiler_params=_params(("arbitrary",)),
        name="ada",
    )(c, w, b.reshape(1, N))


def _group_sumsq(z):
    n = z.shape[1]
    r = lax.broadcasted_iota(I32, (n, n), 0) // HEAD
    c = lax.broadcasted_iota(I32, (n, n), 1) // HEAD
    bd = jnp.where(r == c, 1.0, 0.0).astype(BF16)
    zz = z * z
    hi = zz.astype(BF16)
    lo = (zz - hi.astype(F32)).astype(BF16)
    return _dot(hi, bd) + _dot(lo, bd)


def _inproj_kernel(x_ref, mod_ref, g1_ref, wm_ref, wt_ref, gq_ref, gk_ref, gsq_ref, gsk_ref, gik_ref,
                   fdq_ref, fsq_ref,
                   dq_ref, dk_ref, dv_ref, sq_ref, sk_ref, sv_ref, iq_ref, ikl_ref, ikh_ref, iw_ref):
    x = x_ref[0]
    tm = x.shape[0]
    ms = jnp.mean(x * x, axis=-1, keepdims=True)
    y = x * lax.rsqrt(ms + EPS) * g1_ref[...]
    mod = mod_ref[0]
    h = y * (1.0 + mod[1:2]) + mod[0:1]
    hb = h.astype(BF16)

    lane = lax.broadcasted_iota(I32, (tm, LANES), 1)
    is_head = lane < HEAD
    kpos = pl.program_id(1) * tm + lax.broadcasted_iota(I32, (tm, LANES), 0)
    hi_digit = (kpos // POS_RADIX).astype(F32)
    lo_digit = (kpos % POS_RADIX).astype(F32)
    kfeat = jnp.where(lane < HEAD + N_SPLIT, hi_digit, jnp.where(lane < HEAD + 2 * N_SPLIT, lo_digit, 0.0))

    def plain(seg_idx, out_ref):
        out_ref[0] = _dot(hb, wm_ref[:, seg_idx * SEG:(seg_idx + 1) * SEG]).astype(BF16)

    def normed(seg_idx, g_ref, scale, feat_ref, out_ref):
        half = SEG // 2
        for i in range(2):
            lo = seg_idx * SEG + i * half
            z = _dot(hb, wm_ref[:, lo:lo + half])
            ss = _group_sumsq(z)
            zn = z * lax.rsqrt(ss * (1.0 / HEAD) + EPS) * (g_ref[:, i * half:(i + 1) * half] * scale)
            for g in range(half // LANES):
                zg = zn[:, g * LANES:(g + 1) * LANES]
                for odd in range(2):
                    idx = i * (half // HEAD) + 2 * g + odd
                    src = zg if odd == 0 else pltpu.roll(zg, HEAD, 1)
                    feat = kfeat if feat_ref is None else feat_ref[idx:idx + 1, :]
                    out_ref[0, :, idx * LANES:(idx + 1) * LANES] = jnp.where(is_head, src, feat).astype(BF16)

    normed(0, gq_ref, HEAD ** -0.5 * LOG2E, fdq_ref, dq_ref)
    normed(1, gk_ref, 1.0, None, dk_ref)
    plain(2, dv_ref)
    normed(3, gsq_ref, HEAD ** -0.5 * LOG2E, fsq_ref, sq_ref)
    normed(4, gsk_ref, 1.0, None, sk_ref)
    plain(5, sv_ref)
    plain(6, iq_ref)

    t = _dot(hb, wt_ref[...])
    ikraw = jnp.where(lane < IDX_DIM, t, 0.0)
    ss = jnp.sum(ikraw * ikraw, axis=-1, keepdims=True) * (1.0 / IDX_DIM)
    ikn = ikraw * lax.rsqrt(ss + EPS) * gik_ref[...]
    ikl_ref[0] = ikn.astype(BF16)
    ikh_ref[0] = pltpu.roll(ikn, IDX_DIM, 1).astype(BF16)
    iwraw = jnp.where((lane >= IDX_DIM) & (lane < IDX_DIM + IDX_HEADS), t, 0.0)
    iw_ref[0] = pltpu.roll(iwraw * (IDX_HEADS ** -0.5), LANES - IDX_DIM, 1) * (IDX_DIM ** -0.5)


def _inproj(x, mod3, g1, wm, wt, gq, gk, gsq, gsk, gik, fdq, fsq, tm):
    B, S, D = x.shape
    ns = S // tm
    tok = lambda b, i: (b, i, 0)
    const2 = lambda b, i: (0, 0)
    seg_spec = pl.BlockSpec((1, tm, SEG), tok)
    wide_spec = pl.BlockSpec((1, tm, WIDE), tok)
    lane_spec = pl.BlockSpec((1, tm, LANES), tok)
    seg_shape = jax.ShapeDtypeStruct((B, S, SEG), BF16)
    wide_shape = jax.ShapeDtypeStruct((B, S, WIDE), BF16)
    return pl.pallas_call(
        _inproj_kernel,
        grid=(B, ns),
        in_specs=[pl.BlockSpec((1, tm, D), tok),
                  pl.BlockSpec((1, 6, D), lambda b, i: (b, 0, 0)),
                  pl.BlockSpec((1, D), const2),
                  pl.BlockSpec(wm.shape, const2),
                  pl.BlockSpec(wt.shape, const2),
                  pl.BlockSpec((1, SEG), const2), pl.BlockSpec((1, SEG), const2),
                  pl.BlockSpec((1, SEG), const2), pl.BlockSpec((1, SEG), const2),
                  pl.BlockSpec((1, LANES), const2),
                  pl.BlockSpec((N_MAPS, LANES), const2), pl.BlockSpec((N_MAPS, LANES), const2)],
        out_specs=[wide_spec, wide_spec, seg_spec, wide_spec, wide_spec, seg_spec, seg_spec,
                   lane_spec, lane_spec, lane_spec],
        out_shape=[wide_shape, wide_shape, seg_shape, wide_shape, wide_shape, seg_shape, seg_shape,
                   jax.ShapeDtypeStruct((B, S, LANES), BF16), jax.ShapeDtypeStruct((B, S, LANES), BF16),
                   jax.ShapeDtypeStruct((B, S, LANES), F32)],
        compiler_params=_params(("parallel", "parallel")),
        name="inproj",
    )(x, mod3, g1, wm, wt, gq, gk, gsq, gsk, gik, fdq, fsq)


def _diff_kernel(q_ref, k_ref, v_ref, lam_ref, g_ref, o_ref, s_ref, m_ref, l_ref, acc_ref, *, tq):
    qi = pl.program_id(1)
    ki = pl.program_id(2)
    n_maps = 2 * DIFF_HEADS
    reps = tq // LANES

    @pl.when(ki == 0)
    def _():
        m_ref[...] = jnp.full(m_ref.shape, NEG_BIG, F32)
        l_ref[...] = jnp.zeros(l_ref.shape, F32)
        acc_ref[...] = jnp.zeros(acc_ref.shape, F32)

    def step(diag):
        if diag:
            row = lax.broadcasted_iota(I32, (tq, tq), 0)
            col = lax.broadcasted_iota(I32, (tq, tq), 1)
            causal_bias = jnp.where(col <= row, 0.0, NEG_BIG)
        m_olds, m_news = [], []
        for idx in range(n_maps):
            s = _nt(q_ref[0, :, idx * LANES:(idx + 1) * LANES], k_ref[0, :, idx * LANES:(idx + 1) * LANES])
            if diag:
                s = s + causal_bias
            s_ref[idx] = s
            m_old = m_ref[idx]
            m_olds.append(m_old)
            m_news.append(jnp.maximum(m_old, jnp.max(s, axis=-1, keepdims=True)))
        for idx in range(n_maps):
            h = idx // 2
            p = jnp.exp2(s_ref[idx] - _rep(m_news[idx], reps))
            alpha = jnp.exp2(m_olds[idx] - m_news[idx])
            psum = p[:, 0:LANES]
            for g in range(1, reps):
                psum = psum + p[:, g * LANES:(g + 1) * LANES]
            l_ref[idx] = alpha * l_ref[idx] + psum
            acc_ref[idx] = alpha * acc_ref[idx] + _dot(p.astype(BF16), v_ref[0, :, h * LANES:(h + 1) * LANES])
            m_ref[idx] = m_news[idx]

    @pl.when(ki < qi)
    def _():
        step(False)

    @pl.when(ki == qi)
    def _():
        step(True)
        lv = lam_ref[...]
        lam = (jnp.exp(jnp.sum(lv[0:1] * lv[1:2], axis=-1, keepdims=True))
               - jnp.exp(jnp.sum(lv[2:3] * lv[3:4], axis=-1, keepdims=True)) + LAM_INIT)
        for h in range(DIFF_HEADS):
            o1 = acc_ref[2 * h] / jnp.sum(l_ref[2 * h], axis=-1, keepdims=True)
            o2 = acc_ref[2 * h + 1] / jnp.sum(l_ref[2 * h + 1], axis=-1, keepdims=True)
            o = o1 - lam * o2
            ms = jnp.mean(o * o, axis=-1, keepdims=True)
            on = o * lax.rsqrt(ms + EPS) * g_ref[...]
            o_ref[0, :, h * LANES:(h + 1) * LANES] = (on * (1.0 - LAM_INIT)).astype(BF16)


def _diff_attention(dq, dk, dv, lamv, subln_g, tq):
    B, S, W = dq.shape
    V = dv.shape[2]
    nq = S // tq
    n_maps = 2 * DIFF_HEADS
    return pl.pallas_call(
        functools.partial(_diff_kernel, tq=tq),
        grid=(B, nq, nq),
        in_specs=[pl.BlockSpec((1, tq, W), lambda b, i, j: (b, i, 0)),
                  pl.BlockSpec((1, tq, W), lambda b, i, j: (b, jnp.minimum(i, j), 0)),
                  pl.BlockSpec((1, tq, V), lambda b, i, j: (b, jnp.minimum(i, j), 0)),
                  pl.BlockSpec(lamv.shape, lambda b, i, j: (0, 0)),
                  pl.BlockSpec((1, LANES), lambda b, i, j: (0, 0))],
        out_specs=pl.BlockSpec((1, tq, V), lambda b, i, j: (b, i, 0)),
        out_shape=jax.ShapeDtypeStruct((B, S, V), BF16),
        scratch_shapes=[pltpu.VMEM((n_maps, tq, tq), F32),
                        pltpu.VMEM((n_maps, tq, LANES), F32),
                        pltpu.VMEM((n_maps, tq, LANES), F32),
                        pltpu.VMEM((n_maps, tq, LANES), F32)],
        compiler_params=_params(("parallel", "parallel", "arbitrary")),
        name="diff_attention",
    )(dq, dk, dv, lamv, subln_g)


def _score_key(v):
    bits = lax.bitcast_convert_type(v, I32)
    return bits ^ ((bits >> 31) & 0x7FFFFFFF)


_SUM_LANE = (LANES - 1, 0)
DIGIT_BITS = 16
DIGIT_BIAS = 1 << (DIGIT_BITS - 1)


def _dsa_kernel(iq_ref, iw_ref, ikl_ref, ikh_ref, q_ref, k_ref, v_ref, o_ref,
                key_ref, dig_ref, thr_ref, nties_ref, mb_ref, s_ref, m_ref, acc_ref, *, tq, tk, rs, topk):
    qi = pl.program_id(1)
    q0 = qi * tq
    nkc = (q0 + tq + tk - 1) // tk
    row = q0 + lax.broadcasted_iota(I32, (tq, tk), 0)
    col0 = lax.broadcasted_iota(I32, (tq, tk), 1)
    low = lax.broadcasted_iota(I32, (tq, LANES), 1) < HEAD
    reps = tk // LANES
    iw = iw_ref[0]

    def score_chunk(kc, carry):
        k0 = pl.multiple_of(kc * tk, tk)
        ikl = ikl_ref[0, pl.ds(k0, tk), :]
        ikh = ikh_ref[0, pl.ds(k0, tk), :]
        sc = jnp.zeros((tq, tk), F32)
        for j in range(IDX_HEADS // 2):
            iqp = iq_ref[0, :, j * LANES:(j + 1) * LANES]
            sc = sc + iw[:, 2 * j:2 * j + 1] * jnp.maximum(_nt(iqp, ikl), 0.0)
            sc = sc + iw[:, 2 * j + 1:2 * j + 2] * jnp.maximum(_nt(iqp, ikh), 0.0)
        sc = jnp.where(sc == 0.0, 0.0, sc)
        sc = jnp.where(col0 + k0 <= row, sc, -jnp.inf)
        key = _score_key(sc)
        key_ref[:, pl.ds(k0, tk)] = key
        dig_ref[:, pl.ds(k0, tk)] = (key >> DIGIT_BITS).astype(I16)
        return carry

    lax.fori_loop(0, nkc, score_chunk, 0)

    groups = [slice(r0, r0 + rs) for r0 in range(0, tq, rs)]
    kf = float(topk)

    def count(thrs, strict):
        accs = []
        for rows, thr in zip(groups, thrs):
            thr_t = _rep(thr.astype(I16), reps)

            def body(kc, acc, rows=rows, thr_t=thr_t):
                k0 = pl.multiple_of(kc * tk, tk)
                dig = dig_ref[rows, pl.ds(k0, tk)]
                hit = jnp.where((dig > thr_t) if strict else (dig >= thr_t), jnp.int16(1), jnp.int16(0))
                for g in range(reps):
                    acc = acc + hit[:, g * LANES:(g + 1) * LANES]
                return acc

            accs.append(lax.fori_loop(0, nkc, body, jnp.zeros((rs, LANES), I16)))
        return [jnp.sum(acc.astype(F32), axis=-1, keepdims=True) for acc in accs]

    def digit_search(wanted):
        def bit_step(i, tus):
            bit = jnp.left_shift(jnp.int32(1), DIGIT_BITS - 1 - i)
            cands = [tu | bit for tu in tus]
            cnts = count([c - DIGIT_BIAS for c in cands], False)
            return tuple(jnp.where(cnt >= w, c, tu) for cnt, w, c, tu in zip(cnts, wanted, cands, tus))

        tus = lax.fori_loop(0, DIGIT_BITS, bit_step, tuple(jnp.zeros((rs, LANES), I32) for _ in groups))
        return [tu - DIGIT_BIAS for tu in tus]

    t_hi = digit_search([kf] * len(groups))
    above = count(t_hi, True)
    for rows, th in zip(groups, t_hi):
        th_t = _rep(th, reps)

        def low_digits(kc, carry, rows=rows, th_t=th_t):
            k0 = pl.multiple_of(kc * tk, tk)
            key = key_ref[rows, pl.ds(k0, tk)]
            lo = (key & (2 * DIGIT_BIAS - 1)) - DIGIT_BIAS
            dig_ref[rows, pl.ds(k0, tk)] = jnp.where((key >> DIGIT_BITS) == th_t, lo, -DIGIT_BIAS).astype(I16)
            return carry

        lax.fori_loop(0, nkc, low_digits, 0)
    t_lo = digit_search([kf - a for a in above])
    inside = count(t_lo, True)
    for rows, th, tl, a, b in zip(groups, t_hi, t_lo, above, inside):
        thr_ref[rows, :] = th * (2 * DIGIT_BIAS) + (tl + DIGIT_BIAS)
        nties_ref[rows, :] = jnp.broadcast_to(kf - a - b, (rs, LANES))

    m_ref[...] = jnp.full(m_ref.shape, NEG_BIG, F32)
    acc_ref[...] = jnp.zeros(acc_ref.shape, F32)
    tri =jnp.where(lax.broadcasted_iota(I32, (tk, tk), 0) < lax.broadcasted_iota(I32, (tk, tk), 1),
                    1.0, 0.0).astype(BF16)

    def attend_chunk(kc, ties_before):
        k0 = pl.multiple_of(kc * tk, tk)
        keyc = key_ref[:, pl.ds(k0, tk)]
        thr_t = _rep(thr_ref[...], reps)
        eq = jnp.where(keyc == thr_t, 1.0, 0.0)
        rank = _dot(eq.astype(BF16), tri) + _rep(ties_before, reps)
        take = jnp.where(keyc > thr_t, 1.0, jnp.where(rank < _rep(nties_ref[...], reps), eq, 0.0))
        mb_ref[...] = jnp.where(col0 + k0 <= row, jnp.where(take > 0.5, 0.0, NEG_BIG), NEG_BIG)
        mb = mb_ref[...]
        m_olds, m_news = [], []
        for hd in range(DSA_HEADS):
            s = _nt(q_ref[0, :, hd * LANES:(hd + 1) * LANES],
                    k_ref[0, pl.ds(k0, tk), hd * LANES:(hd + 1) * LANES]) + mb
            s_ref[hd] = s
            m_old = m_ref[hd]
            m_olds.append(m_old)
            m_news.append(jnp.maximum(m_old, jnp.max(s, axis=-1, keepdims=True)))
        for j in range(DSA_HEADS // 2):
            vp = v_ref[0, pl.ds(k0, tk), j * LANES:(j + 1) * LANES]
            vlane = lax.broadcasted_iota(I32, vp.shape, 1)
            for a in range(2):
                hd = 2 * j + a
                keep = (vlane < HEAD) if a == 0 else (vlane >= HEAD)
                va = jnp.where(vlane == _SUM_LANE[a], jnp.ones_like(vp), jnp.where(keep, vp, jnp.zeros_like(vp)))
                p = jnp.exp2((s_ref[hd] - _rep(m_news[hd], reps)).astype(BF16))
                acc_ref[hd] = jnp.exp2(m_olds[hd] - m_news[hd]) * acc_ref[hd] + _dot(p, va)
                m_ref[hd] = m_news[hd]
        psum = eq[:, 0:LANES]
        for g in range(1, reps):
            psum = psum + eq[:, g * LANES:(g + 1) * LANES]
        return ties_before + jnp.sum(psum, axis=-1, keepdims=True)

    lax.fori_loop(0, nkc, attend_chunk, jnp.zeros((tq, LANES), F32))

    lane = lax.broadcasted_iota(I32, (tq, LANES), 1)
    for j in range(DSA_HEADS // 2):
        oa, ob = acc_ref[2 * j], acc_ref[2 * j + 1]
        la = jnp.sum(jnp.where(lane == _SUM_LANE[0], oa, 0.0), axis=-1, keepdims=True)
        lb = jnp.sum(jnp.where(lane == _SUM_LANE[1], ob, 0.0), axis=-1, keepdims=True)
        o_ref[0, :, j * LANES:(j + 1) * LANES] = jnp.where(low, oa / la, ob / lb).astype(BF16)


def _dsa_attention(iq, iw, ikl, ikh, sq, sk, sv, tq, tk, rs, topk):
    B, S, W = sq.shape
    V = sv.shape[2]
    nq = S // tq
    qblk = lambda b, i: (b, i, 0)
    full = lambda b, i: (b, 0, 0)
    once = pl.Buffered(1)
    return pl.pallas_call(
        functools.partial(_dsa_kernel, tq=tq, tk=tk, rs=rs, topk=topk),
        grid=(B, nq),
        in_specs=[pl.BlockSpec((1, tq, iq.shape[2]), qblk),
                  pl.BlockSpec((1, tq, LANES), qblk),
                  pl.BlockSpec((1, S, LANES), full, pipeline_mode=once),
                  pl.BlockSpec((1, S, LANES), full, pipeline_mode=once),
                  pl.BlockSpec((1, tq, W), qblk),
                  pl.BlockSpec((1, S, W), full, pipeline_mode=once),
                  pl.BlockSpec((1, S, V), full, pipeline_mode=once)],
        out_specs=pl.BlockSpec((1, tq, V), qblk),
        out_shape=jax.ShapeDtypeStruct((B, S, V), BF16),
        scratch_shapes=[pltpu.VMEM((tq, S), I32),
                        pltpu.VMEM((tq, S), I16),
                        pltpu.VMEM((tq, LANES), I32),
                        pltpu.VMEM((tq, LANES), F32),
                        pltpu.VMEM((tq, tk), F32),
                        pltpu.VMEM((DSA_HEADS, tq, tk), F32),
                        pltpu.VMEM((DSA_HEADS, tq, LANES), F32),
                        pltpu.VMEM((DSA_HEADS, tq, LANES), F32)],
        compiler_params=_params(("parallel", "arbitrary")),
        name="dsa_attention",
    )(iq, iw, ikl, ikh, sq, sk, sv)


def _mix_kernel(do_ref, so_ref, x_ref, mod_ref, g2_ref, wo1_ref, wo2_ref, wsg_ref, wsu_ref, wsd_ref,
                rwt_ref, base_ref, h2_ref, lg_ref):
    mix = _dot(do_ref[0], wo1_ref[...]) + _dot(so_ref[0], wo2_ref[...])
    mod = mod_ref[0]
    x1 = x_ref[0] + mod[2:3] * mix
    ms = jnp.mean(x1 * x1, axis=-1, keepdims=True)
    h2 = x1 * lax.rsqrt(ms + EPS) * g2_ref[...] * (1.0 + mod[4:5]) + mod[3:4]
    hb = h2.astype(BF16)
    gate = _dot(hb, wsg_ref[...])
    up = _dot(hb, wsu_ref[...])
    act = gate / (1.0 + jnp.exp(-gate)) * up
    shared = _dot(act.astype(BF16), wsd_ref[...])
    base_ref[0] = x1 + mod[5:6] * shared
    lg_ref[...] = _nt(rwt_ref[...], hb)
    hf = hb.astype(F32)
    tm = hf.shape[0]
    for j in range(ROW_TILES):
        h2_ref[pl.ds(j, tm, stride=ROW_TILES), :] = hf[:, j * LANES:(j + 1) * LANES]


def _mix(diff_out, dsa_out, x, mod3, g2, wo1, wo2, wsg, wsu, wsd, rwt, tm):
    B, S, D = x.shape
    ns = S // tm
    T = B * S
    tok = lambda b, i: (b, i, 0)
    c2 = lambda b, i: (0, 0)
    return pl.pallas_call(
        _mix_kernel,
        grid=(B, ns),
        in_specs=[pl.BlockSpec((1, tm, SEG), tok), pl.BlockSpec((1, tm, SEG), tok),
                  pl.BlockSpec((1, tm, D), tok),
                  pl.BlockSpec((1, 6, D), lambda b, i: (b, 0, 0)),
                  pl.BlockSpec((1, D), c2),
                  pl.BlockSpec(wo1.shape, c2), pl.BlockSpec(wo2.shape, c2),
                  pl.BlockSpec(wsg.shape, c2), pl.BlockSpec(wsu.shape, c2), pl.BlockSpec(wsd.shape, c2),
                  pl.BlockSpec(rwt.shape, c2)],
        out_specs=[pl.BlockSpec((1, tm, D), tok),
                   pl.BlockSpec((tm * ROW_TILES, LANES), lambda b, i: (b * ns + i, 0)),
                   pl.BlockSpec((N_EXPERTS, tm), lambda b, i: (0, b * ns + i))],
        out_shape=[jax.ShapeDtypeStruct((B, S, D), F32),
                   jax.ShapeDtypeStruct((T * ROW_TILES, LANES), F32),
                   jax.ShapeDtypeStruct((N_EXPERTS, T), F32)],
        compiler_params=_params(("parallel", "parallel")),
        name="mix_shared_router",
    )(diff_out, dsa_out, x, mod3, g2, wo1, wo2, wsg, wsu, wsd, rwt)


def _first_max(v, idx, sentinel):
    m = jnp.max(v, axis=0, keepdims=True)
    i = jnp.min(jnp.where(v == m, idx, sentinel), axis=0, keepdims=True)
    return m, i


def _route_kernel(lg_ref, bias_ref, eidx_ref, gate_ref):
    lg = lg_ref[...]
    tt = lg.shape[1]
    scores = 1.0 / (1.0 + jnp.exp(-lg))
    biased = scores + bias_ref[...]
    gi = lax.broadcasted_iota(I32, (GROUP_SIZE, tt), 0).astype(F32)
    gscore = []
    for g in range(N_GROUPS):
        blk = biased[g * GROUP_SIZE:(g + 1) * GROUP_SIZE, :]
        m1, i1 = _first_max(blk, gi, float(GROUP_SIZE))
        m2 = jnp.max(jnp.where(gi == i1, -jnp.inf, blk), axis=0, keepdims=True)
        gscore.append(m1 + m2)
    gs = jnp.concatenate(gscore, axis=0)
    gidx = lax.broadcasted_iota(I32, (N_GROUPS, tt), 0).astype(F32)
    chosen = jnp.zeros((N_GROUPS, tt), F32)
    for _ in range(TOPK_GROUPS):
        _, ig = _first_max(gs, gidx, float(N_GROUPS))
        hit = gidx == ig
        chosen = jnp.where(hit, 1.0, chosen)
        gs = jnp.where(hit, -jnp.inf, gs)
    masked = jnp.concatenate(
        [jnp.where(chosen[g:g + 1, :] > 0.5, biased[g * GROUP_SIZE:(g + 1) * GROUP_SIZE, :], -jnp.inf)
         for g in range(N_GROUPS)], axis=0)
    ei = lax.broadcasted_iota(I32, (N_EXPERTS, tt), 0).astype(F32)
    ids, ws = [], []
    for _ in range(TOP_K):
        _, ie = _first_max(masked, ei, float(N_EXPERTS))
        hit = ei == ie
        ws.append(jnp.sum(jnp.where(hit, scores, 0.0), axis=0, keepdims=True))
        ids.append(ie)
        masked = jnp.where(hit, -jnp.inf, masked)
    w = jnp.concatenate(ws, axis=0)
    gate_ref[...] = w / jnp.sum(w, axis=0, keepdims=True) * ROUTED_SCALE
    eidx_ref[...] = jnp.concatenate(ids, axis=0).astype(I32)


def _route(logits_t, bias_col, tt):
    E, T = logits_t.shape
    return pl.pallas_call(
        _route_kernel,
        grid=(T // tt,),
        in_specs=[pl.BlockSpec((E, tt), lambda i: (0, i)),
                  pl.BlockSpec((E, 1), lambda i: (0, 0))],
        out_specs=[pl.BlockSpec((TOP_K, tt), lambda i: (0, i)),
                   pl.BlockSpec((TOP_K, tt), lambda i: (0, i))],
        out_shape=[jax.ShapeDtypeStruct((TOP_K, T), I32),
                   jax.ShapeDtypeStruct((TOP_K, T), F32)],
        compiler_params=_params(("parallel",)),
        name="route",
    )(logits_t, bias_col)


def _plan_kernel(eidx_ref, dest_ref, bexp_ref, nused_ref, cnt_col, cnt_row, slot_base, *, blk, nb_pad):
    ph = pl.program_id(0)
    i = pl.program_id(1)
    tt = eidx_ref.shape[1]
    eidx = eidx_ref[...]
    ei = lax.broadcasted_iota(I32, (N_EXPERTS, tt), 0)
    onehot = jnp.zeros((N_EXPERTS, tt), F32)
    for k in range(TOP_K):
        onehot = onehot + jnp.where(ei == eidx[k:k + 1, :], 1.0, 0.0)
    oh = onehot.astype(BF16)

    @pl.when((ph == 0) & (i == 0))
    def _():
        cnt_col[...] = jnp.zeros(cnt_col.shape, F32)
        cnt_row[...] = jnp.zeros(cnt_row.shape, F32)

    @pl.when(ph == 0)
    def _():
        cnt_col[...] += _dot(oh, jnp.ones((tt, LANES), BF16))
        cnt_row[...] += _nt(jnp.ones((8, tt), BF16), oh)

    @pl.when((ph == 1) & (i == 0))
    def _():
        inv = 1.0 / blk
        nb_col = jnp.floor((cnt_col[:, 0:1] + (blk - 1)) * inv)
        nb_row = jnp.floor((cnt_row[0:1, :] + (blk - 1)) * inv)
        r = lax.broadcasted_iota(I32, (N_EXPERTS, N_EXPERTS), 0)
        c = lax.broadcasted_iota(I32, (N_EXPERTS, N_EXPERTS), 1)
        bstart = jnp.sum(jnp.where(c < r, nb_row, 0.0), axis=-1, keepdims=True)
        bend = bstart + nb_col
        slot_base[...] = bstart * blk
        jb = lax.broadcasted_iota(I32, (N_EXPERTS, nb_pad), 1).astype(F32)
        be = jnp.sum(jnp.where(bend <= jb, 1.0, 0.0), axis=0, keepdims=True)
        bexp_ref[...] = jnp.minimum(be, N_EXPERTS - 1.0).astype(I32)
        nused_ref[...] = jnp.broadcast_to(jnp.sum(nb_row, axis=-1, keepdims=True), nused_ref.shape).astype(I32)

    @pl.when(ph == 1)
    def _():
        tri = jnp.where(lax.broadcasted_iota(I32, (tt, tt), 0) < lax.broadcasted_iota(I32, (tt, tt), 1),
                        1.0, 0.0).astype(BF16)
        slot = _dot(oh, tri) + slot_base[...]
        for k in range(TOP_K):
            dk = jnp.sum(jnp.where(ei == eidx[k:k + 1, :], slot, 0.0), axis=0, keepdims=True)
            dest_ref[k:k + 1, :] = dk.astype(I32)
        slot_base[...] += jnp.sum(onehot, axis=-1, keepdims=True)


def _plan(eidx, tt, blk, nb_pad):
    K, T = eidx.shape
    nt = T // tt
    return pl.pallas_call(
        functools.partial(_plan_kernel, blk=blk, nb_pad=nb_pad),
        grid=(2, nt),
        in_specs=[pl.BlockSpec((K, tt), lambda p, i: (0, i))],
        out_specs=[pl.BlockSpec((K, tt), lambda p, i: (0, i * p)),
                   pl.BlockSpec((1, nb_pad), lambda p, i: (0, 0)),
                   pl.BlockSpec((1, LANES), lambda p, i: (0, 0))],
        out_shape=[jax.ShapeDtypeStruct((K, T), I32),
                   jax.ShapeDtypeStruct((1, nb_pad), I32),
                   jax.ShapeDtypeStruct((1, LANES), I32)],
        scratch_shapes=[pltpu.VMEM((N_EXPERTS, LANES), F32),
                        pltpu.VMEM((8, N_EXPERTS), F32),
                        pltpu.VMEM((N_EXPERTS, 1), F32)],
        compiler_params=_params(("arbitrary", "arbitrary")),
        name="plan",
    )(eidx)


def _row(ref, r):
    return ref.at[pl.ds(pl.multiple_of(r * ROW_TILES, ROW_TILES), ROW_TILES), :]


def _dispatch_kernel(dest_ref, h_ref, xs_in_ref, xs_ref, sem):
    del xs_in_ref
    tt = h_ref.shape[0] // ROW_TILES

    def issue(t, c):
        for k in range(TOP_K):
            pltpu.make_async_copy(_row(h_ref, t), _row(xs_ref, dest_ref[k, t]), sem).start()
        return c

    lax.fori_loop(0, tt, issue, 0)

    def drain(t, c):
        for k in range(TOP_K):
            pltpu.make_async_copy(_row(h_ref, 0), _row(xs_ref, 0), sem).wait()
        return c

    lax.fori_loop(0, tt, drain, 0)


def _dispatch(dest, h2rows, xs_init, tt):
    T = h2rows.shape[0] // ROW_TILES
    return pl.pallas_call(
        _dispatch_kernel,
        grid=(T // tt,),
        in_specs=[pl.BlockSpec((TOP_K, tt), lambda i: (0, i), memory_space=pltpu.SMEM),
                  pl.BlockSpec((tt * ROW_TILES, LANES), lambda i: (i, 0)),
                  pl.BlockSpec(memory_space=pl.ANY)],
        out_specs=pl.BlockSpec(memory_space=pl.ANY),
        out_shape=jax.ShapeDtypeStruct(xs_init.shape, xs_init.dtype),
        scratch_shapes=[pltpu.SemaphoreType.DMA(())],
        input_output_aliases={2: 0},
        compiler_params=_params(("arbitrary",), has_side_effects=True),
        name="dispatch",
    )(dest, h2rows, xs_init)


def _experts_kernel(bexp_ref, nused_ref, xs_ref, wg_ref, wu_ref, wd_ref, y_ref, xb_ref):
    del bexp_ref

    @pl.when(pl.program_id(0) < nused_ref[0])
    def _():
        blk = xb_ref.shape[0]
        for j in range(ROW_TILES):
            xb_ref[:, j * LANES:(j + 1) * LANES] = xs_ref[pl.ds(j, blk, stride=ROW_TILES), :].astype(BF16)
        xb = xb_ref[...]
        gate = _dot(xb, wg_ref[0])
        up = _dot(xb, wu_ref[0])
        act = gate / (1.0 + jnp.exp(-gate)) * up
        y = _dot(act.astype(BF16), wd_ref[0])
        for j in range(ROW_TILES):
            y_ref[pl.ds(j, blk, stride=ROW_TILES), :] = y[:, j * LANES:(j + 1) * LANES]


def _experts(bexp, nused, xs, wg, wu, wd, blk, n_blocks):
    live = lambda j, be, nu: jnp.minimum(j, nu[0] - 1)
    row_spec = pl.BlockSpec((blk * ROW_TILES, LANES), lambda j, be, nu: (live(j, be, nu), 0))
    wspec = lambda w: pl.BlockSpec((1,) + w.shape[1:], lambda j, be, nu: (be[live(j, be, nu)], 0, 0))
    return pl.pallas_call(
        _experts_kernel,
        grid_spec=pltpu.PrefetchScalarGridSpec(
            num_scalar_prefetch=2,
            grid=(n_blocks,),
            in_specs=[row_spec, wspec(wg), wspec(wu), wspec(wd)],
            out_specs=row_spec,
            scratch_shapes=[pltpu.VMEM((blk, D_MODEL), BF16)]),
        out_shape=jax.ShapeDtypeStruct(xs.shape, F32),
        compiler_params=_params(("arbitrary",)),
        name="experts",
    )(bexp, nused, xs, wg, wu, wd)


def _combine_kernel(dest_ref, gate_ref, base_ref, mod_ref, y_ref, o_ref, buf, sem):
    tt = base_ref.shape[1]

    def issue(t, c):
        for k in range(TOP_K):
            pltpu.make_async_copy(_row(y_ref, dest_ref[k, t]), _row(buf, k * tt + t), sem).start()
        return c

    lax.fori_loop(0, tt, issue, 0)

    def drain(t, c):
        for k in range(TOP_K):
            pltpu.make_async_copy(_row(y_ref, 0), _row(buf, 0), sem).wait()
        return c

    lax.fori_loop(0, tt, drain, 0)

    gates = gate_ref[...]
    g2 = mod_ref[0][5:6]
    for j in range(D_MODEL // LANES):
        acc = jnp.zeros((tt, LANES), F32)
        for k in range(TOP_K):
            acc = acc + gates[:, k:k + 1] * buf[pl.ds(k * tt * ROW_TILES + j, tt, stride=ROW_TILES), :]
        cols = slice(j * LANES, (j + 1) * LANES)
        o_ref[0, :, cols] = base_ref[0, :, cols] + g2[:, cols] * acc


def _combine(dest, gates_tk, base, mod3, y, tt):
    B, S, D = base.shape
    ns = S // tt
    return pl.pallas_call(
        _combine_kernel,
        grid=(B, ns),
        in_specs=[pl.BlockSpec((TOP_K, tt), lambda b, i: (0, b * ns + i), memory_space=pltpu.SMEM),
                  pl.BlockSpec((tt, TOP_K), lambda b, i: (b * ns + i, 0)),
                  pl.BlockSpec((1, tt, D), lambda b, i: (b, i, 0)),
                  pl.BlockSpec((1, 6, D), lambda b, i: (b, 0, 0)),
                  pl.BlockSpec(memory_space=pl.ANY)],
        out_specs=pl.BlockSpec((1, tt, D), lambda b, i: (b, i, 0)),
        out_shape=jax.ShapeDtypeStruct((B, S, D), F32),
        scratch_shapes=[pltpu.VMEM((TOP_K * tt * ROW_TILES, LANES), F32),
                        pltpu.SemaphoreType.DMA(())],
        compiler_params=_params(("arbitrary", "arbitrary")),
        name="combine",
    )(dest, gates_tk, base, mod3, y)


def _alibi_q_features(n_heads, maps_per_head):
    slopes = 2.0 ** (-8.0 * jnp.arange(1, n_heads + 1, dtype=F32) / n_heads)
    c = jnp.repeat(slopes, maps_per_head) * LOG2E * POS_RADIX
    pieces = []
    rest = c
    for _ in range(N_SPLIT):
        p = rest.astype(BF16).astype(F32)
        pieces.append(p)
        rest = rest - p
    hi = jnp.stack(pieces, axis=1)
    feat = jnp.concatenate([hi, hi / POS_RADIX], axis=1)
    return jnp.pad(feat, ((0, 0), (HEAD, LANES - HEAD - 2 * N_SPLIT)))


def kernel(x, c, ada_w, ada_b, norm1_g, norm2_g, w_in, diff_q_norm_g, diff_k_norm_g, lam_q1, lam_k1, lam_q2, lam_k2, diff_subln_g, dsa_q_norm_g, dsa_k_norm_g, idx_k_norm_g, w_out, router_w, router_bias, exp_w_gate, exp_w_up, exp_w_down, shared_w_gate, shared_w_up, shared_w_down):
    B, S, D = x.shape
    assert D == D_MODEL and ada_w.shape[0] == 1 and S <= POS_RADIX * 128
    T = B * S
    topk = min(DSA_TOPK, S // 4)
    tm = min(512, S)
    tq_diff = min(512, S)
    tq_dsa = min(256, S)
    tk_dsa = min(512, S)
    rs_dsa = min(128, tq_dsa)
    tt_route = min(512, T)
    tt_move = min(128, S)
    blk = 512
    n_blocks = (T * TOP_K) // blk + N_EXPERTS
    nb_pad = -(-n_blocks // LANES) * LANES

    n_main = 7 * SEG
    wm = w_in[0, :, :n_main].astype(BF16)
    wt = jnp.pad(w_in[0, :, n_main:], ((0, 0), (0, LANES - (IDX_DIM + IDX_HEADS)))).astype(BF16)
    tile8 = lambda g: jnp.tile(g[0], SEG // g.shape[1]).reshape(1, SEG)
    gik = jnp.pad(idx_k_norm_g[0], (0, LANES - IDX_DIM)).reshape(1, LANES)
    lamv = jnp.concatenate([lam_q1, lam_k1, lam_q2, lam_k2], axis=0)
    wo1 = w_out[0, :SEG].astype(BF16)
    wo2 = w_out[0, SEG:].astype(BF16)
    rwt = router_w[0].T.astype(BF16)
    wg = exp_w_gate[0].astype(BF16)
    wu = exp_w_up[0].astype(BF16)
    wd = exp_w_down[0].astype(BF16)

    mod3 = _ada(c, ada_w[0], ada_b[0]).reshape(B, 6, D)

    dq, dk, dv, sq, sk, sv, iq, ikl, ikh, iw = _inproj(
        x, mod3, norm1_g, wm, wt, tile8(diff_q_norm_g), tile8(diff_k_norm_g),
        tile8(dsa_q_norm_g), tile8(dsa_k_norm_g), gik,
        _alibi_q_features(DIFF_HEADS, 2), _alibi_q_features(DSA_HEADS, 1), tm)

    diff_out = _diff_attention(dq, dk, dv, lamv, diff_subln_g, tq_diff)
    dsa_out = _dsa_attention(iq, iw, ikl, ikh, sq, sk, sv, tq_dsa, tk_dsa, rs_dsa, topk)

    base, h2rows, logits_t = _mix(diff_out, dsa_out, x, mod3, norm2_g, wo1, wo2,
                                  shared_w_gate[0].astype(BF16), shared_w_up[0].astype(BF16),
                                  shared_w_down[0].astype(BF16), rwt, tm)

    eidx, gates = _route(logits_t, router_bias[0].reshape(N_EXPERTS, 1), tt_route)
    dest, bexp, nused = _plan(eidx, tt_route, blk, nb_pad)

    xs = _dispatch(dest, h2rows, jnp.zeros((n_blocks * blk * ROW_TILES, LANES), F32), tt_move)
    y = _experts(bexp.reshape(nb_pad), nused[0, :1], xs, wg, wu, wd, blk, n_blocks)
    return _combine(dest, gates.T, base, mod3, y, tt_move)
```

```python
import functools
import math

import jax
import jax.numpy as jnp
from jax import lax
from jax.experimental import pallas as pl
from jax.experimental.pallas import tpu as pltpu

F32 = jnp.float32
BF16 = jnp.bfloat16
I32 = jnp.int32
I16 = jnp.int16

D_MODEL = 1024
DIFF_HEADS = 4
DIFF_HEAD_DIM = 64
DSA_HEADS = 8
DSA_HEAD_DIM = 64
IDX_HEADS = 8
IDX_DIM = 64
DSA_TOPK = 256
N_EXPERTS = 256
TOP_K = 8
N_GROUPS = 8
GROUP_SIZE = N_EXPERTS // N_GROUPS
TOPK_GROUPS = 4
D_EXPERT = 256
D_SHARED = 256
ROUTED_SCALE = 2.5
EPS = 1e-6
LAM_INIT = 0.2

LANES = 128
ROW_TILES = D_MODEL // LANES
PACK_TILES = ROW_TILES // 2
SEG = 512
HEAD = 64
N_MAPS = SEG // HEAD
WIDE = N_MAPS * LANES
POS_RADIX = 64
N_SPLIT = 3
NEG_BIG = -1e30
INT_MIN = -2147483648
LOG2E = math.log2(math.e)
VMEM_LIMIT = 56 * 1024 * 1024

NT_DIMS = (((1,), (1,)), ((), ()))


def _nt(a, b):
    return lax.dot_general(a, b, NT_DIMS, preferred_element_type=F32)


def _dot(a, b):
    return jnp.dot(a, b, preferred_element_type=F32)


def _rep(x, reps):
    return jnp.concatenate([x] * reps, axis=1)


def _params(sem, vmem=VMEM_LIMIT, **kw):
    return pltpu.CompilerParams(dimension_semantics=sem, vmem_limit_bytes=vmem, **kw)


def _ada_kernel(c_ref, w_ref, b_ref, o_ref):
    c = c_ref[...]
    s = c / (1.0 + jnp.exp(-c))
    o_ref[...] = jnp.dot(s, w_ref[...], preferred_element_type=F32,
                         precision=lax.Precision.HIGHEST) + b_ref[...]


def _ada(c, w, b):
    B, D = c.shape
    N = w.shape[1]
    tn = D
    return pl.pallas_call(
        _ada_kernel,
        grid=(N // tn,),
        in_specs=[pl.BlockSpec((B, D), lambda j: (0, 0)),
                  pl.BlockSpec((D, tn), lambda j: (0, j)),
                  pl.BlockSpec((1, tn), lambda j: (0, j))],
        out_specs=pl.BlockSpec((B, tn), lambda j: (0, j)),
        out_shape=jax.ShapeDtypeStruct((B, N), F32),
        compiler_params=_params(("arbitrary",)),
        name="ada",
    )(c, w, b.reshape(1, N))


def _group_sumsq(z):
    n = z.shape[1]
    r = lax.broadcasted_iota(I32, (n, n), 0) // HEAD
    c = lax.broadcasted_iota(I32, (n, n), 1) // HEAD
    bd = jnp.where(r == c, 1.0, 0.0).astype(BF16)
    zz = z * z
    hi = zz.astype(BF16)
    lo = (zz - hi.astype(F32)).astype(BF16)
    return _dot(hi, bd) + _dot(lo, bd)


def _inproj_kernel(x_ref, mod_ref, g1_ref, wm_ref, wt_ref, gq_ref, gk_ref, gsq_ref, gsk_ref, gik_ref,
                   fdq_ref, fsq_ref,
                   dq_ref, dk_ref, dv_ref, sq_ref, sk_ref, sv_ref, iq_ref, ikl_ref, ikh_ref, iw_ref):
    x = x_ref[0]
    tm = x.shape[0]
    ms = jnp.mean(x * x, axis=-1, keepdims=True)
    y = x * lax.rsqrt(ms + EPS) * g1_ref[...]
    mod = mod_ref[0]
    h = y * (1.0 + mod[1:2]) + mod[0:1]
    hb = h.astype(BF16)

    lane = lax.broadcasted_iota(I32, (tm, LANES), 1)
    is_head = lane < HEAD
    kpos = pl.program_id(1) * tm + lax.broadcasted_iota(I32, (tm, LANES), 0)
    hi_digit = (kpos // POS_RADIX).astype(F32)
    lo_digit = (kpos % POS_RADIX).astype(F32)
    kfeat = jnp.where(lane < HEAD + N_SPLIT, hi_digit, jnp.where(lane < HEAD + 2 * N_SPLIT, lo_digit, 0.0))

    def plain(seg_idx, out_ref):
        out_ref[0] = _dot(hb, wm_ref[:, seg_idx * SEG:(seg_idx + 1) * SEG]).astype(BF16)

    def normed(seg_idx, g_ref, scale, feat_ref, out_ref):
        half = SEG // 2
        for i in range(2):
            lo = seg_idx * SEG + i * half
            z = _dot(hb, wm_ref[:, lo:lo + half])
            ss = _group_sumsq(z)
            zn = z * lax.rsqrt(ss * (1.0 / HEAD) + EPS) * (g_ref[:, i * half:(i + 1) * half] * scale)
            for g in range(half // LANES):
                zg = zn[:, g * LANES:(g + 1) * LANES]
                for odd in range(2):
                    idx = i * (half // HEAD) + 2 * g + odd
                    src = zg if odd == 0 else pltpu.roll(zg, HEAD, 1)
                    feat = kfeat if feat_ref is None else feat_ref[idx:idx + 1, :]
                    out_ref[0, :, idx * LANES:(idx + 1) * LANES] = jnp.where(is_head, src, feat).astype(BF16)

    normed(0, gq_ref, HEAD ** -0.5 * LOG2E, fdq_ref, dq_ref)
    normed(1, gk_ref, 1.0, None, dk_ref)
    plain(2, dv_ref)
    normed(3, gsq_ref, HEAD ** -0.5 * LOG2E, fsq_ref, sq_ref)
    normed(4, gsk_ref, 1.0, None, sk_ref)
    plain(5, sv_ref)
    plain(6, iq_ref)

    t = _dot(hb, wt_ref[...])
    ikraw = jnp.where(lane < IDX_DIM, t, 0.0)
    ss = jnp.sum(ikraw * ikraw, axis=-1, keepdims=True) * (1.0 / IDX_DIM)
    ikn = ikraw * lax.rsqrt(ss + EPS) * gik_ref[...]
    ikl_ref[0] = ikn.astype(BF16)
    ikh_ref[0] = pltpu.roll(ikn, IDX_DIM, 1).astype(BF16)
    iwraw = jnp.where((lane >= IDX_DIM) & (lane < IDX_DIM + IDX_HEADS), t, 0.0)
    iw_ref[0] = pltpu.roll(iwraw * (IDX_HEADS ** -0.5), LANES - IDX_DIM, 1) * (IDX_DIM ** -0.5)


def _inproj(x, mod3, g1, wm, wt, gq, gk, gsq, gsk, gik, fdq, fsq, tm):
    B, S, D = x.shape
    ns = S // tm
    tok = lambda b, i: (b, i, 0)
    const2 = lambda b, i: (0, 0)
    seg_spec = pl.BlockSpec((1, tm, SEG), tok)
    wide_spec = pl.BlockSpec((1, tm, WIDE), tok)
    lane_spec = pl.BlockSpec((1, tm, LANES), tok)
    seg_shape = jax.ShapeDtypeStruct((B, S, SEG), BF16)
    wide_shape = jax.ShapeDtypeStruct((B, S, WIDE), BF16)
    return pl.pallas_call(
        _inproj_kernel,
        grid=(B, ns),
        in_specs=[pl.BlockSpec((1, tm, D), tok),
                  pl.BlockSpec((1, 6, D), lambda b, i: (b, 0, 0)),
                  pl.BlockSpec((1, D), const2),
                  pl.BlockSpec(wm.shape, const2),
                  pl.BlockSpec(wt.shape, const2),
                  pl.BlockSpec((1, SEG), const2), pl.BlockSpec((1, SEG), const2),
                  pl.BlockSpec((1, SEG), const2), pl.BlockSpec((1, SEG), const2),
                  pl.BlockSpec((1, LANES), const2),
                  pl.BlockSpec((N_MAPS, LANES), const2), pl.BlockSpec((N_MAPS, LANES), const2)],
        out_specs=[wide_spec, wide_spec, seg_spec, wide_spec, wide_spec, seg_spec, seg_spec,
                   lane_spec, lane_spec, lane_spec],
        out_shape=[wide_shape, wide_shape, seg_shape, wide_shape, wide_shape, seg_shape, seg_shape,
                   jax.ShapeDtypeStruct((B, S, LANES), BF16), jax.ShapeDtypeStruct((B, S, LANES), BF16),
                   jax.ShapeDtypeStruct((B, S, LANES), F32)],
        compiler_params=_params(("parallel", "parallel")),
        name="inproj",
    )(x, mod3, g1, wm, wt, gq, gk, gsq, gsk, gik, fdq, fsq)


def _diff_kernel(q_ref, k_ref, v_ref, lam_ref, g_ref, o_ref, s_ref, m_ref, l_ref, acc_ref, *, tq):
    qi = pl.program_id(1)
    ki = pl.program_id(2)
    n_maps = 2 * DIFF_HEADS
    reps = tq // LANES

    @pl.when(ki == 0)
    def _():
        m_ref[...] = jnp.full(m_ref.shape, NEG_BIG, F32)
        l_ref[...] = jnp.zeros(l_ref.shape, F32)
        acc_ref[...] = jnp.zeros(acc_ref.shape, F32)

    def step(diag):
        if diag:
            row = lax.broadcasted_iota(I32, (tq, tq), 0)
            col = lax.broadcasted_iota(I32, (tq, tq), 1)
            causal_bias = jnp.where(col <= row, 0.0, NEG_BIG)
        m_olds, m_news = [], []
        for idx in range(n_maps):
            s = _nt(q_ref[0, :, idx * LANES:(idx + 1) * LANES], k_ref[0, :, idx * LANES:(idx + 1) * LANES])
            if diag:
                s = s + causal_bias
            s_ref[idx] = s
            m_old = m_ref[idx]
            m_olds.append(m_old)
            m_news.append(jnp.maximum(m_old, jnp.max(s, axis=-1, keepdims=True)))
        for idx in range(n_maps):
            h = idx // 2
            p = jnp.exp2(s_ref[idx] - _rep(m_news[idx], reps))
            alpha = jnp.exp2(m_olds[idx] - m_news[idx])
            psum = p[:, 0:LANES]
            for g in range(1, reps):
                psum = psum + p[:, g * LANES:(g + 1) * LANES]
            l_ref[idx] = alpha * l_ref[idx] + psum
            acc_ref[idx] = alpha * acc_ref[idx] + _dot(p.astype(BF16), v_ref[0, :, h * LANES:(h + 1) * LANES])
            m_ref[idx] = m_news[idx]

    @pl.when(ki < qi)
    def _():
        step(False)

    @pl.when(ki == qi)
    def _():
        step(True)
        lv = lam_ref[...]
        lam = (jnp.exp(jnp.sum(lv[0:1] * lv[1:2], axis=-1, keepdims=True))
               - jnp.exp(jnp.sum(lv[2:3] * lv[3:4], axis=-1, keepdims=True)) + LAM_INIT)
        for h in range(DIFF_HEADS):
            o1 = acc_ref[2 * h] / jnp.sum(l_ref[2 * h], axis=-1, keepdims=True)
            o2 = acc_ref[2 * h + 1] / jnp.sum(l_ref[2 * h + 1], axis=-1, keepdims=True)
            o = o1 - lam * o2
            ms = jnp.mean(o * o, axis=-1, keepdims=True)
            on = o * lax.rsqrt(ms + EPS) * g_ref[...]
            o_ref[0, :, h * LANES:(h + 1) * LANES] = (on * (1.0 - LAM_INIT)).astype(BF16)


def _diff_attention(dq, dk, dv, lamv, subln_g, tq):
    B, S, W = dq.shape
    V = dv.shape[2]
    nq = S // tq
    n_maps = 2 * DIFF_HEADS
    return pl.pallas_call(
        functools.partial(_diff_kernel, tq=tq),
        grid=(B, nq, nq),
        in_specs=[pl.BlockSpec((1, tq, W), lambda b, i, j: (b, i, 0)),
                  pl.BlockSpec((1, tq, W), lambda b, i, j: (b, jnp.minimum(i, j), 0)),
                  pl.BlockSpec((1, tq, V), lambda b, i, j: (b, jnp.minimum(i, j), 0)),
                  pl.BlockSpec(lamv.shape, lambda b, i, j: (0, 0)),
                  pl.BlockSpec((1, LANES), lambda b, i, j: (0, 0))],
        out_specs=pl.BlockSpec((1, tq, V), lambda b, i, j: (b, i, 0)),
        out_shape=jax.ShapeDtypeStruct((B, S, V), BF16),
        scratch_shapes=[pltpu.VMEM((n_maps, tq, tq), F32),
                        pltpu.VMEM((n_maps, tq, LANES), F32),
                        pltpu.VMEM((n_maps, tq, LANES), F32),
                        pltpu.VMEM((n_maps, tq, LANES), F32)],
        compiler_params=_params(("parallel", "parallel", "arbitrary")),
        name="diff_attention",
    )(dq, dk, dv, lamv, subln_g)


def _score_key(v):
    bits = lax.bitcast_convert_type(v, I32)
    return bits ^ ((bits >> 31) & 0x7FFFFFFF)


_SUM_LANE = (LANES - 1, 0)
DIGIT_BITS = 16
DIGIT_BIAS = 1 << (DIGIT_BITS - 1)


def _dsa_kernel(iq_ref, iw_ref, ikl_ref, ikh_ref, q_ref, k_ref, v_ref, o_ref,
                key_ref, dig_ref, thr_ref, nties_ref, mb_ref, s_ref, m_ref, acc_ref, *, tq, tk, rs, topk):
    qi = pl.program_id(1)
    q0 = qi * tq
    nkc = (q0 + tq + tk - 1) // tk
    row = q0 + lax.broadcasted_iota(I32, (tq, tk), 0)
    col0 = lax.broadcasted_iota(I32, (tq, tk), 1)
    low = lax.broadcasted_iota(I32, (tq, LANES), 1) < HEAD
    reps = tk // LANES
    iw = iw_ref[0]

    def score_chunk(kc, carry):
        k0 = pl.multiple_of(kc * tk, tk)
        ikl = ikl_ref[0, pl.ds(k0, tk), :]
        ikh = ikh_ref[0, pl.ds(k0, tk), :]
        sc = jnp.zeros((tq, tk), F32)
        for j in range(IDX_HEADS // 2):
            iqp = iq_ref[0, :, j * LANES:(j + 1) * LANES]
            sc = sc + iw[:, 2 * j:2 * j + 1] * jnp.maximum(_nt(iqp, ikl), 0.0)
            sc = sc + iw[:, 2 * j + 1:2 * j + 2] * jnp.maximum(_nt(iqp, ikh), 0.0)
        sc = jnp.where(sc == 0.0, 0.0, sc)
        sc = jnp.where(col0 + k0 <= row, sc, -jnp.inf)
        key = _score_key(sc)
        key_ref[:, pl.ds(k0, tk)] = key
        dig_ref[:, pl.ds(k0, tk)] = (key >> DIGIT_BITS).astype(I16)
        return carry

    lax.fori_loop(0, nkc, score_chunk, 0)

    groups = [slice(r0, r0 + rs) for r0 in range(0, tq, rs)]
    kf = float(topk)

    def count(thrs, strict):
        accs = []
        for rows, thr in zip(groups, thrs):
            thr_t = _rep(thr.astype(I16), reps)

            def body(kc, acc, rows=rows, thr_t=thr_t):
                k0 = pl.multiple_of(kc * tk, tk)
                dig = dig_ref[rows, pl.ds(k0, tk)]
                hit = jnp.where((dig > thr_t) if strict else (dig >= thr_t), jnp.int16(1), jnp.int16(0))
                for g in range(reps):
                    acc = acc + hit[:, g * LANES:(g + 1) * LANES]
                return acc

            accs.append(lax.fori_loop(0, nkc, body, jnp.zeros((rs, LANES), I16)))
        return [jnp.sum(acc.astype(F32), axis=-1, keepdims=True) for acc in accs]

    def digit_search(wanted):
        def bit_step(i, tus):
            bit = jnp.left_shift(jnp.int32(1), DIGIT_BITS - 1 - i)
            cands = [tu | bit for tu in tus]
            cnts = count([c - DIGIT_BIAS for c in cands], False)
            return tuple(jnp.where(cnt >= w, c, tu) for cnt, w, c, tu in zip(cnts, wanted, cands, tus))

        tus = lax.fori_loop(0, DIGIT_BITS, bit_step, tuple(jnp.zeros((rs, LANES), I32) for _ in groups))
        return [tu - DIGIT_BIAS for tu in tus]

    t_hi = digit_search([kf] * len(groups))
    above = count(t_hi, True)
    for rows, th in zip(groups, t_hi):
        th_t = _rep(th, reps)

        def low_digits(kc, carry, rows=rows, th_t=th_t):
            k0 = pl.multiple_of(kc * tk, tk)
            key = key_ref[rows, pl.ds(k0, tk)]
            lo = (key & (2 * DIGIT_BIAS - 1)) - DIGIT_BIAS
            dig_ref[rows, pl.ds(k0, tk)] = jnp.where((key >> DIGIT_BITS) == th_t, lo, -DIGIT_BIAS).astype(I16)
            return carry

        lax.fori_loop(0, nkc, low_digits, 0)
    t_lo = digit_search([kf - a for a in above])
    inside = count(t_lo, True)
    for rows, th, tl, a, b in zip(groups, t_hi, t_lo, above, inside):
        thr_ref[rows, :] = th * (2 * DIGIT_BIAS) + (tl + DIGIT_BIAS)
        nties_ref[rows, :] = jnp.broadcast_to(kf - a - b, (rs, LANES))

    m_ref[...] = jnp.full(m_ref.shape, NEG_BIG, F32)
    acc_ref[...] = jnp.zeros(acc_ref.shape, F32)
    tri =jnp.where(lax.broadcasted_iota(I32, (tk, tk), 0) < lax.broadcasted_iota(I32, (tk, tk), 1),
                    1.0, 0.0).astype(BF16)

    def attend_chunk(kc, ties_before):
        k0 = pl.multiple_of(kc * tk, tk)
        keyc = key_ref[:, pl.ds(k0, tk)]
        thr_t = _rep(thr_ref[...], reps)
        eq = jnp.where(keyc == thr_t, 1.0, 0.0)
        rank = _dot(eq.astype(BF16), tri) + _rep(ties_before, reps)
        take = jnp.where(keyc > thr_t, 1.0, jnp.where(rank < _rep(nties_ref[...], reps), eq, 0.0))
        mb_ref[...] = jnp.where(col0 + k0 <= row, jnp.where(take > 0.5, 0.0, NEG_BIG), NEG_BIG)
        mb = mb_ref[...]
        m_olds, m_news = [], []
        for hd in range(DSA_HEADS):
            s = _nt(q_ref[0, :, hd * LANES:(hd + 1) * LANES],
                    k_ref[0, pl.ds(k0, tk), hd * LANES:(hd + 1) * LANES]) + mb
            s_ref[hd] = s
            m_old = m_ref[hd]
            m_olds.append(m_old)
            m_news.append(jnp.maximum(m_old, jnp.max(s, axis=-1, keepdims=True)))
        for j in range(DSA_HEADS // 2):
            vp = v_ref[0, pl.ds(k0, tk), j * LANES:(j + 1) * LANES]
            vlane = lax.broadcasted_iota(I32, vp.shape, 1)
            for a in range(2):
                hd = 2 * j + a
                keep = (vlane < HEAD) if a == 0 else (vlane >= HEAD)
                va = jnp.where(vlane == _SUM_LANE[a], jnp.ones_like(vp), jnp.where(keep, vp, jnp.zeros_like(vp)))
                p = jnp.exp2((s_ref[hd] - _rep(m_news[hd], reps)).astype(BF16))
                acc_ref[hd] = jnp.exp2(m_olds[hd] - m_news[hd]) * acc_ref[hd] + _dot(p, va)
                m_ref[hd] = m_news[hd]
        psum = eq[:, 0:LANES]
        for g in range(1, reps):
            psum = psum + eq[:, g * LANES:(g + 1) * LANES]
        return ties_before + jnp.sum(psum, axis=-1, keepdims=True)

    lax.fori_loop(0, nkc, attend_chunk, jnp.zeros((tq, LANES), F32))

    lane = lax.broadcasted_iota(I32, (tq, LANES), 1)
    for j in range(DSA_HEADS // 2):
        oa, ob = acc_ref[2 * j], acc_ref[2 * j + 1]
        la = jnp.sum(jnp.where(lane == _SUM_LANE[0], oa, 0.0), axis=-1, keepdims=True)
        lb = jnp.sum(jnp.where(lane == _SUM_LANE[1], ob, 0.0), axis=-1, keepdims=True)
        o_ref[0, :, j * LANES:(j + 1) * LANES] = jnp.where(low, oa / la, ob / lb).astype(BF16)


def _dsa_attention(iq, iw, ikl, ikh, sq, sk, sv, tq, tk, rs, topk):
    B, S, W = sq.shape
    V = sv.shape[2]
    nq = S // tq
    qblk = lambda b, i: (b, i, 0)
    full = lambda b, i: (b, 0, 0)
    once = pl.Buffered(1)
    return pl.pallas_call(
        functools.partial(_dsa_kernel, tq=tq, tk=tk, rs=rs, topk=topk),
        grid=(B, nq),
        in_specs=[pl.BlockSpec((1, tq, iq.shape[2]), qblk),
                  pl.BlockSpec((1, tq, LANES), qblk),
                  pl.BlockSpec((1, S, LANES), full, pipeline_mode=once),
                  pl.BlockSpec((1, S, LANES), full, pipeline_mode=once),
                  pl.BlockSpec((1, tq, W), qblk),
                  pl.BlockSpec((1, S, W), full, pipeline_mode=once),
                  pl.BlockSpec((1, S, V), full, pipeline_mode=once)],
        out_specs=pl.BlockSpec((1, tq, V), qblk),
        out_shape=jax.ShapeDtypeStruct((B, S, V), BF16),
        scratch_shapes=[pltpu.VMEM((tq, S), I32),
                        pltpu.VMEM((tq, S), I16),
                        pltpu.VMEM((tq, LANES), I32),
                        pltpu.VMEM((tq, LANES), F32),
                        pltpu.VMEM((tq, tk), F32),
                        pltpu.VMEM((DSA_HEADS, tq, tk), F32),
                        pltpu.VMEM((DSA_HEADS, tq, LANES), F32),
                        pltpu.VMEM((DSA_HEADS, tq, LANES), F32)],
        compiler_params=_params(("parallel", "arbitrary")),
        name="dsa_attention",
    )(iq, iw, ikl, ikh, sq, sk, sv)


def _mix_kernel(do_ref, so_ref, x_ref, mod_ref, g2_ref, wo1_ref, wo2_ref, wsg_ref, wsu_ref, wsd_ref,
                rwt_ref, base_ref, h2_ref, lg_ref):
    mix = _dot(do_ref[0], wo1_ref[...]) + _dot(so_ref[0], wo2_ref[...])
    mod = mod_ref[0]
    x1 = x_ref[0] + mod[2:3] * mix
    ms = jnp.mean(x1 * x1, axis=-1, keepdims=True)
    h2 = x1 * lax.rsqrt(ms + EPS) * g2_ref[...] * (1.0 + mod[4:5]) + mod[3:4]
    hb = h2.astype(BF16)
    gate = _dot(hb, wsg_ref[...])
    up = _dot(hb, wsu_ref[...])
    act = gate / (1.0 + jnp.exp(-gate)) * up
    shared = _dot(act.astype(BF16), wsd_ref[...])
    base_ref[0] = x1 + mod[5:6] * shared
    lg_ref[...] = _nt(rwt_ref[...], hb)
    hf = hb.astype(F32)
    tm = hf.shape[0]
    bits = lax.bitcast_convert_type(hf, I32)
    half = D_MODEL // 2
    packed = (bits[:, half:] & -65536) | lax.shift_right_logical(bits[:, :half], 16)
    for j in range(PACK_TILES):
        h2_ref[pl.ds(j, tm, stride=PACK_TILES), :] = packed[:, j * LANES:(j + 1) * LANES]


def _mix(diff_out, dsa_out, x, mod3, g2, wo1, wo2, wsg, wsu, wsd, rwt, tm):
    B, S, D = x.shape
    ns = S // tm
    T = B * S
    tok = lambda b, i: (b, i, 0)
    c2 = lambda b, i: (0, 0)
    return pl.pallas_call(
        _mix_kernel,
        grid=(B, ns),
        in_specs=[pl.BlockSpec((1, tm, SEG), tok), pl.BlockSpec((1, tm, SEG), tok),
                  pl.BlockSpec((1, tm, D), tok),
                  pl.BlockSpec((1, 6, D), lambda b, i: (b, 0, 0)),
                  pl.BlockSpec((1, D), c2),
                  pl.BlockSpec(wo1.shape, c2), pl.BlockSpec(wo2.shape, c2),
                  pl.BlockSpec(wsg.shape, c2), pl.BlockSpec(wsu.shape, c2), pl.BlockSpec(wsd.shape, c2),
                  pl.BlockSpec(rwt.shape, c2)],
        out_specs=[pl.BlockSpec((1, tm, D), tok),
                   pl.BlockSpec((tm * PACK_TILES, LANES), lambda b, i: (b * ns + i, 0)),
                   pl.BlockSpec((N_EXPERTS, tm), lambda b, i: (0, b * ns + i))],
        out_shape=[jax.ShapeDtypeStruct((B, S, D), F32),
                   jax.ShapeDtypeStruct((T * PACK_TILES, LANES), I32),
                   jax.ShapeDtypeStruct((N_EXPERTS, T), F32)],
        compiler_params=_params(("parallel", "parallel")),
        name="mix_shared_router",
    )(diff_out, dsa_out, x, mod3, g2, wo1, wo2, wsg, wsu, wsd, rwt)


def _first_max(v, idx, sentinel):
    m = jnp.max(v, axis=0, keepdims=True)
    i = jnp.min(jnp.where(v == m, idx, sentinel), axis=0, keepdims=True)
    return m, i


def _route_kernel(lg_ref, bias_ref, eidx_ref, gate_ref):
    lg = lg_ref[...]
    tt = lg.shape[1]
    scores = 1.0 / (1.0 + jnp.exp(-lg))
    biased = scores + bias_ref[...]
    gi = lax.broadcasted_iota(I32, (GROUP_SIZE, tt), 0).astype(F32)
    gscore = []
    for g in range(N_GROUPS):
        blk = biased[g * GROUP_SIZE:(g + 1) * GROUP_SIZE, :]
        m1, i1 = _first_max(blk, gi, float(GROUP_SIZE))
        m2 = jnp.max(jnp.where(gi == i1, -jnp.inf, blk), axis=0, keepdims=True)
        gscore.append(m1 + m2)
    gs = jnp.concatenate(gscore, axis=0)
    gidx = lax.broadcasted_iota(I32, (N_GROUPS, tt), 0).astype(F32)
    chosen = jnp.zeros((N_GROUPS, tt), F32)
    for _ in range(TOPK_GROUPS):
        _, ig = _first_max(gs, gidx, float(N_GROUPS))
        hit = gidx == ig
        chosen = jnp.where(hit, 1.0, chosen)
        gs = jnp.where(hit, -jnp.inf, gs)
    masked = jnp.concatenate(
        [jnp.where(chosen[g:g + 1, :] > 0.5, biased[g * GROUP_SIZE:(g + 1) * GROUP_SIZE, :], -jnp.inf)
         for g in range(N_GROUPS)], axis=0)
    ei = lax.broadcasted_iota(I32, (N_EXPERTS, tt), 0).astype(F32)
    ids, ws = [], []
    for _ in range(TOP_K):
        _, ie = _first_max(masked, ei, float(N_EXPERTS))
        hit = ei == ie
        ws.append(jnp.sum(jnp.where(hit, scores, 0.0), axis=0, keepdims=True))
        ids.append(ie)
        masked = jnp.where(hit, -jnp.inf, masked)
    w = jnp.concatenate(ws, axis=0)
    gate_ref[...] = w / jnp.sum(w, axis=0, keepdims=True) * ROUTED_SCALE
    eidx_ref[...] = jnp.concatenate(ids, axis=0).astype(I32)


def _route(logits_t, bias_col, tt):
    E, T = logits_t.shape
    return pl.pallas_call(
        _route_kernel,
        grid=(T // tt,),
        in_specs=[pl.BlockSpec((E, tt), lambda i: (0, i)),
                  pl.BlockSpec((E, 1), lambda i: (0, 0))],
        out_specs=[pl.BlockSpec((TOP_K, tt), lambda i: (0, i)),
                   pl.BlockSpec((TOP_K, tt), lambda i: (0, i))],
        out_shape=[jax.ShapeDtypeStruct((TOP_K, T), I32),
                   jax.ShapeDtypeStruct((TOP_K, T), F32)],
        compiler_params=_params(("parallel",)),
        name="route",
    )(logits_t, bias_col)


def _plan_kernel(eidx_ref, dest_ref, bexp_ref, nused_ref, cnt_col, cnt_row, slot_base, *, blk, nb_pad):
    ph = pl.program_id(0)
    i = pl.program_id(1)
    tt = eidx_ref.shape[1]
    eidx = eidx_ref[...]
    ei = lax.broadcasted_iota(I32, (N_EXPERTS, tt), 0)
    onehot = jnp.zeros((N_EXPERTS, tt), F32)
    for k in range(TOP_K):
        onehot = onehot + jnp.where(ei == eidx[k:k + 1, :], 1.0, 0.0)
    oh = onehot.astype(BF16)

    @pl.when((ph == 0) & (i == 0))
    def _():
        cnt_col[...] = jnp.zeros(cnt_col.shape, F32)
        cnt_row[...] = jnp.zeros(cnt_row.shape, F32)

    @pl.when(ph == 0)
    def _():
        cnt_col[...] += _dot(oh, jnp.ones((tt, LANES), BF16))
        cnt_row[...] += _nt(jnp.ones((8, tt), BF16), oh)

    @pl.when((ph == 1) & (i == 0))
    def _():
        inv = 1.0 / blk
        nb_col = jnp.floor((cnt_col[:, 0:1] + (blk - 1)) * inv)
        nb_row = jnp.floor((cnt_row[0:1, :] + (blk - 1)) * inv)
        r = lax.broadcasted_iota(I32, (N_EXPERTS, N_EXPERTS), 0)
        c = lax.broadcasted_iota(I32, (N_EXPERTS, N_EXPERTS), 1)
        bstart = jnp.sum(jnp.where(c < r, nb_row, 0.0), axis=-1, keepdims=True)
        bend = bstart + nb_col
        slot_base[...] = bstart * blk
        jb = lax.broadcasted_iota(I32, (N_EXPERTS, nb_pad), 1).astype(F32)
        be = jnp.sum(jnp.where(bend <= jb, 1.0, 0.0), axis=0, keepdims=True)
        bexp_ref[...] = jnp.minimum(be, N_EXPERTS - 1.0).astype(I32)
        nused_ref[...] = jnp.broadcast_to(jnp.sum(nb_row, axis=-1, keepdims=True), nused_ref.shape).astype(I32)

    @pl.when(ph == 1)
    def _():
        tri = jnp.where(lax.broadcasted_iota(I32, (tt, tt), 0) < lax.broadcasted_iota(I32, (tt, tt), 1),
                        1.0, 0.0).astype(BF16)
        slot = _dot(oh, tri) + slot_base[...]
        for k in range(TOP_K):
            dk = jnp.sum(jnp.where(ei == eidx[k:k + 1, :], slot, 0.0), axis=0, keepdims=True)
            dest_ref[k:k + 1, :] = dk.astype(I32)
        slot_base[...] += jnp.sum(onehot, axis=-1, keepdims=True)


def _plan(eidx, tt, blk, nb_pad):
    K, T = eidx.shape
    nt = T // tt
    return pl.pallas_call(
        functools.partial(_plan_kernel, blk=blk, nb_pad=nb_pad),
        grid=(2, nt),
        in_specs=[pl.BlockSpec((K, tt), lambda p, i: (0, i))],
        out_specs=[pl.BlockSpec((K, tt), lambda p, i: (0, i * p)),
                   pl.BlockSpec((1, nb_pad), lambda p, i: (0, 0)),
                   pl.BlockSpec((1, LANES), lambda p, i: (0, 0))],
        out_shape=[jax.ShapeDtypeStruct((K, T), I32),
                   jax.ShapeDtypeStruct((1, nb_pad), I32),
                   jax.ShapeDtypeStruct((1, LANES), I32)],
        scratch_shapes=[pltpu.VMEM((N_EXPERTS, LANES), F32),
                        pltpu.VMEM((8, N_EXPERTS), F32),
                        pltpu.VMEM((N_EXPERTS, 1), F32)],
        compiler_params=_params(("arbitrary", "arbitrary")),
        name="plan",
    )(eidx)


def _row(ref, r, tiles=ROW_TILES):
    return ref.at[pl.ds(pl.multiple_of(r * tiles, tiles), tiles), :]


def _dispatch_kernel(dest_ref, h_ref, xs_in_ref, xs_ref, sem):
    del xs_in_ref
    tt = h_ref.shape[0] // PACK_TILES

    def issue(t, c):
        for k in range(TOP_K):
            pltpu.make_async_copy(_row(h_ref, t, PACK_TILES), _row(xs_ref, dest_ref[k, t], PACK_TILES), sem).start()
        return c

    lax.fori_loop(0, tt, issue, 0)

    def drain(t, c):
        for k in range(TOP_K):
            pltpu.make_async_copy(_row(h_ref, 0, PACK_TILES), _row(xs_ref, 0, PACK_TILES), sem).wait()
        return c

    lax.fori_loop(0, tt, drain, 0)


def _dispatch(dest, h2rows, xs_init, tt):
    T = h2rows.shape[0] // PACK_TILES
    return pl.pallas_call(
        _dispatch_kernel,
        grid=(T // tt,),
        in_specs=[pl.BlockSpec((TOP_K, tt), lambda i: (0, i), memory_space=pltpu.SMEM),
                  pl.BlockSpec((tt * PACK_TILES, LANES), lambda i: (i, 0)),
                  pl.BlockSpec(memory_space=pl.ANY)],
        out_specs=pl.BlockSpec(memory_space=pl.ANY),
        out_shape=jax.ShapeDtypeStruct(xs_init.shape, xs_init.dtype),
        scratch_shapes=[pltpu.SemaphoreType.DMA(())],
        input_output_aliases={2: 0},
        compiler_params=_params(("arbitrary",), has_side_effects=True),
        name="dispatch",
    )(dest, h2rows, xs_init)


def _experts_kernel(bexp_ref, nused_ref, xs_ref, wg_ref, wu_ref, wd_ref, y_ref, xb_ref):
    del bexp_ref

    @pl.when(pl.program_id(0) < nused_ref[0])
    def _():
        blk = xb_ref.shape[0]
        half = D_MODEL // 2
        for j in range(PACK_TILES):
            w = xs_ref[pl.ds(j, blk, stride=PACK_TILES), :]
            cols = slice(j * LANES, (j + 1) * LANES)
            xb_ref[:, cols] = lax.bitcast_convert_type(w << 16, F32).astype(BF16)
            xb_ref[:, half + j * LANES:half + (j + 1) * LANES] = lax.bitcast_convert_type(w & -65536, F32).astype(BF16)
        xb = xb_ref[...]
        gate = _dot(xb, wg_ref[0])
        up = _dot(xb, wu_ref[0])
        act = gate / (1.0 + jnp.exp(-gate)) * up
        y = _dot(act.astype(BF16), wd_ref[0])
        for j in range(ROW_TILES):
            y_ref[pl.ds(j, blk, stride=ROW_TILES), :] = y[:, j * LANES:(j + 1) * LANES]


def _experts(bexp, nused, xs, wg, wu, wd, blk, n_blocks):
    live = lambda j, be, nu: jnp.minimum(j, nu[0] - 1)
    row_spec = pl.BlockSpec((blk * ROW_TILES, LANES), lambda j, be, nu: (live(j, be, nu), 0))
    xs_spec = pl.BlockSpec((blk * PACK_TILES, LANES), lambda j, be, nu: (live(j, be, nu), 0))
    wspec = lambda w: pl.BlockSpec((1,) + w.shape[1:], lambda j, be, nu: (be[live(j, be, nu)], 0, 0))
    return pl.pallas_call(
        _experts_kernel,
        grid_spec=pltpu.PrefetchScalarGridSpec(
            num_scalar_prefetch=2,
            grid=(n_blocks,),
            in_specs=[xs_spec, wspec(wg), wspec(wu), wspec(wd)],
            out_specs=row_spec,
            scratch_shapes=[pltpu.VMEM((blk, D_MODEL), BF16)]),
        out_shape=jax.ShapeDtypeStruct((n_blocks * blk * ROW_TILES, LANES), F32),
        compiler_params=_params(("arbitrary",)),
        name="experts",
    )(bexp, nused, xs, wg, wu, wd)


def _combine_kernel(dest_ref, gate_ref, base_ref, mod_ref, y_ref, o_ref, buf, sem):
    tt = base_ref.shape[1]

    def issue(t, c):
        for k in range(TOP_K):
            pltpu.make_async_copy(_row(y_ref, dest_ref[k, t]), _row(buf, k * tt + t), sem).start()
        return c

    lax.fori_loop(0, tt, issue, 0)

    def drain(t, c):
        for k in range(TOP_K):
            pltpu.make_async_copy(_row(y_ref, 0), _row(buf, 0), sem).wait()
        return c

    lax.fori_loop(0, tt, drain, 0)

    gates = gate_ref[...]
    g2 = mod_ref[0][5:6]
    for j in range(D_MODEL // LANES):
        acc = jnp.zeros((tt, LANES), F32)
        for k in range(TOP_K):
            acc = acc + gates[:, k:k + 1] * buf[pl.ds(k * tt * ROW_TILES + j, tt, stride=ROW_TILES), :]
        cols = slice(j * LANES, (j + 1) * LANES)
        o_ref[0, :, cols] = base_ref[0, :, cols] + g2[:, cols] * acc


def _combine(dest, gates_tk, base, mod3, y, tt):
    B, S, D = base.shape
    ns = S // tt
    return pl.pallas_call(
        _combine_kernel,
        grid=(B, ns),
        in_specs=[pl.BlockSpec((TOP_K, tt), lambda b, i: (0, b * ns + i), memory_space=pltpu.SMEM),
                  pl.BlockSpec((tt, TOP_K), lambda b, i: (b * ns + i, 0)),
                  pl.BlockSpec((1, tt, D), lambda b, i: (b, i, 0)),
                  pl.BlockSpec((1, 6, D), lambda b, i: (b, 0, 0)),
                  pl.BlockSpec(memory_space=pl.ANY)],
        out_specs=pl.BlockSpec((1, tt, D), lambda b, i: (b, i, 0)),
        out_shape=jax.ShapeDtypeStruct((B, S, D), F32),
        scratch_shapes=[pltpu.VMEM((TOP_K * tt * ROW_TILES, LANES), F32),
                        pltpu.SemaphoreType.DMA(())],
        compiler_params=_params(("arbitrary", "arbitrary")),
        name="combine",
    )(dest, gates_tk, base, mod3, y)


def _alibi_q_features(n_heads, maps_per_head):
    slopes = 2.0 ** (-8.0 * jnp.arange(1, n_heads + 1, dtype=F32) / n_heads)
    c = jnp.repeat(slopes, maps_per_head) * LOG2E * POS_RADIX
    pieces = []
    rest = c
    for _ in range(N_SPLIT):
        p = rest.astype(BF16).astype(F32)
        pieces.append(p)
        rest = rest - p
    hi = jnp.stack(pieces, axis=1)
    feat = jnp.concatenate([hi, hi / POS_RADIX], axis=1)
    return jnp.pad(feat, ((0, 0), (HEAD, LANES - HEAD - 2 * N_SPLIT)))


def kernel(x, c, ada_w, ada_b, norm1_g, norm2_g, w_in, diff_q_norm_g, diff_k_norm_g, lam_q1, lam_k1, lam_q2, lam_k2, diff_subln_g, dsa_q_norm_g, dsa_k_norm_g, idx_k_norm_g, w_out, router_w, router_bias, exp_w_gate, exp_w_up, exp_w_down, shared_w_gate, shared_w_up, shared_w_down):
    B, S, D = x.shape
    assert D == D_MODEL and ada_w.shape[0] == 1 and S <= POS_RADIX * 128
    T = B * S
    topk = min(DSA_TOPK, S // 4)
    tm = min(512, S)
    tq_diff = min(512, S)
    tq_dsa = min(256, S)
    tk_dsa = min(512, S)
    rs_dsa = min(128, tq_dsa)
    tt_route = min(512, T)
    tt_move = min(256, S)
    blk = 512
    n_blocks = (T * TOP_K) // blk + N_EXPERTS
    nb_pad = -(-n_blocks // LANES) * LANES

    n_main = 7 * SEG
    wm = w_in[0, :, :n_main].astype(BF16)
    wt = jnp.pad(w_in[0, :, n_main:], ((0, 0), (0, LANES - (IDX_DIM + IDX_HEADS)))).astype(BF16)
    tile8 = lambda g: jnp.tile(g[0], SEG // g.shape[1]).reshape(1, SEG)
    gik = jnp.pad(idx_k_norm_g[0], (0, LANES - IDX_DIM)).reshape(1, LANES)
    lamv = jnp.concatenate([lam_q1, lam_k1, lam_q2, lam_k2], axis=0)
    wo1 = w_out[0, :SEG].astype(BF16)
    wo2 = w_out[0, SEG:].astype(BF16)
    rwt = router_w[0].T.astype(BF16)
    wg = exp_w_gate[0].astype(BF16)
    wu = exp_w_up[0].astype(BF16)
    wd = exp_w_down[0].astype(BF16)

    mod3 = _ada(c, ada_w[0], ada_b[0]).reshape(B, 6, D)

    dq, dk, dv, sq, sk, sv, iq, ikl, ikh, iw = _inproj(
        x, mod3, norm1_g, wm, wt, tile8(diff_q_norm_g), tile8(diff_k_norm_g),
        tile8(dsa_q_norm_g), tile8(dsa_k_norm_g), gik,
        _alibi_q_features(DIFF_HEADS, 2), _alibi_q_features(DSA_HEADS, 1), tm)

    diff_out = _diff_attention(dq, dk, dv, lamv, diff_subln_g, tq_diff)
    dsa_out = _dsa_attention(iq, iw, ikl, ikh, sq, sk, sv, tq_dsa, tk_dsa, rs_dsa, topk)

    base, h2rows, logits_t = _mix(diff_out, dsa_out, x, mod3, norm2_g, wo1, wo2,
                                  shared_w_gate[0].astype(BF16), shared_w_up[0].astype(BF16),
                                  shared_w_down[0].astype(BF16), rwt, tm)

    eidx, gates = _route(logits_t, router_bias[0].reshape(N_EXPERTS, 1), tt_route)
    dest, bexp, nused = _plan(eidx, tt_route, blk, nb_pad)

    xs = _dispatch(dest, h2rows, jnp.zeros((n_blocks * blk * PACK_TILES, LANES), I32), tt_move)
    y = _experts(bexp.reshape(nb_pad), nused[0, :1], xs, wg, wu, wd, blk, n_blocks)
    return _combine(dest, gates.T, base, mod3, y, tt_move)
```

```python
import functools
import math

import jax
import jax.numpy as jnp
from jax import lax
from jax.experimental import pallas as pl
from jax.experimental.pallas import tpu as pltpu

F32 = jnp.float32
BF16 = jnp.bfloat16
I32 = jnp.int32
I16 = jnp.int16

D_MODEL = 1024
DIFF_HEADS = 4
DIFF_HEAD_DIM = 64
DSA_HEADS = 8
DSA_HEAD_DIM = 64
IDX_HEADS = 8
IDX_DIM = 64
DSA_TOPK = 256
N_EXPERTS = 256
TOP_K = 8
N_GROUPS = 8
GROUP_SIZE = N_EXPERTS // N_GROUPS
TOPK_GROUPS = 4
D_EXPERT = 256
D_SHARED = 256
ROUTED_SCALE = 2.5
EPS = 1e-6
LAM_INIT = 0.2

LANES = 128
PACK_TILES = D_MODEL // (2 * LANES)
SEG = 512
HEAD = 64
N_MAPS = SEG // HEAD
WIDE = N_MAPS * LANES
POS_RADIX = 64
N_SPLIT = 3
NEG_BIG = -1e30
INT_MIN = -2147483648
LOG2E = math.log2(math.e)
VMEM_LIMIT = 56 * 1024 * 1024

NT_DIMS = (((1,), (1,)), ((), ()))


def _nt(a, b):
    return lax.dot_general(a, b, NT_DIMS, preferred_element_type=F32)


def _dot(a, b):
    return jnp.dot(a, b, preferred_element_type=F32)


def _rep(x, reps):
    return jnp.concatenate([x] * reps, axis=1)


def _params(sem, vmem=VMEM_LIMIT, **kw):
    return pltpu.CompilerParams(dimension_semantics=sem, vmem_limit_bytes=vmem, **kw)


def _ada_kernel(c_ref, w_ref, b_ref, o_ref):
    c = c_ref[...]
    s = c / (1.0 + jnp.exp(-c))
    o_ref[...] = jnp.dot(s, w_ref[...], preferred_element_type=F32,
                         precision=lax.Precision.HIGHEST) + b_ref[...]


def _ada(c, w, b):
    B, D = c.shape
    N = w.shape[1]
    tn = D
    return pl.pallas_call(
        _ada_kernel,
        grid=(N // tn,),
        in_specs=[pl.BlockSpec((B, D), lambda j: (0, 0)),
                  pl.BlockSpec((D, tn), lambda j: (0, j)),
                  pl.BlockSpec((1, tn), lambda j: (0, j))],
        out_specs=pl.BlockSpec((B, tn), lambda j: (0, j)),
        out_shape=jax.ShapeDtypeStruct((B, N), F32),
        compiler_params=_params(("arbitrary",)),
        name="ada",
    )(c, w, b.reshape(1, N))


def _group_sumsq(z):
    n = z.shape[1]
    r = lax.broadcasted_iota(I32, (n, n), 0) // HEAD
    c = lax.broadcasted_iota(I32, (n, n), 1) // HEAD
    bd = jnp.where(r == c, 1.0, 0.0).astype(BF16)
    zz = z * z
    hi = zz.astype(BF16)
    lo = (zz - hi.astype(F32)).astype(BF16)
    return _dot(hi, bd) + _dot(lo, bd)


def _inproj_kernel(x_ref, mod_ref, g1_ref, wm_ref, wt_ref, gq_ref, gk_ref, gsq_ref, gsk_ref, gik_ref,
                   fdq_ref, fsq_ref,
                   dq_ref, dk_ref, dv_ref, sq_ref, sk_ref, sv_ref, iq_ref, ikl_ref, ikh_ref, iw_ref):
    x = x_ref[0]
    tm = x.shape[0]
    ms = jnp.mean(x * x, axis=-1, keepdims=True)
    y = x * lax.rsqrt(ms + EPS) * g1_ref[...]
    mod = mod_ref[0]
    h = y * (1.0 + mod[1:2]) + mod[0:1]
    hb = h.astype(BF16)

    lane = lax.broadcasted_iota(I32, (tm, LANES), 1)
    is_head = lane < HEAD
    kpos = pl.program_id(1) * tm + lax.broadcasted_iota(I32, (tm, LANES), 0)
    hi_digit = (kpos // POS_RADIX).astype(F32)
    lo_digit = (kpos % POS_RADIX).astype(F32)
    kfeat = jnp.where(lane < HEAD + N_SPLIT, hi_digit, jnp.where(lane < HEAD + 2 * N_SPLIT, lo_digit, 0.0))

    def plain(seg_idx, out_ref):
        out_ref[0] = _dot(hb, wm_ref[:, seg_idx * SEG:(seg_idx + 1) * SEG]).astype(BF16)

    def normed(seg_idx, g_ref, scale, feat_ref, out_ref):
        half = SEG // 2
        for i in range(2):
            lo = seg_idx * SEG + i * half
            z = _dot(hb, wm_ref[:, lo:lo + half])
            ss = _group_sumsq(z)
            zn = z * lax.rsqrt(ss * (1.0 / HEAD) + EPS) * (g_ref[:, i * half:(i + 1) * half] * scale)
            for g in range(half // LANES):
                zg = zn[:, g * LANES:(g + 1) * LANES]
                for odd in range(2):
                    idx = i * (half // HEAD) + 2 * g + odd
                    src = zg if odd == 0 else pltpu.roll(zg, HEAD, 1)
                    feat = kfeat if feat_ref is None else feat_ref[idx:idx + 1, :]
                    out_ref[0, :, idx * LANES:(idx + 1) * LANES] = jnp.where(is_head, src, feat).astype(BF16)

    normed(0, gq_ref, HEAD ** -0.5 * LOG2E, fdq_ref, dq_ref)
    normed(1, gk_ref, 1.0, None, dk_ref)
    plain(2, dv_ref)
    normed(3, gsq_ref, HEAD ** -0.5 * LOG2E, fsq_ref, sq_ref)
    normed(4, gsk_ref, 1.0, None, sk_ref)
    plain(5, sv_ref)
    plain(6, iq_ref)

    t = _dot(hb, wt_ref[...])
    ikraw = jnp.where(lane < IDX_DIM, t, 0.0)
    ss = jnp.sum(ikraw * ikraw, axis=-1, keepdims=True) * (1.0 / IDX_DIM)
    ikn = ikraw * lax.rsqrt(ss + EPS) * gik_ref[...]
    ikl_ref[0] = ikn.astype(BF16)
    ikh_ref[0] = pltpu.roll(ikn, IDX_DIM, 1).astype(BF16)
    iwraw = jnp.where((lane >= IDX_DIM) & (lane < IDX_DIM + IDX_HEADS), t, 0.0)
    iw_ref[0] = pltpu.roll(iwraw * (IDX_HEADS ** -0.5), LANES - IDX_DIM, 1) * (IDX_DIM ** -0.5)


def _inproj(x, mod3, g1, wm, wt, gq, gk, gsq, gsk, gik, fdq, fsq, tm):
    B, S, D = x.shape
    ns = S // tm
    tok = lambda b, i: (b, i, 0)
    const2 = lambda b, i: (0, 0)
    seg_spec = pl.BlockSpec((1, tm, SEG), tok)
    wide_spec = pl.BlockSpec((1, tm, WIDE), tok)
    lane_spec = pl.BlockSpec((1, tm, LANES), tok)
    seg_shape = jax.ShapeDtypeStruct((B, S, SEG), BF16)
    wide_shape = jax.ShapeDtypeStruct((B, S, WIDE), BF16)
    return pl.pallas_call(
        _inproj_kernel,
        grid=(B, ns),
        in_specs=[pl.BlockSpec((1, tm, D), tok),
                  pl.BlockSpec((1, 6, D), lambda b, i: (b, 0, 0)),
                  pl.BlockSpec((1, D), const2),
                  pl.BlockSpec(wm.shape, const2),
                  pl.BlockSpec(wt.shape, const2),
                  pl.BlockSpec((1, SEG), const2), pl.BlockSpec((1, SEG), const2),
                  pl.BlockSpec((1, SEG), const2), pl.BlockSpec((1, SEG), const2),
                  pl.BlockSpec((1, LANES), const2),
                  pl.BlockSpec((N_MAPS, LANES), const2), pl.BlockSpec((N_MAPS, LANES), const2)],
        out_specs=[wide_spec, wide_spec, seg_spec, wide_spec, wide_spec, seg_spec, seg_spec,
                   lane_spec, lane_spec, lane_spec],
        out_shape=[wide_shape, wide_shape, seg_shape, wide_shape, wide_shape, seg_shape, seg_shape,
                   jax.ShapeDtypeStruct((B, S, LANES), BF16), jax.ShapeDtypeStruct((B, S, LANES), BF16),
                   jax.ShapeDtypeStruct((B, S, LANES), F32)],
        compiler_params=_params(("parallel", "parallel")),
        name="inproj",
    )(x, mod3, g1, wm, wt, gq, gk, gsq, gsk, gik, fdq, fsq)


def _diff_kernel(q_ref, k_ref, v_ref, lam_ref, g_ref, o_ref, s_ref, m_ref, l_ref, acc_ref, *, tq):
    qi = pl.program_id(1)
    ki = pl.program_id(2)
    n_maps = 2 * DIFF_HEADS
    reps = tq // LANES

    @pl.when(ki == 0)
    def _():
        m_ref[...] = jnp.full(m_ref.shape, NEG_BIG, F32)
        l_ref[...] = jnp.zeros(l_ref.shape, F32)
        acc_ref[...] = jnp.zeros(acc_ref.shape, F32)

    def step(diag):
        if diag:
            row = lax.broadcasted_iota(I32, (tq, tq), 0)
            col = lax.broadcasted_iota(I32, (tq, tq), 1)
            causal_bias = jnp.where(col <= row, 0.0, NEG_BIG)
        m_olds, m_news = [], []
        for idx in range(n_maps):
            s = _nt(q_ref[0, :, idx * LANES:(idx + 1) * LANES], k_ref[0, :, idx * LANES:(idx + 1) * LANES])
            if diag:
                s = s + causal_bias
            s_ref[idx] = s
            m_old = m_ref[idx]
            m_olds.append(m_old)
            m_news.append(jnp.maximum(m_old, jnp.max(s, axis=-1, keepdims=True)))
        for idx in range(n_maps):
            h = idx // 2
            p = jnp.exp2(s_ref[idx] - _rep(m_news[idx], reps))
            alpha = jnp.exp2(m_olds[idx] - m_news[idx])
            psum = p[:, 0:LANES]
            for g in range(1, reps):
                psum = psum + p[:, g * LANES:(g + 1) * LANES]
            l_ref[idx] = alpha * l_ref[idx] + psum
            acc_ref[idx] = alpha * acc_ref[idx] + _dot(p.astype(BF16), v_ref[0, :, h * LANES:(h + 1) * LANES])
            m_ref[idx] = m_news[idx]

    @pl.when(ki < qi)
    def _():
        step(False)

    @pl.when(ki == qi)
    def _():
        step(True)
        lv = lam_ref[...]
        lam = (jnp.exp(jnp.sum(lv[0:1] * lv[1:2], axis=-1, keepdims=True))
               - jnp.exp(jnp.sum(lv[2:3] * lv[3:4], axis=-1, keepdims=True)) + LAM_INIT)
        for h in range(DIFF_HEADS):
            o1 = acc_ref[2 * h] / jnp.sum(l_ref[2 * h], axis=-1, keepdims=True)
            o2 = acc_ref[2 * h + 1] / jnp.sum(l_ref[2 * h + 1], axis=-1, keepdims=True)
            o = o1 - lam * o2
            ms = jnp.mean(o * o, axis=-1, keepdims=True)
            on = o * lax.rsqrt(ms + EPS) * g_ref[...]
            o_ref[0, :, h * LANES:(h + 1) * LANES] = (on * (1.0 - LAM_INIT)).astype(BF16)


def _diff_attention(dq, dk, dv, lamv, subln_g, tq):
    B, S, W = dq.shape
    V = dv.shape[2]
    nq = S // tq
    n_maps = 2 * DIFF_HEADS
    return pl.pallas_call(
        functools.partial(_diff_kernel, tq=tq),
        grid=(B, nq, nq),
        in_specs=[pl.BlockSpec((1, tq, W), lambda b, i, j: (b, i, 0)),
                  pl.BlockSpec((1, tq, W), lambda b, i, j: (b, jnp.minimum(i, j), 0)),
                  pl.BlockSpec((1, tq, V), lambda b, i, j: (b, jnp.minimum(i, j), 0)),
                  pl.BlockSpec(lamv.shape, lambda b, i, j: (0, 0)),
                  pl.BlockSpec((1, LANES), lambda b, i, j: (0, 0))],
        out_specs=pl.BlockSpec((1, tq, V), lambda b, i, j: (b, i, 0)),
        out_shape=jax.ShapeDtypeStruct((B, S, V), BF16),
        scratch_shapes=[pltpu.VMEM((n_maps, tq, tq), F32),
                        pltpu.VMEM((n_maps, tq, LANES), F32),
                        pltpu.VMEM((n_maps, tq, LANES), F32),
                        pltpu.VMEM((n_maps, tq, LANES), F32)],
        compiler_params=_params(("parallel", "parallel", "arbitrary")),
        name="diff_attention",
    )(dq, dk, dv, lamv, subln_g)


def _score_key(v):
    bits = lax.bitcast_convert_type(v, I32)
    return bits ^ ((bits >> 31) & 0x7FFFFFFF)


_SUM_LANE = (LANES - 1, 0)
DIGIT_BITS = 16
DIGIT_BIAS = 1 << (DIGIT_BITS - 1)


def _dsa_kernel(iq_ref, iw_ref, ikl_ref, ikh_ref, q_ref, k_ref, v_ref, o_ref,
                key_ref, dig_ref, thr_ref, nties_ref, mb_ref, s_ref, m_ref, acc_ref, *, tq, tk, rs, topk):
    qi = pl.program_id(1)
    q0 = qi * tq
    nkc = (q0 + tq + tk - 1) // tk
    row = q0 + lax.broadcasted_iota(I32, (tq, tk), 0)
    col0 = lax.broadcasted_iota(I32, (tq, tk), 1)
    low = lax.broadcasted_iota(I32, (tq, LANES), 1) < HEAD
    reps = tk // LANES
    iw = iw_ref[0]

    def score_chunk(kc, carry):
        k0 = pl.multiple_of(kc * tk, tk)
        ikl = ikl_ref[0, pl.ds(k0, tk), :]
        ikh = ikh_ref[0, pl.ds(k0, tk), :]
        sc = jnp.zeros((tq, tk), F32)
        for j in range(IDX_HEADS // 2):
            iqp = iq_ref[0, :, j * LANES:(j + 1) * LANES]
            sc = sc + iw[:, 2 * j:2 * j + 1] * jnp.maximum(_nt(iqp, ikl), 0.0)
            sc = sc + iw[:, 2 * j + 1:2 * j + 2] * jnp.maximum(_nt(iqp, ikh), 0.0)
        sc = jnp.where(sc == 0.0, 0.0, sc)
        sc = jnp.where(col0 + k0 <= row, sc, -jnp.inf)
        key = _score_key(sc)
        key_ref[:, pl.ds(k0, tk)] = key
        dig_ref[:, pl.ds(k0, tk)] = (key >> DIGIT_BITS).astype(I16)
        return carry

    lax.fori_loop(0, nkc, score_chunk, 0)

    groups = [slice(r0, r0 + rs) for r0 in range(0, tq, rs)]
    kf = float(topk)

    def count(thrs, strict):
        accs = []
        for rows, thr in zip(groups, thrs):
            thr_t = _rep(thr.astype(I16), reps)

            def body(kc, acc, rows=rows, thr_t=thr_t):
                k0 = pl.multiple_of(kc * tk, tk)
                dig = dig_ref[rows, pl.ds(k0, tk)]
                hit = jnp.where((dig > thr_t) if strict else (dig >= thr_t), jnp.int16(1), jnp.int16(0))
                for g in range(reps):
                    acc = acc + hit[:, g * LANES:(g + 1) * LANES]
                return acc

            accs.append(lax.fori_loop(0, nkc, body, jnp.zeros((rs, LANES), I16)))
        return [jnp.sum(acc.astype(F32), axis=-1, keepdims=True) for acc in accs]

    def digit_search(wanted):
        def bit_step(i, tus):
            bit = jnp.left_shift(jnp.int32(1), DIGIT_BITS - 1 - i)
            cands = [tu | bit for tu in tus]
            cnts = count([c - DIGIT_BIAS for c in cands], False)
            return tuple(jnp.where(cnt >= w, c, tu) for cnt, w, c, tu in zip(cnts, wanted, cands, tus))

        tus = lax.fori_loop(0, DIGIT_BITS, bit_step, tuple(jnp.zeros((rs, LANES), I32) for _ in groups))
        return [tu - DIGIT_BIAS for tu in tus]

    t_hi = digit_search([kf] * len(groups))
    above = count(t_hi, True)
    for rows, th in zip(groups, t_hi):
        th_t = _rep(th, reps)

        def low_digits(kc, carry, rows=rows, th_t=th_t):
            k0 = pl.multiple_of(kc * tk, tk)
            key = key_ref[rows, pl.ds(k0, tk)]
            lo = (key & (2 * DIGIT_BIAS - 1)) - DIGIT_BIAS
            dig_ref[rows, pl.ds(k0, tk)] = jnp.where((key >> DIGIT_BITS) == th_t, lo, -DIGIT_BIAS).astype(I16)
            return carry

        lax.fori_loop(0, nkc, low_digits, 0)
    t_lo = digit_search([kf - a for a in above])
    inside = count(t_lo, True)
    for rows, th, tl, a, b in zip(groups, t_hi, t_lo, above, inside):
        thr_ref[rows, :] = th * (2 * DIGIT_BIAS) + (tl + DIGIT_BIAS)
        nties_ref[rows, :] = jnp.broadcast_to(kf - a - b, (rs, LANES))

    m_ref[...] = jnp.full(m_ref.shape, NEG_BIG, F32)
    acc_ref[...] = jnp.zeros(acc_ref.shape, F32)
    tri =jnp.where(lax.broadcasted_iota(I32, (tk, tk), 0) < lax.broadcasted_iota(I32, (tk, tk), 1),
                    1.0, 0.0).astype(BF16)

    def attend_chunk(kc, ties_before):
        k0 = pl.multiple_of(kc * tk, tk)
        keyc = key_ref[:, pl.ds(k0, tk)]
        thr_t = _rep(thr_ref[...], reps)
        eq = jnp.where(keyc == thr_t, 1.0, 0.0)
        rank = _dot(eq.astype(BF16), tri) + _rep(ties_before, reps)
        take = jnp.where(keyc > thr_t, 1.0, jnp.where(rank < _rep(nties_ref[...], reps), eq, 0.0))
        mb_ref[...] = jnp.where(col0 + k0 <= row, jnp.where(take > 0.5, 0.0, NEG_BIG), NEG_BIG)
        mb = mb_ref[...]
        m_olds, m_news = [], []
        for hd in range(DSA_HEADS):
            s = _nt(q_ref[0, :, hd * LANES:(hd + 1) * LANES],
                    k_ref[0, pl.ds(k0, tk), hd * LANES:(hd + 1) * LANES]) + mb
            s_ref[hd] = s
            m_old = m_ref[hd]
            m_olds.append(m_old)
            m_news.append(jnp.maximum(m_old, jnp.max(s, axis=-1, keepdims=True)))
        for j in range(DSA_HEADS // 2):
            vp = v_ref[0, pl.ds(k0, tk), j * LANES:(j + 1) * LANES]
            vlane = lax.broadcasted_iota(I32, vp.shape, 1)
            for a in range(2):
                hd = 2 * j + a
                keep = (vlane < HEAD) if a == 0 else (vlane >= HEAD)
                va = jnp.where(vlane == _SUM_LANE[a], jnp.ones_like(vp), jnp.where(keep, vp, jnp.zeros_like(vp)))
                p = jnp.exp2((s_ref[hd] - _rep(m_news[hd], reps)).astype(BF16))
                acc_ref[hd] = jnp.exp2(m_olds[hd] - m_news[hd]) * acc_ref[hd] + _dot(p, va)
                m_ref[hd] = m_news[hd]
        psum = eq[:, 0:LANES]
        for g in range(1, reps):
            psum = psum + eq[:, g * LANES:(g + 1) * LANES]
        return ties_before + jnp.sum(psum, axis=-1, keepdims=True)

    lax.fori_loop(0, nkc, attend_chunk, jnp.zeros((tq, LANES), F32))

    lane = lax.broadcasted_iota(I32, (tq, LANES), 1)
    for j in range(DSA_HEADS // 2):
        oa, ob = acc_ref[2 * j], acc_ref[2 * j + 1]
        la = jnp.sum(jnp.where(lane == _SUM_LANE[0], oa, 0.0), axis=-1, keepdims=True)
        lb = jnp.sum(jnp.where(lane == _SUM_LANE[1], ob, 0.0), axis=-1, keepdims=True)
        o_ref[0, :, j * LANES:(j + 1) * LANES] = jnp.where(low, oa / la, ob / lb).astype(BF16)


def _dsa_attention(iq, iw, ikl, ikh, sq, sk, sv, tq, tk, rs, topk):
    B, S, W = sq.shape
    V = sv.shape[2]
    nq = S // tq
    qblk = lambda b, i: (b, i, 0)
    full = lambda b, i: (b, 0, 0)
    once = pl.Buffered(1)
    return pl.pallas_call(
        functools.partial(_dsa_kernel, tq=tq, tk=tk, rs=rs, topk=topk),
        grid=(B, nq),
        in_specs=[pl.BlockSpec((1, tq, iq.shape[2]), qblk),
                  pl.BlockSpec((1, tq, LANES), qblk),
                  pl.BlockSpec((1, S, LANES), full, pipeline_mode=once),
                  pl.BlockSpec((1, S, LANES), full, pipeline_mode=once),
                  pl.BlockSpec((1, tq, W), qblk),
                  pl.BlockSpec((1, S, W), full, pipeline_mode=once),
                  pl.BlockSpec((1, S, V), full, pipeline_mode=once)],
        out_specs=pl.BlockSpec((1, tq, V), qblk),
        out_shape=jax.ShapeDtypeStruct((B, S, V), BF16),
        scratch_shapes=[pltpu.VMEM((tq, S), I32),
                        pltpu.VMEM((tq, S), I16),
                        pltpu.VMEM((tq, LANES), I32),
                        pltpu.VMEM((tq, LANES), F32),
                        pltpu.VMEM((tq, tk), F32),
                        pltpu.VMEM((DSA_HEADS, tq, tk), F32),
                        pltpu.VMEM((DSA_HEADS, tq, LANES), F32),
                        pltpu.VMEM((DSA_HEADS, tq, LANES), F32)],
        compiler_params=_params(("parallel", "arbitrary")),
        name="dsa_attention",
    )(iq, iw, ikl, ikh, sq, sk, sv)


def _mix_kernel(do_ref, so_ref, x_ref, mod_ref, g2_ref, wo1_ref, wo2_ref, wsg_ref, wsu_ref, wsd_ref,
                rwt_ref, base_ref, h2_ref, lg_ref):
    mix = _dot(do_ref[0], wo1_ref[...]) + _dot(so_ref[0], wo2_ref[...])
    mod = mod_ref[0]
    x1 = x_ref[0] + mod[2:3] * mix
    ms = jnp.mean(x1 * x1, axis=-1, keepdims=True)
    h2 = x1 * lax.rsqrt(ms + EPS) * g2_ref[...] * (1.0 + mod[4:5]) + mod[3:4]
    hb = h2.astype(BF16)
    gate = _dot(hb, wsg_ref[...])
    up = _dot(hb, wsu_ref[...])
    act = gate / (1.0 + jnp.exp(-gate)) * up
    shared = _dot(act.astype(BF16), wsd_ref[...])
    base_ref[0] = x1 + mod[5:6] * shared
    lg_ref[...] = _nt(rwt_ref[...], hb)
    hf = hb.astype(F32)
    tm = hf.shape[0]
    bits = lax.bitcast_convert_type(hf, I32)
    half = D_MODEL // 2
    packed = (bits[:, half:] & -65536) | lax.shift_right_logical(bits[:, :half], 16)
    for j in range(PACK_TILES):
        h2_ref[pl.ds(j, tm, stride=PACK_TILES), :] = packed[:, j * LANES:(j + 1) * LANES]


def _mix(diff_out, dsa_out, x, mod3, g2, wo1, wo2, wsg, wsu, wsd, rwt, tm):
    B, S, D = x.shape
    ns = S // tm
    T = B * S
    tok = lambda b, i: (b, i, 0)
    c2 = lambda b, i: (0, 0)
    return pl.pallas_call(
        _mix_kernel,
        grid=(B, ns),
        in_specs=[pl.BlockSpec((1, tm, SEG), tok), pl.BlockSpec((1, tm, SEG), tok),
                  pl.BlockSpec((1, tm, D), tok),
                  pl.BlockSpec((1, 6, D), lambda b, i: (b, 0, 0)),
                  pl.BlockSpec((1, D), c2),
                  pl.BlockSpec(wo1.shape, c2), pl.BlockSpec(wo2.shape, c2),
                  pl.BlockSpec(wsg.shape, c2), pl.BlockSpec(wsu.shape, c2), pl.BlockSpec(wsd.shape, c2),
                  pl.BlockSpec(rwt.shape, c2)],
        out_specs=[pl.BlockSpec((1, tm, D), tok),
                   pl.BlockSpec((tm * PACK_TILES, LANES), lambda b, i: (b * ns + i, 0)),
                   pl.BlockSpec((N_EXPERTS, tm), lambda b, i: (0, b * ns + i))],
        out_shape=[jax.ShapeDtypeStruct((B, S, D), F32),
                   jax.ShapeDtypeStruct((T * PACK_TILES, LANES), I32),
                   jax.ShapeDtypeStruct((N_EXPERTS, T), F32)],
        compiler_params=_params(("parallel", "parallel")),
        name="mix_shared_router",
    )(diff_out, dsa_out, x, mod3, g2, wo1, wo2, wsg, wsu, wsd, rwt)


def _first_max(v, idx, sentinel):
    m = jnp.max(v, axis=0, keepdims=True)
    i = jnp.min(jnp.where(v == m, idx, sentinel), axis=0, keepdims=True)
    return m, i


def _route_kernel(lg_ref, bias_ref, eidx_ref, gate_ref):
    lg = lg_ref[...]
    tt = lg.shape[1]
    scores = 1.0 / (1.0 + jnp.exp(-lg))
    biased = scores + bias_ref[...]
    gi = lax.broadcasted_iota(I32, (GROUP_SIZE, tt), 0).astype(F32)
    gscore = []
    for g in range(N_GROUPS):
        blk = biased[g * GROUP_SIZE:(g + 1) * GROUP_SIZE, :]
        m1, i1 = _first_max(blk, gi, float(GROUP_SIZE))
        m2 = jnp.max(jnp.where(gi == i1, -jnp.inf, blk), axis=0, keepdims=True)
        gscore.append(m1 + m2)
    gs = jnp.concatenate(gscore, axis=0)
    gidx = lax.broadcasted_iota(I32, (N_GROUPS, tt), 0).astype(F32)
    chosen = jnp.zeros((N_GROUPS, tt), F32)
    for _ in range(TOPK_GROUPS):
        _, ig = _first_max(gs, gidx, float(N_GROUPS))
        hit = gidx == ig
        chosen = jnp.where(hit, 1.0, chosen)
        gs = jnp.where(hit, -jnp.inf, gs)
    masked = jnp.concatenate(
        [jnp.where(chosen[g:g + 1, :] > 0.5, biased[g * GROUP_SIZE:(g + 1) * GROUP_SIZE, :], -jnp.inf)
         for g in range(N_GROUPS)], axis=0)
    ei = lax.broadcasted_iota(I32, (N_EXPERTS, tt), 0).astype(F32)
    ids, ws = [], []
    for _ in range(TOP_K):
        _, ie = _first_max(masked, ei, float(N_EXPERTS))
        hit = ei == ie
        ws.append(jnp.sum(jnp.where(hit, scores, 0.0), axis=0, keepdims=True))
        ids.append(ie)
        masked = jnp.where(hit, -jnp.inf, masked)
    w = jnp.concatenate(ws, axis=0)
    gate_ref[...] = w / jnp.sum(w, axis=0, keepdims=True) * ROUTED_SCALE
    eidx_ref[...] = jnp.concatenate(ids, axis=0).astype(I32)


def _route(logits_t, bias_col, tt):
    E, T = logits_t.shape
    return pl.pallas_call(
        _route_kernel,
        grid=(T // tt,),
        in_specs=[pl.BlockSpec((E, tt), lambda i: (0, i)),
                  pl.BlockSpec((E, 1), lambda i: (0, 0))],
        out_specs=[pl.BlockSpec((TOP_K, tt), lambda i: (0, i)),
                   pl.BlockSpec((TOP_K, tt), lambda i: (0, i))],
        out_shape=[jax.ShapeDtypeStruct((TOP_K, T), I32),
                   jax.ShapeDtypeStruct((TOP_K, T), F32)],
        compiler_params=_params(("parallel",)),
        name="route",
    )(logits_t, bias_col)


def _plan_kernel(eidx_ref, dest_ref, bexp_ref, nused_ref, cnt_col, cnt_row, slot_base, *, blk, nb_pad):
    ph = pl.program_id(0)
    i = pl.program_id(1)
    tt = eidx_ref.shape[1]
    eidx = eidx_ref[...]
    ei = lax.broadcasted_iota(I32, (N_EXPERTS, tt), 0)
    onehot = jnp.zeros((N_EXPERTS, tt), F32)
    for k in range(TOP_K):
        onehot = onehot + jnp.where(ei == eidx[k:k + 1, :], 1.0, 0.0)
    oh = onehot.astype(BF16)

    @pl.when((ph == 0) & (i == 0))
    def _():
        cnt_col[...] = jnp.zeros(cnt_col.shape, F32)
        cnt_row[...] = jnp.zeros(cnt_row.shape, F32)

    @pl.when(ph == 0)
    def _():
        cnt_col[...] += _dot(oh, jnp.ones((tt, LANES), BF16))
        cnt_row[...] += _nt(jnp.ones((8, tt), BF16), oh)

    @pl.when((ph == 1) & (i == 0))
    def _():
        inv = 1.0 / blk
        nb_col = jnp.floor((cnt_col[:, 0:1] + (blk - 1)) * inv)
        nb_row = jnp.floor((cnt_row[0:1, :] + (blk - 1)) * inv)
        r = lax.broadcasted_iota(I32, (N_EXPERTS, N_EXPERTS), 0)
        c = lax.broadcasted_iota(I32, (N_EXPERTS, N_EXPERTS), 1)
        bstart = jnp.sum(jnp.where(c < r, nb_row, 0.0), axis=-1, keepdims=True)
        bend = bstart + nb_col
        slot_base[...] = bstart * blk
        jb = lax.broadcasted_iota(I32, (N_EXPERTS, nb_pad), 1).astype(F32)
        be = jnp.sum(jnp.where(bend <= jb, 1.0, 0.0), axis=0, keepdims=True)
        bexp_ref[...] = jnp.minimum(be, N_EXPERTS - 1.0).astype(I32)
        nused_ref[...] = jnp.broadcast_to(jnp.sum(nb_row, axis=-1, keepdims=True), nused_ref.shape).astype(I32)

    @pl.when(ph == 1)
    def _():
        tri = jnp.where(lax.broadcasted_iota(I32, (tt, tt), 0) < lax.broadcasted_iota(I32, (tt, tt), 1),
                        1.0, 0.0).astype(BF16)
        slot = _dot(oh, tri) + slot_base[...]
        for k in range(TOP_K):
            dk = jnp.sum(jnp.where(ei == eidx[k:k + 1, :], slot, 0.0), axis=0, keepdims=True)
            dest_ref[k:k + 1, :] = dk.astype(I32)
        slot_base[...] += jnp.sum(onehot, axis=-1, keepdims=True)


def _plan(eidx, tt, blk, nb_pad):
    K, T = eidx.shape
    nt = T // tt
    return pl.pallas_call(
        functools.partial(_plan_kernel, blk=blk, nb_pad=nb_pad),
        grid=(2, nt),
        in_specs=[pl.BlockSpec((K, tt), lambda p, i: (0, i))],
        out_specs=[pl.BlockSpec((K, tt), lambda p, i: (0, i * p)),
                   pl.BlockSpec((1, nb_pad), lambda p, i: (0, 0)),
                   pl.BlockSpec((1, LANES), lambda p, i: (0, 0))],
        out_shape=[jax.ShapeDtypeStruct((K, T), I32),
                   jax.ShapeDtypeStruct((1, nb_pad), I32),
                   jax.ShapeDtypeStruct((1, LANES), I32)],
        scratch_shapes=[pltpu.VMEM((N_EXPERTS, LANES), F32),
                        pltpu.VMEM((8, N_EXPERTS), F32),
                        pltpu.VMEM((N_EXPERTS, 1), F32)],
        compiler_params=_params(("arbitrary", "arbitrary")),
        name="plan",
    )(eidx)


def _row(ref, r):
    return ref.at[pl.ds(pl.multiple_of(r * PACK_TILES, PACK_TILES), PACK_TILES), :]


def _dispatch_kernel(dest_ref, h_ref, xs_in_ref, xs_ref, sem):
    del xs_in_ref
    tt = h_ref.shape[0] // PACK_TILES

    def issue(t, c):
        for k in range(TOP_K):
            pltpu.make_async_copy(_row(h_ref, t), _row(xs_ref, dest_ref[k, t]), sem).start()
        return c

    lax.fori_loop(0, tt, issue, 0)

    def drain(t, c):
        for k in range(TOP_K):
            pltpu.make_async_copy(_row(h_ref, 0), _row(xs_ref, 0), sem).wait()
        return c

    lax.fori_loop(0, tt, drain, 0)


def _dispatch(dest, h2rows, xs_init, tt):
    T = h2rows.shape[0] // PACK_TILES
    return pl.pallas_call(
        _dispatch_kernel,
        grid=(T // tt,),
        in_specs=[pl.BlockSpec((TOP_K, tt), lambda i: (0, i), memory_space=pltpu.SMEM),
                  pl.BlockSpec((tt * PACK_TILES, LANES), lambda i: (i, 0)),
                  pl.BlockSpec(memory_space=pl.ANY)],
        out_specs=pl.BlockSpec(memory_space=pl.ANY),
        out_shape=jax.ShapeDtypeStruct(xs_init.shape, xs_init.dtype),
        scratch_shapes=[pltpu.SemaphoreType.DMA(())],
        input_output_aliases={2: 0},
        compiler_params=_params(("arbitrary",), has_side_effects=True),
        name="dispatch",
    )(dest, h2rows, xs_init)


def _experts_kernel(bexp_ref, nused_ref, xs_ref, wg_ref, wu_ref, wd_ref, y_ref, xb_ref, wgb_ref, wub_ref, wdb_ref):
    j = pl.program_id(0)

    @pl.when(j < nused_ref[0])
    def _():
        @pl.when((j == 0) | (bexp_ref[j] != bexp_ref[jnp.maximum(j - 1, 0)]))
        def _():
            wgb_ref[...] = wg_ref[0].astype(BF16)
            wub_ref[...] = wu_ref[0].astype(BF16)
            wdb_ref[...] = wd_ref[0].astype(BF16)

        blk = xb_ref.shape[0]
        half = D_MODEL // 2
        for c in range(PACK_TILES):
            w = xs_ref[pl.ds(c, blk, stride=PACK_TILES), :]
            cols = slice(c * LANES, (c + 1) * LANES)
            xb_ref[:, cols] = lax.bitcast_convert_type(w << 16, F32).astype(BF16)
            xb_ref[:, half + c * LANES:half + (c + 1) * LANES] = lax.bitcast_convert_type(w & -65536, F32).astype(BF16)
        xb = xb_ref[...]
        gate = _dot(xb, wgb_ref[...])
        up = _dot(xb, wub_ref[...])
        act = gate / (1.0 + jnp.exp(-gate)) * up
        y = _dot(act.astype(BF16), wdb_ref[...])
        bits = lax.bitcast_convert_type(y.astype(BF16).astype(F32), I32)
        packed = (bits[:, half:] & -65536) | lax.shift_right_logical(bits[:, :half], 16)
        for c in range(PACK_TILES):
            y_ref[pl.ds(c, blk, stride=PACK_TILES), :] = packed[:, c * LANES:(c + 1) * LANES]


def _experts(bexp, nused, xs, wg, wu, wd, blk, n_blocks):
    live = lambda j, be, nu: jnp.minimum(j, nu[0] - 1)
    row_spec = pl.BlockSpec((blk * PACK_TILES, LANES), lambda j, be, nu: (live(j, be, nu), 0))
    wspec = lambda w: pl.BlockSpec((1,) + w.shape[1:], lambda j, be, nu: (be[live(j, be, nu)], 0, 0))
    return pl.pallas_call(
        _experts_kernel,
        grid_spec=pltpu.PrefetchScalarGridSpec(
            num_scalar_prefetch=2,
            grid=(n_blocks,),
            in_specs=[row_spec, wspec(wg), wspec(wu), wspec(wd)],
            out_specs=row_spec,
            scratch_shapes=[pltpu.VMEM((blk, D_MODEL), BF16),
                            pltpu.VMEM(wg.shape[1:], BF16), pltpu.VMEM(wu.shape[1:], BF16),
                            pltpu.VMEM(wd.shape[1:], BF16)]),
        out_shape=jax.ShapeDtypeStruct(xs.shape, I32),
        compiler_params=_params(("arbitrary",)),
        name="experts",
    )(bexp, nused, xs, wg, wu, wd)


def _combine_kernel(dest_ref, gate_ref, base_ref, mod_ref, y_ref, o_ref, buf, sem):
    tt = base_ref.shape[1]

    def issue(t, c):
        for k in range(TOP_K):
            pltpu.make_async_copy(_row(y_ref, dest_ref[k, t]), _row(buf, k * tt + t), sem).start()
        return c

    lax.fori_loop(0, tt, issue, 0)

    def drain(t, c):
        for k in range(TOP_K):
            pltpu.make_async_copy(_row(y_ref, 0), _row(buf, 0), sem).wait()
        return c

    lax.fori_loop(0, tt, drain, 0)

    gates = gate_ref[...]
    g2 = mod_ref[0][5:6]
    half = D_MODEL // 2
    for j in range(PACK_TILES):
        lo = jnp.zeros((tt, LANES), F32)
        hi = jnp.zeros((tt, LANES), F32)
        for k in range(TOP_K):
            w = buf[pl.ds(k * tt * PACK_TILES + j, tt, stride=PACK_TILES), :]
            lo = lo + gates[:, k:k + 1] * lax.bitcast_convert_type(w << 16, F32)
            hi = hi + gates[:, k:k + 1] * lax.bitcast_convert_type(w & -65536, F32)
        for off, acc in ((0, lo), (half, hi)):
            cols = slice(off + j * LANES, off + (j + 1) * LANES)
            o_ref[0, :, cols] = base_ref[0, :, cols] + g2[:, cols] * acc


def _combine(dest, gates_tk, base, mod3, y, tt):
    B, S, D = base.shape
    ns = S // tt
    return pl.pallas_call(
        _combine_kernel,
        grid=(B, ns),
        in_specs=[pl.BlockSpec((TOP_K, tt), lambda b, i: (0, b * ns + i), memory_space=pltpu.SMEM),
                  pl.BlockSpec((tt, TOP_K), lambda b, i: (b * ns + i, 0)),
                  pl.BlockSpec((1, tt, D), lambda b, i: (b, i, 0)),
                  pl.BlockSpec((1, 6, D), lambda b, i: (b, 0, 0)),
                  pl.BlockSpec(memory_space=pl.ANY)],
        out_specs=pl.BlockSpec((1, tt, D), lambda b, i: (b, i, 0)),
        out_shape=jax.ShapeDtypeStruct((B, S, D), F32),
        scratch_shapes=[pltpu.VMEM((TOP_K * tt * PACK_TILES, LANES), I32),
                        pltpu.SemaphoreType.DMA(())],
        compiler_params=_params(("arbitrary", "arbitrary")),
        name="combine",
    )(dest, gates_tk, base, mod3, y)


def _alibi_q_features(n_heads, maps_per_head):
    slopes = 2.0 ** (-8.0 * jnp.arange(1, n_heads + 1, dtype=F32) / n_heads)
    c = jnp.repeat(slopes, maps_per_head) * LOG2E * POS_RADIX
    pieces = []
    rest = c
    for _ in range(N_SPLIT):
        p = rest.astype(BF16).astype(F32)
        pieces.append(p)
        rest = rest - p
    hi = jnp.stack(pieces, axis=1)
    feat = jnp.concatenate([hi, hi / POS_RADIX], axis=1)
    return jnp.pad(feat, ((0, 0), (HEAD, LANES - HEAD - 2 * N_SPLIT)))


def kernel(x, c, ada_w, ada_b, norm1_g, norm2_g, w_in, diff_q_norm_g, diff_k_norm_g, lam_q1, lam_k1, lam_q2, lam_k2, diff_subln_g, dsa_q_norm_g, dsa_k_norm_g, idx_k_norm_g, w_out, router_w, router_bias, exp_w_gate, exp_w_up, exp_w_down, shared_w_gate, shared_w_up, shared_w_down):
    B, S, D = x.shape
    assert D == D_MODEL and ada_w.shape[0] == 1 and S <= POS_RADIX * 128
    T = B * S
    topk = min(DSA_TOPK, S // 4)
    tm = min(512, S)
    tq_diff = min(512, S)
    tq_dsa = min(256, S)
    tk_dsa = min(512, S)
    rs_dsa = min(128, tq_dsa)
    tt_route = min(512, T)
    tt_move = min(256, S)
    blk = 512
    n_blocks = (T * TOP_K) // blk + N_EXPERTS
    nb_pad = -(-n_blocks // LANES) * LANES

    n_main = 7 * SEG
    wm = w_in[0, :, :n_main].astype(BF16)
    wt = jnp.pad(w_in[0, :, n_main:], ((0, 0), (0, LANES - (IDX_DIM + IDX_HEADS)))).astype(BF16)
    tile8 = lambda g: jnp.tile(g[0], SEG // g.shape[1]).reshape(1, SEG)
    gik = jnp.pad(idx_k_norm_g[0], (0, LANES - IDX_DIM)).reshape(1, LANES)
    lamv = jnp.concatenate([lam_q1, lam_k1, lam_q2, lam_k2], axis=0)
    wo1 = w_out[0, :SEG].astype(BF16)
    wo2 = w_out[0, SEG:].astype(BF16)
    rwt = router_w[0].T.astype(BF16)
    wg, wu, wd = exp_w_gate[0], exp_w_up[0], exp_w_down[0]

    mod3 = _ada(c, ada_w[0], ada_b[0]).reshape(B, 6, D)

    dq, dk, dv, sq, sk, sv, iq, ikl, ikh, iw = _inproj(
        x, mod3, norm1_g, wm, wt, tile8(diff_q_norm_g), tile8(diff_k_norm_g),
        tile8(dsa_q_norm_g), tile8(dsa_k_norm_g), gik,
        _alibi_q_features(DIFF_HEADS, 2), _alibi_q_features(DSA_HEADS, 1), tm)

    diff_out = _diff_attention(dq, dk, dv, lamv, diff_subln_g, tq_diff)
    dsa_out = _dsa_attention(iq, iw, ikl, ikh, sq, sk, sv, tq_dsa, tk_dsa, rs_dsa, topk)

    base, h2rows, logits_t = _mix(diff_out, dsa_out, x, mod3, norm2_g, wo1, wo2,
                                  shared_w_gate[0].astype(BF16), shared_w_up[0].astype(BF16),
                                  shared_w_down[0].astype(BF16), rwt, tm)

    eidx, gates = _route(logits_t, router_bias[0].reshape(N_EXPERTS, 1), tt_route)
    dest, bexp, nused = _plan(eidx, tt_route, blk, nb_pad)

    xs = _dispatch(dest, h2rows, jnp.zeros((n_blocks * blk * PACK_TILES, LANES), I32), tt_move)
    y = _experts(bexp.reshape(nb_pad), nused[0, :1], xs, wg, wu, wd, blk, n_blocks)
    return _combine(dest, gates.T, base, mod3, y, tt_move)
```

```python
import functools
import math

import jax
import jax.numpy as jnp
from jax import lax
from jax.experimental import pallas as pl
from jax.experimental.pallas import tpu as pltpu

F32 = jnp.float32
BF16 = jnp.bfloat16
I32 = jnp.int32

D_MODEL = 1024
DIFF_HEADS = 4
DIFF_HEAD_DIM = 64
DSA_HEADS = 8
DSA_HEAD_DIM = 64
IDX_HEADS = 8
IDX_DIM = 64
DSA_TOPK = 256
N_EXPERTS = 256
TOP_K = 8
N_GROUPS = 8
GROUP_SIZE = N_EXPERTS // N_GROUPS
TOPK_GROUPS = 4
D_EXPERT = 256
D_SHARED = 256
ROUTED_SCALE = 2.5
EPS = 1e-6
LAM_INIT = 0.2

LANES = 128
PACK_TILES = D_MODEL // (2 * LANES)
SEG = 512
HEAD = 64
N_MAPS = SEG // HEAD
WIDE = N_MAPS * LANES
POS_RADIX = 64
N_SPLIT = 3
NEG_BIG = -1e30
INT_MIN = -2147483648
LOG2E = math.log2(math.e)
VMEM_LIMIT = 56 * 1024 * 1024

NT_DIMS = (((1,), (1,)), ((), ()))


def _nt(a, b):
    return lax.dot_general(a, b, NT_DIMS, preferred_element_type=F32)


def _dot(a, b):
    return jnp.dot(a, b, preferred_element_type=F32)


def _rep(x, reps):
    return jnp.concatenate([x] * reps, axis=1)


def _params(sem, vmem=VMEM_LIMIT, **kw):
    return pltpu.CompilerParams(dimension_semantics=sem, vmem_limit_bytes=vmem, **kw)


def _ada_kernel(c_ref, w_ref, b_ref, o_ref):
    c = c_ref[...]
    s = c / (1.0 + jnp.exp(-c))
    o_ref[...] = jnp.dot(s, w_ref[...], preferred_element_type=F32,
                         precision=lax.Precision.HIGHEST) + b_ref[...]


def _ada(c, w, b):
    B, D = c.shape
    N = w.shape[1]
    tn = D
    return pl.pallas_call(
        _ada_kernel,
        grid=(N // tn,),
        in_specs=[pl.BlockSpec((B, D), lambda j: (0, 0)),
                  pl.BlockSpec((D, tn), lambda j: (0, j)),
                  pl.BlockSpec((1, tn), lambda j: (0, j))],
        out_specs=pl.BlockSpec((B, tn), lambda j: (0, j)),
        out_shape=jax.ShapeDtypeStruct((B, N), F32),
        compiler_params=_params(("arbitrary",)),
        name="ada",
    )(c, w, b.reshape(1, N))


def _group_sumsq(z):
    n = z.shape[1]
    r = lax.broadcasted_iota(I32, (n, n), 0) // HEAD
    c = lax.broadcasted_iota(I32, (n, n), 1) // HEAD
    bd = jnp.where(r == c, 1.0, 0.0).astype(BF16)
    zz = z * z
    hi = zz.astype(BF16)
    lo = (zz - hi.astype(F32)).astype(BF16)
    return _dot(hi, bd) + _dot(lo, bd)


def _inproj_kernel(x_ref, mod_ref, g1_ref, wm_ref, wt_ref, gq_ref, gk_ref, gsq_ref, gsk_ref, gik_ref,
                   fdq_ref, fsq_ref,
                   dq_ref, dk_ref, dv_ref, sq_ref, sk_ref, sv_ref, iq_ref, ikl_ref, ikh_ref, iw_ref):
    x = x_ref[0]
    tm = x.shape[0]
    ms = jnp.mean(x * x, axis=-1, keepdims=True)
    y = x * lax.rsqrt(ms + EPS) * g1_ref[...]
    mod = mod_ref[0]
    h = y * (1.0 + mod[1:2]) + mod[0:1]
    hb = h.astype(BF16)

    lane = lax.broadcasted_iota(I32, (tm, LANES), 1)
    is_head = lane < HEAD
    kpos = pl.program_id(1) * tm + lax.broadcasted_iota(I32, (tm, LANES), 0)
    hi_digit = (kpos // POS_RADIX).astype(F32)
    lo_digit = (kpos % POS_RADIX).astype(F32)
    kfeat = jnp.where(lane < HEAD + N_SPLIT, hi_digit, jnp.where(lane < HEAD + 2 * N_SPLIT, lo_digit, 0.0))

    def plain(seg_idx, out_ref):
        out_ref[0] = _dot(hb, wm_ref[:, seg_idx * SEG:(seg_idx + 1) * SEG]).astype(BF16)

    def normed(seg_idx, g_ref, scale, feat_ref, out_ref):
        half = SEG // 2
        for i in range(2):
            lo = seg_idx * SEG + i * half
            z = _dot(hb, wm_ref[:, lo:lo + half])
            ss = _group_sumsq(z)
            zn = z * lax.rsqrt(ss * (1.0 / HEAD) + EPS) * (g_ref[:, i * half:(i + 1) * half] * scale)
            for g in range(half // LANES):
                zg = zn[:, g * LANES:(g + 1) * LANES]
                for odd in range(2):
                    idx = i * (half // HEAD) + 2 * g + odd
                    src = zg if odd == 0 else pltpu.roll(zg, HEAD, 1)
                    feat = kfeat if feat_ref is None else feat_ref[idx:idx + 1, :]
                    out_ref[0, :, idx * LANES:(idx + 1) * LANES] = jnp.where(is_head, src, feat).astype(BF16)

    normed(0, gq_ref, HEAD ** -0.5 * LOG2E, fdq_ref, dq_ref)
    normed(1, gk_ref, 1.0, None, dk_ref)
    plain(2, dv_ref)
    normed(3, gsq_ref, HEAD ** -0.5 * LOG2E, fsq_ref, sq_ref)
    normed(4, gsk_ref, 1.0, None, sk_ref)
    plain(5, sv_ref)
    plain(6, iq_ref)

    t = _dot(hb, wt_ref[...])
    ikraw = jnp.where(lane < IDX_DIM, t, 0.0)
    ss = jnp.sum(ikraw * ikraw, axis=-1, keepdims=True) * (1.0 / IDX_DIM)
    ikn = ikraw * lax.rsqrt(ss + EPS) * gik_ref[...]
    ikl_ref[0] = ikn.astype(BF16)
    ikh_ref[0] = pltpu.roll(ikn, IDX_DIM, 1).astype(BF16)
    iwraw = jnp.where((lane >= IDX_DIM) & (lane < IDX_DIM + IDX_HEADS), t, 0.0)
    iw_ref[0] = pltpu.roll(iwraw * (IDX_HEADS ** -0.5), LANES - IDX_DIM, 1) * (IDX_DIM ** -0.5)


def _inproj(x, mod3, g1, wm, wt, gq, gk, gsq, gsk, gik, fdq, fsq, tm):
    B, S, D = x.shape
    ns = S // tm
    tok = lambda b, i: (b, i, 0)
    const2 = lambda b, i: (0, 0)
    seg_spec = pl.BlockSpec((1, tm, SEG), tok)
    wide_spec = pl.BlockSpec((1, tm, WIDE), tok)
    lane_spec = pl.BlockSpec((1, tm, LANES), tok)
    seg_shape = jax.ShapeDtypeStruct((B, S, SEG), BF16)
    wide_shape = jax.ShapeDtypeStruct((B, S, WIDE), BF16)
    return pl.pallas_call(
        _inproj_kernel,
        grid=(B, ns),
        in_specs=[pl.BlockSpec((1, tm, D), tok),
                  pl.BlockSpec((1, 6, D), lambda b, i: (b, 0, 0)),
                  pl.BlockSpec((1, D), const2),
                  pl.BlockSpec(wm.shape, const2),
                  pl.BlockSpec(wt.shape, const2),
                  pl.BlockSpec((1, SEG), const2), pl.BlockSpec((1, SEG), const2),
                  pl.BlockSpec((1, SEG), const2), pl.BlockSpec((1, SEG), const2),
                  pl.BlockSpec((1, LANES), const2),
                  pl.BlockSpec((N_MAPS, LANES), const2), pl.BlockSpec((N_MAPS, LANES), const2)],
        out_specs=[wide_spec, wide_spec, seg_spec, wide_spec, wide_spec, seg_spec, seg_spec,
                   lane_spec, lane_spec, lane_spec],
        out_shape=[wide_shape, wide_shape, seg_shape, wide_shape, wide_shape, seg_shape, seg_shape,
                   jax.ShapeDtypeStruct((B, S, LANES), BF16), jax.ShapeDtypeStruct((B, S, LANES), BF16),
                   jax.ShapeDtypeStruct((B, S, LANES), F32)],
        compiler_params=_params(("parallel", "parallel")),
        name="inproj",
    )(x, mod3, g1, wm, wt, gq, gk, gsq, gsk, gik, fdq, fsq)


def _diff_kernel(q_ref, k_ref, v_ref, lam_ref, g_ref, o_ref, s_ref, m_ref, l_ref, acc_ref, *, tq):
    qi = pl.program_id(1)
    ki = pl.program_id(2)
    n_maps = 2 * DIFF_HEADS
    reps = tq // LANES

    @pl.when(ki == 0)
    def _():
        m_ref[...] = jnp.full(m_ref.shape, NEG_BIG, F32)
        l_ref[...] = jnp.zeros(l_ref.shape, F32)
        acc_ref[...] = jnp.zeros(acc_ref.shape, F32)

    def step(diag):
        if diag:
            row = lax.broadcasted_iota(I32, (tq, tq), 0)
            col = lax.broadcasted_iota(I32, (tq, tq), 1)
            causal_bias = jnp.where(col <= row, 0.0, NEG_BIG)
        m_olds, m_news = [], []
        for idx in range(n_maps):
            s = _nt(q_ref[0, :, idx * LANES:(idx + 1) * LANES], k_ref[0, :, idx * LANES:(idx + 1) * LANES])
            if diag:
                s = s + causal_bias
            s_ref[idx] = s
            m_old = m_ref[idx]
            m_olds.append(m_old)
            m_news.append(jnp.maximum(m_old, jnp.max(s, axis=-1, keepdims=True)))
        for idx in range(n_maps):
            h = idx // 2
            p = jnp.exp2(s_ref[idx] - _rep(m_news[idx], reps))
            alpha = jnp.exp2(m_olds[idx] - m_news[idx])
            psum = p[:, 0:LANES]
            for g in range(1, reps):
                psum = psum + p[:, g * LANES:(g + 1) * LANES]
            l_ref[idx] = alpha * l_ref[idx] + psum
            acc_ref[idx] = alpha * acc_ref[idx] + _dot(p.astype(BF16), v_ref[0, :, h * LANES:(h + 1) * LANES])
            m_ref[idx] = m_news[idx]

    @pl.when(ki < qi)
    def _():
        step(False)

    @pl.when(ki == qi)
    def _():
        step(True)
        lv = lam_ref[...]
        lam = (jnp.exp(jnp.sum(lv[0:1] * lv[1:2], axis=-1, keepdims=True))
               - jnp.exp(jnp.sum(lv[2:3] * lv[3:4], axis=-1, keepdims=True)) + LAM_INIT)
        for h in range(DIFF_HEADS):
            o1 = acc_ref[2 * h] / jnp.sum(l_ref[2 * h], axis=-1, keepdims=True)
            o2 = acc_ref[2 * h + 1] / jnp.sum(l_ref[2 * h + 1], axis=-1, keepdims=True)
            o = o1 - lam * o2
            ms = jnp.mean(o * o, axis=-1, keepdims=True)
            on = o * lax.rsqrt(ms + EPS) * g_ref[...]
            o_ref[0, :, h * LANES:(h + 1) * LANES] = (on * (1.0 - LAM_INIT)).astype(BF16)


def _diff_attention(dq, dk, dv, lamv, subln_g, tq):
    B, S, W = dq.shape
    V = dv.shape[2]
    nq = S // tq
    n_maps = 2 * DIFF_HEADS
    return pl.pallas_call(
        functools.partial(_diff_kernel, tq=tq),
        grid=(B, nq, nq),
        in_specs=[pl.BlockSpec((1, tq, W), lambda b, i, j: (b, i, 0)),
                  pl.BlockSpec((1, tq, W), lambda b, i, j: (b, jnp.minimum(i, j), 0)),
                  pl.BlockSpec((1, tq, V), lambda b, i, j: (b, jnp.minimum(i, j), 0)),
                  pl.BlockSpec(lamv.shape, lambda b, i, j: (0, 0)),
                  pl.BlockSpec((1, LANES), lambda b, i, j: (0, 0))],
        out_specs=pl.BlockSpec((1, tq, V), lambda b, i, j: (b, i, 0)),
        out_shape=jax.ShapeDtypeStruct((B, S, V), BF16),
        scratch_shapes=[pltpu.VMEM((n_maps, tq, tq), F32),
                        pltpu.VMEM((n_maps, tq, LANES), F32),
                        pltpu.VMEM((n_maps, tq, LANES), F32),
                        pltpu.VMEM((n_maps, tq, LANES), F32)],
        compiler_params=_params(("parallel", "parallel", "arbitrary")),
        name="diff_attention",
    )(dq, dk, dv, lamv, subln_g)


def _score_key(v):
    bits = lax.bitcast_convert_type(v, I32)
    return bits ^ ((bits >> 31) & 0x7FFFFFFF)


def _key_score(k):
    return lax.bitcast_convert_type(k ^ ((k >> 31) & 0x7FFFFFFF), F32)


_SUM_LANE = (LANES - 1, 0)
N_CAND = 12
CAND_ROWS = 16


def _dsa_kernel(iq_ref, iw_ref, ikl_ref, ikh_ref, q_ref, k_ref, v_ref, o_ref,
                key_ref, cand_ref, thr_ref, nties_ref, mb_ref, s_ref, m_ref, acc_ref, *, tq, tk, rs, topk):
    qi = pl.program_id(1)
    q0 = qi * tq
    nkc = (q0 + tq + tk - 1) // tk
    row = q0 + lax.broadcasted_iota(I32, (tq, tk), 0)
    col0 = lax.broadcasted_iota(I32, (tq, tk), 1)
    low = lax.broadcasted_iota(I32, (tq, LANES), 1) < HEAD
    reps = tk // LANES
    iw = iw_ref[0]

    def score_chunk(kc, carry):
        k0 = pl.multiple_of(kc * tk, tk)
        ikl = ikl_ref[0, pl.ds(k0, tk), :]
        ikh = ikh_ref[0, pl.ds(k0, tk), :]
        sc = jnp.zeros((tq, tk), F32)
        for j in range(IDX_HEADS // 2):
            iqp = iq_ref[0, :, j * LANES:(j + 1) * LANES]
            sc = sc + iw[:, 2 * j:2 * j + 1] * jnp.maximum(_nt(iqp, ikl), 0.0)
            sc = sc + iw[:, 2 * j + 1:2 * j + 2] * jnp.maximum(_nt(iqp, ikh), 0.0)
        sc = jnp.where(sc == 0.0, 0.0, sc)
        sc = jnp.where(col0 + k0 <= row, sc, -jnp.inf)
        key_ref[:, pl.ds(k0, tk)] = _score_key(sc)
        return carry

    lax.fori_loop(0, nkc, score_chunk, 0)

    groups = [slice(r0, r0 + rs) for r0 in range(0, tq, rs)]
    kf = float(topk)

    def candidates(g, carry):
        rows = pl.ds(pl.multiple_of(g * CAND_ROWS, CAND_ROWS), CAND_ROWS)

        def insert_chunk(kc, best):
            k0 = pl.multiple_of(kc * tk, tk)
            x = _key_score(key_ref[rows, pl.ds(k0, tk)])
            best = list(best)
            for lg in range(reps):
                v = x[:, lg * LANES:(lg + 1) * LANES]
                for i in range(N_CAND):
                    best[i], v = jnp.maximum(best[i], v), jnp.minimum(best[i], v)
            return tuple(best)

        lowest = jnp.full((CAND_ROWS, LANES), -jnp.inf, F32)
        best = lax.fori_loop(0, nkc, insert_chunk, (lowest,) * N_CAND)
        for i in range(N_CAND):
            cand_ref[rows, i * LANES:(i + 1) * LANES] = _score_key(best[i])
        return carry

    lax.fori_loop(0, tq // CAND_ROWS, candidates, 0)

    def count(ref, n_chunks, thrs, strict):
        accs = []
        for rows, thr in zip(groups, thrs):
            thr_t = _rep(thr, reps)

            def body(kc, acc, rows=rows, thr_t=thr_t):
                k0 = pl.multiple_of(kc * tk, tk)
                keyc = ref[rows, pl.ds(k0, tk)]
                hit = jnp.where((keyc > thr_t) if strict else (keyc >= thr_t), 1.0, 0.0)
                for g in range(reps):
                    acc = acc + hit[:, g * LANES:(g + 1) * LANES]
                return acc

            accs.append(lax.fori_loop(0, n_chunks, body, jnp.zeros((rs, LANES), F32)))
        return [jnp.broadcast_to(jnp.sum(acc, axis=-1, keepdims=True), (rs, LANES)) for acc in accs]

    def search(ref, n_chunks):
        def bit_step(i, tus):
            bit = jnp.left_shift(jnp.int32(1), 31 - i)
            cands = [tu | bit for tu in tus]
            cnts = count(ref, n_chunks, [c ^ INT_MIN for c in cands], False)
            return tuple(jnp.where(cnt >= kf, c, tu) for cnt, c, tu in zip(cnts, cands, tus))

        tus = lax.fori_loop(0, 32, bit_step, tuple(jnp.zeros((rs, LANES), I32) for _ in groups))
        return [tu ^ INT_MIN for tu in tus]

    def publish(thrs, n_above):
        for rows, thr, n_gt in zip(groups, thrs, n_above):
            thr_ref[rows, :] = thr
            nties_ref[rows, :] = kf - n_gt

    cand_chunks = N_CAND * LANES // tk
    thrs = search(cand_ref, cand_chunks)
    publish(thrs, count(cand_ref, cand_chunks, thrs, True))
    inside = count(cand_ref, cand_chunks, thrs, False)
    overall = count(key_ref, nkc, thrs, False)
    missed = [jnp.max(jnp.where(a != b, 1.0, 0.0)) for a, b in zip(inside, overall)]

    @pl.when(functools.reduce(jnp.maximum, missed) > 0.0)
    def _():
        full = search(key_ref, nkc)
        publish(full, count(key_ref, nkc, full, True))

    m_ref[...] = jnp.full(m_ref.shape, NEG_BIG, F32)
    acc_ref[...] = jnp.zeros(acc_ref.shape, F32)
    tri =jnp.where(lax.broadcasted_iota(I32, (tk, tk), 0) < lax.broadcasted_iota(I32, (tk, tk), 1),
                    1.0, 0.0).astype(BF16)

    def attend_chunk(kc, ties_before):
        k0 = pl.multiple_of(kc * tk, tk)
        keyc = key_ref[:, pl.ds(k0, tk)]
        thr_t = _rep(thr_ref[...], reps)
        eq = jnp.where(keyc == thr_t, 1.0, 0.0)
        rank = _dot(eq.astype(BF16), tri) + _rep(ties_before, reps)
        take = jnp.where(keyc > thr_t, 1.0, jnp.where(rank < _rep(nties_ref[...], reps), eq, 0.0))
        mb_ref[...] = jnp.where(col0 + k0 <= row, jnp.where(take > 0.5, 0.0, NEG_BIG), NEG_BIG)
        mb = mb_ref[...]
        m_olds, m_news = [], []
        for hd in range(DSA_HEADS):
            s = _nt(q_ref[0, :, hd * LANES:(hd + 1) * LANES],
                    k_ref[0, pl.ds(k0, tk), hd * LANES:(hd + 1) * LANES]) + mb
            s_ref[hd] = s
            m_old = m_ref[hd]
            m_olds.append(m_old)
            m_news.append(jnp.maximum(m_old, jnp.max(s, axis=-1, keepdims=True)))
        for j in range(DSA_HEADS // 2):
            vp = v_ref[0, pl.ds(k0, tk), j * LANES:(j + 1) * LANES]
            vlane = lax.broadcasted_iota(I32, vp.shape, 1)
            for a in range(2):
                hd = 2 * j + a
                keep = (vlane < HEAD) if a == 0 else (vlane >= HEAD)
                va = jnp.where(vlane == _SUM_LANE[a], jnp.ones_like(vp), jnp.where(keep, vp, jnp.zeros_like(vp)))
                p = jnp.exp2((s_ref[hd] - _rep(m_news[hd], reps)).astype(BF16))
                acc_ref[hd] = jnp.exp2(m_olds[hd] - m_news[hd]) * acc_ref[hd] + _dot(p, va)
                m_ref[hd] = m_news[hd]
        psum = eq[:, 0:LANES]
        for g in range(1, reps):
            psum = psum + eq[:, g * LANES:(g + 1) * LANES]
        return ties_before + jnp.sum(psum, axis=-1, keepdims=True)

    lax.fori_loop(0, nkc, attend_chunk, jnp.zeros((tq, LANES), F32))

    lane = lax.broadcasted_iota(I32, (tq, LANES), 1)
    for j in range(DSA_HEADS // 2):
        oa, ob = acc_ref[2 * j], acc_ref[2 * j + 1]
        la = jnp.sum(jnp.where(lane == _SUM_LANE[0], oa, 0.0), axis=-1, keepdims=True)
        lb = jnp.sum(jnp.where(lane == _SUM_LANE[1], ob, 0.0), axis=-1, keepdims=True)
        o_ref[0, :, j * LANES:(j + 1) * LANES] = jnp.where(low, oa / la, ob / lb).astype(BF16)


def _dsa_attention(iq, iw, ikl, ikh, sq, sk, sv, tq, tk, rs, topk):
    B, S, W = sq.shape
    V = sv.shape[2]
    nq = S // tq
    assert (N_CAND * LANES) % tk == 0 and tq % CAND_ROWS == 0 and tq % rs == 0
    qblk = lambda b, i: (b, i, 0)
    full = lambda b, i: (b, 0, 0)
    once = pl.Buffered(1)
    return pl.pallas_call(
        functools.partial(_dsa_kernel, tq=tq, tk=tk, rs=rs, topk=topk),
        grid=(B, nq),
        in_specs=[pl.BlockSpec((1, tq, iq.shape[2]), qblk),
                  pl.BlockSpec((1, tq, LANES), qblk),
                  pl.BlockSpec((1, S, LANES), full, pipeline_mode=once),
                  pl.BlockSpec((1, S, LANES), full, pipeline_mode=once),
                  pl.BlockSpec((1, tq, W), qblk),
                  pl.BlockSpec((1, S, W), full, pipeline_mode=once),
                  pl.BlockSpec((1, S, V), full, pipeline_mode=once)],
        out_specs=pl.BlockSpec((1, tq, V), qblk),
        out_shape=jax.ShapeDtypeStruct((B, S, V), BF16),
        scratch_shapes=[pltpu.VMEM((tq, S), I32),
                        pltpu.VMEM((tq, N_CAND * LANES), I32),
                        pltpu.VMEM((tq, LANES), I32),
                        pltpu.VMEM((tq, LANES), F32),
                        pltpu.VMEM((tq, tk), F32),
                        pltpu.VMEM((DSA_HEADS, tq, tk), F32),
                        pltpu.VMEM((DSA_HEADS, tq, LANES), F32),
                        pltpu.VMEM((DSA_HEADS, tq, LANES), F32)],
        compiler_params=_params(("parallel", "arbitrary")),
        name="dsa_attention",
    )(iq, iw, ikl, ikh, sq, sk, sv)


def _mix_kernel(do_ref, so_ref, x_ref, mod_ref, g2_ref, wo1_ref, wo2_ref, wsg_ref, wsu_ref, wsd_ref,
                rwt_ref, base_ref, h2_ref, lg_ref):
    mix = _dot(do_ref[0], wo1_ref[...]) + _dot(so_ref[0], wo2_ref[...])
    mod = mod_ref[0]
    x1 = x_ref[0] + mod[2:3] * mix
    ms = jnp.mean(x1 * x1, axis=-1, keepdims=True)
    h2 = x1 * lax.rsqrt(ms + EPS) * g2_ref[...] * (1.0 + mod[4:5]) + mod[3:4]
    hb = h2.astype(BF16)
    gate = _dot(hb, wsg_ref[...])
    up = _dot(hb, wsu_ref[...])
    act = gate / (1.0 + jnp.exp(-gate)) * up
    shared = _dot(act.astype(BF16), wsd_ref[...])
    base_ref[0] = x1 + mod[5:6] * shared
    lg_ref[...] = _nt(rwt_ref[...], hb)
    hf = hb.astype(F32)
    tm = hf.shape[0]
    bits = lax.bitcast_convert_type(hf, I32)
    half = D_MODEL // 2
    packed = (bits[:, half:] & -65536) | lax.shift_right_logical(bits[:, :half], 16)
    for j in range(PACK_TILES):
        h2_ref[pl.ds(j, tm, stride=PACK_TILES), :] = packed[:, j * LANES:(j + 1) * LANES]


def _mix(diff_out, dsa_out, x, mod3, g2, wo1, wo2, wsg, wsu, wsd, rwt, tm):
    B, S, D = x.shape
    ns = S // tm
    T = B * S
    tok = lambda b, i: (b, i, 0)
    c2 = lambda b, i: (0, 0)
    return pl.pallas_call(
        _mix_kernel,
        grid=(B, ns),
        in_specs=[pl.BlockSpec((1, tm, SEG), tok), pl.BlockSpec((1, tm, SEG), tok),
                  pl.BlockSpec((1, tm, D), tok),
                  pl.BlockSpec((1, 6, D), lambda b, i: (b, 0, 0)),
                  pl.BlockSpec((1, D), c2),
                  pl.BlockSpec(wo1.shape, c2), pl.BlockSpec(wo2.shape, c2),
                  pl.BlockSpec(wsg.shape, c2), pl.BlockSpec(wsu.shape, c2), pl.BlockSpec(wsd.shape, c2),
                  pl.BlockSpec(rwt.shape, c2)],
        out_specs=[pl.BlockSpec((1, tm, D), tok),
                   pl.BlockSpec((tm * PACK_TILES, LANES), lambda b, i: (b * ns + i, 0)),
                   pl.BlockSpec((N_EXPERTS, tm), lambda b, i: (0, b * ns + i))],
        out_shape=[jax.ShapeDtypeStruct((B, S, D), F32),
                   jax.ShapeDtypeStruct((T * PACK_TILES, LANES), I32),
                   jax.ShapeDtypeStruct((N_EXPERTS, T), F32)],
        compiler_params=_params(("parallel", "parallel")),
        name="mix_shared_router",
    )(diff_out, dsa_out, x, mod3, g2, wo1, wo2, wsg, wsu, wsd, rwt)


def _first_max(v, idx, sentinel):
    m = jnp.max(v, axis=0, keepdims=True)
    i = jnp.min(jnp.where(v == m, idx, sentinel), axis=0, keepdims=True)
    return m, i


def _route_kernel(lg_ref, bias_ref, eidx_ref, gate_ref):
    lg = lg_ref[...]
    tt = lg.shape[1]
    scores = 1.0 / (1.0 + jnp.exp(-lg))
    biased = scores + bias_ref[...]
    gi = lax.broadcasted_iota(I32, (GROUP_SIZE, tt), 0).astype(F32)
    gscore = []
    for g in range(N_GROUPS):
        blk = biased[g * GROUP_SIZE:(g + 1) * GROUP_SIZE, :]
        m1, i1 = _first_max(blk, gi, float(GROUP_SIZE))
        m2 = jnp.max(jnp.where(gi == i1, -jnp.inf, blk), axis=0, keepdims=True)
        gscore.append(m1 + m2)
    gs = jnp.concatenate(gscore, axis=0)
    gidx = lax.broadcasted_iota(I32, (N_GROUPS, tt), 0).astype(F32)
    chosen = jnp.zeros((N_GROUPS, tt), F32)
    for _ in range(TOPK_GROUPS):
        _, ig = _first_max(gs, gidx, float(N_GROUPS))
        hit = gidx == ig
        chosen = jnp.where(hit, 1.0, chosen)
        gs = jnp.where(hit, -jnp.inf, gs)
    masked = jnp.concatenate(
        [jnp.where(chosen[g:g + 1, :] > 0.5, biased[g * GROUP_SIZE:(g + 1) * GROUP_SIZE, :], -jnp.inf)
         for g in range(N_GROUPS)], axis=0)
    ei = lax.broadcasted_iota(I32, (N_EXPERTS, tt), 0).astype(F32)
    ids, ws = [], []
    for _ in range(TOP_K):
        _, ie = _first_max(masked, ei, float(N_EXPERTS))
        hit = ei == ie
        ws.append(jnp.sum(jnp.where(hit, scores, 0.0), axis=0, keepdims=True))
        ids.append(ie)
        masked = jnp.where(hit, -jnp.inf, masked)
    w = jnp.concatenate(ws, axis=0)
    gate_ref[...] = w / jnp.sum(w, axis=0, keepdims=True) * ROUTED_SCALE
    eidx_ref[...] = jnp.concatenate(ids, axis=0).astype(I32)


def _route(logits_t, bias_col, tt):
    E, T = logits_t.shape
    return pl.pallas_call(
        _route_kernel,
        grid=(T // tt,),
        in_specs=[pl.BlockSpec((E, tt), lambda i: (0, i)),
                  pl.BlockSpec((E, 1), lambda i: (0, 0))],
        out_specs=[pl.BlockSpec((TOP_K, tt), lambda i: (0, i)),
                   pl.BlockSpec((TOP_K, tt), lambda i: (0, i))],
        out_shape=[jax.ShapeDtypeStruct((TOP_K, T), I32),
                   jax.ShapeDtypeStruct((TOP_K, T), F32)],
        compiler_params=_params(("parallel",)),
        name="route",
    )(logits_t, bias_col)


def _plan_kernel(eidx_ref, dest_ref, bexp_ref, nused_ref, cnt_col, cnt_row, slot_base, *, blk, nb_pad):
    ph = pl.program_id(0)
    i = pl.program_id(1)
    tt = eidx_ref.shape[1]
    eidx = eidx_ref[...]
    ei = lax.broadcasted_iota(I32, (N_EXPERTS, tt), 0)
    onehot = jnp.zeros((N_EXPERTS, tt), F32)
    for k in range(TOP_K):
        onehot = onehot + jnp.where(ei == eidx[k:k + 1, :], 1.0, 0.0)
    oh = onehot.astype(BF16)

    @pl.when((ph == 0) & (i == 0))
    def _():
        cnt_col[...] = jnp.zeros(cnt_col.shape, F32)
        cnt_row[...] = jnp.zeros(cnt_row.shape, F32)

    @pl.when(ph == 0)
    def _():
        cnt_col[...] += _dot(oh, jnp.ones((tt, LANES), BF16))
        cnt_row[...] += _nt(jnp.ones((8, tt), BF16), oh)

    @pl.when((ph == 1) & (i == 0))
    def _():
        inv = 1.0 / blk
        nb_col = jnp.floor((cnt_col[:, 0:1] + (blk - 1)) * inv)
        nb_row = jnp.floor((cnt_row[0:1, :] + (blk - 1)) * inv)
        r = lax.broadcasted_iota(I32, (N_EXPERTS, N_EXPERTS), 0)
        c = lax.broadcasted_iota(I32, (N_EXPERTS, N_EXPERTS), 1)
        bstart = jnp.sum(jnp.where(c < r, nb_row, 0.0), axis=-1, keepdims=True)
        bend = bstart + nb_col
        slot_base[...] = bstart * blk
        jb = lax.broadcasted_iota(I32, (N_EXPERTS, nb_pad), 1).astype(F32)
        be = jnp.sum(jnp.where(bend <= jb, 1.0, 0.0), axis=0, keepdims=True)
        bexp_ref[...] = jnp.minimum(be, N_EXPERTS - 1.0).astype(I32)
        nused_ref[...] = jnp.broadcast_to(jnp.sum(nb_row, axis=-1, keepdims=True), nused_ref.shape).astype(I32)

    @pl.when(ph == 1)
    def _():
        tri = jnp.where(lax.broadcasted_iota(I32, (tt, tt), 0) < lax.broadcasted_iota(I32, (tt, tt), 1),
                        1.0, 0.0).astype(BF16)
        slot = _dot(oh, tri) + slot_base[...]
        for k in range(TOP_K):
            dk = jnp.sum(jnp.where(ei == eidx[k:k + 1, :], slot, 0.0), axis=0, keepdims=True)
            dest_ref[k:k + 1, :] = dk.astype(I32)
        slot_base[...] += jnp.sum(onehot, axis=-1, keepdims=True)


def _plan(eidx, tt, blk, nb_pad):
    K, T = eidx.shape
    nt = T // tt
    return pl.pallas_call(
        functools.partial(_plan_kernel, blk=blk, nb_pad=nb_pad),
        grid=(2, nt),
        in_specs=[pl.BlockSpec((K, tt), lambda p, i: (0, i))],
        out_specs=[pl.BlockSpec((K, tt), lambda p, i: (0, i * p)),
                   pl.BlockSpec((1, nb_pad), lambda p, i: (0, 0)),
                   pl.BlockSpec((1, LANES), lambda p, i: (0, 0))],
        out_shape=[jax.ShapeDtypeStruct((K, T), I32),
                   jax.ShapeDtypeStruct((1, nb_pad), I32),
                   jax.ShapeDtypeStruct((1, LANES), I32)],
        scratch_shapes=[pltpu.VMEM((N_EXPERTS, LANES), F32),
                        pltpu.VMEM((8, N_EXPERTS), F32),
                        pltpu.VMEM((N_EXPERTS, 1), F32)],
        compiler_params=_params(("arbitrary", "arbitrary")),
        name="plan",
    )(eidx)


def _row(ref, r):
    return ref.at[pl.ds(pl.multiple_of(r * PACK_TILES, PACK_TILES), PACK_TILES), :]


def _dispatch_kernel(dest_ref, h_ref, xs_in_ref, xs_ref, sem):
    del xs_in_ref
    tt = h_ref.shape[0] // PACK_TILES

    def issue(t, c):
        for k in range(TOP_K):
            pltpu.make_async_copy(_row(h_ref, t), _row(xs_ref, dest_ref[k, t]), sem).start()
        return c

    lax.fori_loop(0, tt, issue, 0)

    def drain(t, c):
        for k in range(TOP_K):
            pltpu.make_async_copy(_row(h_ref, 0), _row(xs_ref, 0), sem).wait()
        return c

    lax.fori_loop(0, tt, drain, 0)


def _dispatch(dest, h2rows, xs_init, tt):
    T = h2rows.shape[0] // PACK_TILES
    return pl.pallas_call(
        _dispatch_kernel,
        grid=(T // tt,),
        in_specs=[pl.BlockSpec((TOP_K, tt), lambda i: (0, i), memory_space=pltpu.SMEM),
                  pl.BlockSpec((tt * PACK_TILES, LANES), lambda i: (i, 0)),
                  pl.BlockSpec(memory_space=pl.ANY)],
        out_specs=pl.BlockSpec(memory_space=pl.ANY),
        out_shape=jax.ShapeDtypeStruct(xs_init.shape, xs_init.dtype),
        scratch_shapes=[pltpu.SemaphoreType.DMA(())],
        input_output_aliases={2: 0},
        compiler_params=_params(("arbitrary",), has_side_effects=True),
        name="dispatch",
    )(dest, h2rows, xs_init)


def _experts_kernel(bexp_ref, nused_ref, xs_ref, wg_ref, wu_ref, wd_ref, y_ref, xb_ref, wgb_ref, wub_ref, wdb_ref):
    j = pl.program_id(0)

    @pl.when(j < nused_ref[0])
    def _():
        @pl.when((j == 0) | (bexp_ref[j] != bexp_ref[jnp.maximum(j - 1, 0)]))
        def _():
            wgb_ref[...] = wg_ref[0].astype(BF16)
            wub_ref[...] = wu_ref[0].astype(BF16)
            wdb_ref[...] = wd_ref[0].astype(BF16)

        blk = xb_ref.shape[0]
        half = D_MODEL // 2
        for c in range(PACK_TILES):
            w = xs_ref[pl.ds(c, blk, stride=PACK_TILES), :]
            cols = slice(c * LANES, (c + 1) * LANES)
            xb_ref[:, cols] = lax.bitcast_convert_type(w << 16, F32).astype(BF16)
            xb_ref[:, half + c * LANES:half + (c + 1) * LANES] = lax.bitcast_convert_type(w & -65536, F32).astype(BF16)
        xb = xb_ref[...]
        gate = _dot(xb, wgb_ref[...])
        up = _dot(xb, wub_ref[...])
        act = gate / (1.0 + jnp.exp(-gate)) * up
        y = _dot(act.astype(BF16), wdb_ref[...])
        bits = lax.bitcast_convert_type(y.astype(BF16).astype(F32), I32)
        packed = (bits[:, half:] & -65536) | lax.shift_right_logical(bits[:, :half], 16)
        for c in range(PACK_TILES):
            y_ref[pl.ds(c, blk, stride=PACK_TILES), :] = packed[:, c * LANES:(c + 1) * LANES]


def _experts(bexp, nused, xs, wg, wu, wd, blk, n_blocks):
    live = lambda j, be, nu: jnp.minimum(j, nu[0] - 1)
    row_spec = pl.BlockSpec((blk * PACK_TILES, LANES), lambda j, be, nu: (live(j, be, nu), 0))
    wspec = lambda w: pl.BlockSpec((1,) + w.shape[1:], lambda j, be, nu: (be[live(j, be, nu)], 0, 0))
    return pl.pallas_call(
        _experts_kernel,
        grid_spec=pltpu.PrefetchScalarGridSpec(
            num_scalar_prefetch=2,
            grid=(n_blocks,),
            in_specs=[row_spec, wspec(wg), wspec(wu), wspec(wd)],
            out_specs=row_spec,
            scratch_shapes=[pltpu.VMEM((blk, D_MODEL), BF16),
                            pltpu.VMEM(wg.shape[1:], BF16), pltpu.VMEM(wu.shape[1:], BF16),
                            pltpu.VMEM(wd.shape[1:], BF16)]),
        out_shape=jax.ShapeDtypeStruct(xs.shape, I32),
        compiler_params=_params(("arbitrary",)),
        name="experts",
    )(bexp, nused, xs, wg, wu, wd)


def _combine_kernel(dest_ref, gate_ref, base_ref, mod_ref, y_ref, o_ref, buf, sem):
    tt = base_ref.shape[1]

    def issue(t, c):
        for k in range(TOP_K):
            pltpu.make_async_copy(_row(y_ref, dest_ref[k, t]), _row(buf, k * tt + t), sem).start()
        return c

    lax.fori_loop(0, tt, issue, 0)

    def drain(t, c):
        for k in range(TOP_K):
            pltpu.make_async_copy(_row(y_ref, 0), _row(buf, 0), sem).wait()
        return c

    lax.fori_loop(0, tt, drain, 0)

    gates = gate_ref[...]
    g2 = mod_ref[0][5:6]
    half = D_MODEL // 2
    for j in range(PACK_TILES):
        lo = jnp.zeros((tt, LANES), F32)
        hi = jnp.zeros((tt, LANES), F32)
        for k in range(TOP_K):
            w = buf[pl.ds(k * tt * PACK_TILES + j, tt, stride=PACK_TILES), :]
            lo = lo + gates[:, k:k + 1] * lax.bitcast_convert_type(w << 16, F32)
            hi = hi + gates[:, k:k + 1] * lax.bitcast_convert_type(w & -65536, F32)
        for off, acc in ((0, lo), (half, hi)):
            cols = slice(off + j * LANES, off + (j + 1) * LANES)
            o_ref[0, :, cols] = base_ref[0, :, cols] + g2[:, cols] * acc


def _combine(dest, gates_tk, base, mod3, y, tt):
    B, S, D = base.shape
    ns = S // tt
    return pl.pallas_call(
        _combine_kernel,
        grid=(B, ns),
        in_specs=[pl.BlockSpec((TOP_K, tt), lambda b, i: (0, b * ns + i), memory_space=pltpu.SMEM),
                  pl.BlockSpec((tt, TOP_K), lambda b, i: (b * ns + i, 0)),
                  pl.BlockSpec((1, tt, D), lambda b, i: (b, i, 0)),
                  pl.BlockSpec((1, 6, D), lambda b, i: (b, 0, 0)),
                  pl.BlockSpec(memory_space=pl.ANY)],
        out_specs=pl.BlockSpec((1, tt, D), lambda b, i: (b, i, 0)),
        out_shape=jax.ShapeDtypeStruct((B, S, D), F32),
        scratch_shapes=[pltpu.VMEM((TOP_K * tt * PACK_TILES, LANES), I32),
                        pltpu.SemaphoreType.DMA(())],
        compiler_params=_params(("arbitrary", "arbitrary")),
        name="combine",
    )(dest, gates_tk, base, mod3, y)


def _alibi_q_features(n_heads, maps_per_head):
    slopes = 2.0 ** (-8.0 * jnp.arange(1, n_heads + 1, dtype=F32) / n_heads)
    c = jnp.repeat(slopes, maps_per_head) * LOG2E * POS_RADIX
    pieces = []
    rest = c
    for _ in range(N_SPLIT):
        p = rest.astype(BF16).astype(F32)
        pieces.append(p)
        rest = rest - p
    hi = jnp.stack(pieces, axis=1)
    feat = jnp.concatenate([hi, hi / POS_RADIX], axis=1)
    return jnp.pad(feat, ((0, 0), (HEAD, LANES - HEAD - 2 * N_SPLIT)))


def kernel(x, c, ada_w, ada_b, norm1_g, norm2_g, w_in, diff_q_norm_g, diff_k_norm_g, lam_q1, lam_k1, lam_q2, lam_k2, diff_subln_g, dsa_q_norm_g, dsa_k_norm_g, idx_k_norm_g, w_out, router_w, router_bias, exp_w_gate, exp_w_up, exp_w_down, shared_w_gate, shared_w_up, shared_w_down):
    B, S, D = x.shape
    assert D == D_MODEL and ada_w.shape[0] == 1 and S <= POS_RADIX * 128
    T = B * S
    topk = min(DSA_TOPK, S // 4)
    tm = min(512, S)
    tq_diff = min(512, S)
    tq_dsa = min(256, S)
    tk_dsa = min(512, S)
    rs_dsa = min(128, tq_dsa)
    tt_route = min(512, T)
    tt_move = min(256, S)
    blk = 512
    n_blocks = (T * TOP_K) // blk + N_EXPERTS
    nb_pad = -(-n_blocks // LANES) * LANES

    n_main = 7 * SEG
    wm = w_in[0, :, :n_main].astype(BF16)
    wt = jnp.pad(w_in[0, :, n_main:], ((0, 0), (0, LANES - (IDX_DIM + IDX_HEADS)))).astype(BF16)
    tile8 = lambda g: jnp.tile(g[0], SEG // g.shape[1]).reshape(1, SEG)
    gik = jnp.pad(idx_k_norm_g[0], (0, LANES - IDX_DIM)).reshape(1, LANES)
    lamv = jnp.concatenate([lam_q1, lam_k1, lam_q2, lam_k2], axis=0)
    wo1 = w_out[0, :SEG].astype(BF16)
    wo2 = w_out[0, SEG:].astype(BF16)
    rwt = router_w[0].T.astype(BF16)
    wg, wu, wd = exp_w_gate[0], exp_w_up[0], exp_w_down[0]

    mod3 = _ada(c, ada_w[0], ada_b[0]).reshape(B, 6, D)

    dq, dk, dv, sq, sk, sv, iq, ikl, ikh, iw = _inproj(
        x, mod3, norm1_g, wm, wt, tile8(diff_q_norm_g), tile8(diff_k_norm_g),
        tile8(dsa_q_norm_g), tile8(dsa_k_norm_g), gik,
        _alibi_q_features(DIFF_HEADS, 2), _alibi_q_features(DSA_HEADS, 1), tm)

    diff_out = _diff_attention(dq, dk, dv, lamv, diff_subln_g, tq_diff)
    dsa_out = _dsa_attention(iq, iw, ikl, ikh, sq, sk, sv, tq_dsa, tk_dsa, rs_dsa, topk)

    base, h2rows, logits_t = _mix(diff_out, dsa_out, x, mod3, norm2_g, wo1, wo2,
                                  shared_w_gate[0].astype(BF16), shared_w_up[0].astype(BF16),
                                  shared_w_down[0].astype(BF16), rwt, tm)

    eidx, gates = _route(logits_t, router_bias[0].reshape(N_EXPERTS, 1), tt_route)
    dest, bexp, nused = _plan(eidx, tt_route, blk, nb_pad)

    xs = _dispatch(dest, h2rows, jnp.zeros((n_blocks * blk * PACK_TILES, LANES), I32), tt_move)
    y = _experts(bexp.reshape(nb_pad), nused[0, :1], xs, wg, wu, wd, blk, n_blocks)
    return _combine(dest, gates.T, base, mod3, y, tt_move)
```

```python
import functools
import math

import jax
import jax.numpy as jnp
from jax import lax
from jax.experimental import pallas as pl
from jax.experimental.pallas import tpu as pltpu

F32 = jnp.float32
BF16 = jnp.bfloat16
I32 = jnp.int32

D_MODEL = 1024
DIFF_HEADS = 4
DIFF_HEAD_DIM = 64
DSA_HEADS = 8
DSA_HEAD_DIM = 64
IDX_HEADS = 8
IDX_DIM = 64
DSA_TOPK = 256
N_EXPERTS = 256
TOP_K = 8
N_GROUPS = 8
GROUP_SIZE = N_EXPERTS // N_GROUPS
TOPK_GROUPS = 4
D_EXPERT = 256
D_SHARED = 256
ROUTED_SCALE = 2.5
EPS = 1e-6
LAM_INIT = 0.2

LANES = 128
PACK_TILES = D_MODEL // (2 * LANES)
SEG = 512
HEAD = 64
N_MAPS = SEG // HEAD
WIDE = N_MAPS * LANES
POS_RADIX = 64
N_SPLIT = 3
NEG_BIG = -1e30
INT_MIN = -2147483648
LOG2E = math.log2(math.e)
VMEM_LIMIT = 56 * 1024 * 1024

NT_DIMS = (((1,), (1,)), ((), ()))


def _nt(a, b):
    return lax.dot_general(a, b, NT_DIMS, preferred_element_type=F32)


def _dot(a, b):
    return jnp.dot(a, b, preferred_element_type=F32)


def _rep(x, reps):
    return jnp.concatenate([x] * reps, axis=1)


def _params(sem, vmem=VMEM_LIMIT, **kw):
    return pltpu.CompilerParams(dimension_semantics=sem, vmem_limit_bytes=vmem, **kw)


def _ada_kernel(c_ref, w_ref, b_ref, o_ref):
    c = c_ref[...]
    s = c / (1.0 + jnp.exp(-c))
    o_ref[...] = jnp.dot(s, w_ref[...], preferred_element_type=F32,
                         precision=lax.Precision.HIGHEST) + b_ref[...]


def _ada(c, w, b):
    B, D = c.shape
    N = w.shape[1]
    tn = D
    return pl.pallas_call(
        _ada_kernel,
        grid=(N // tn,),
        in_specs=[pl.BlockSpec((B, D), lambda j: (0, 0)),
                  pl.BlockSpec((D, tn), lambda j: (0, j)),
                  pl.BlockSpec((1, tn), lambda j: (0, j))],
        out_specs=pl.BlockSpec((B, tn), lambda j: (0, j)),
        out_shape=jax.ShapeDtypeStruct((B, N), F32),
        compiler_params=_params(("arbitrary",)),
        name="ada",
    )(c, w, b.reshape(1, N))


def _group_sumsq(z):
    n = z.shape[1]
    r = lax.broadcasted_iota(I32, (n, n), 0) // HEAD
    c = lax.broadcasted_iota(I32, (n, n), 1) // HEAD
    bd = jnp.where(r == c, 1.0, 0.0).astype(BF16)
    zz = z * z
    hi = zz.astype(BF16)
    lo = (zz - hi.astype(F32)).astype(BF16)
    return _dot(hi, bd) + _dot(lo, bd)


def _inproj_kernel(x_ref, mod_ref, g1_ref, wm_ref, wt_ref, gq_ref, gk_ref, gsq_ref, gsk_ref, gik_ref,
                   fdq_ref, fsq_ref,
                   dq_ref, dk_ref, dv_ref, sq_ref, sk_ref, sv_ref, iq_ref, ikl_ref, ikh_ref, iw_ref):
    x = x_ref[0]
    tm = x.shape[0]
    ms = jnp.mean(x * x, axis=-1, keepdims=True)
    y = x * lax.rsqrt(ms + EPS) * g1_ref[...]
    mod = mod_ref[0]
    h = y * (1.0 + mod[1:2]) + mod[0:1]
    hb = h.astype(BF16)

    lane = lax.broadcasted_iota(I32, (tm, LANES), 1)
    is_head = lane < HEAD
    kpos = pl.program_id(1) * tm + lax.broadcasted_iota(I32, (tm, LANES), 0)
    hi_digit = (kpos // POS_RADIX).astype(F32)
    lo_digit = (kpos % POS_RADIX).astype(F32)
    kfeat = jnp.where(lane < HEAD + N_SPLIT, hi_digit, jnp.where(lane < HEAD + 2 * N_SPLIT, lo_digit, 0.0))

    def plain(seg_idx, out_ref):
        out_ref[0] = _dot(hb, wm_ref[:, seg_idx * SEG:(seg_idx + 1) * SEG]).astype(BF16)

    def normed(seg_idx, g_ref, scale, feat_ref, out_ref):
        half = SEG // 2
        for i in range(2):
            lo = seg_idx * SEG + i * half
            z = _dot(hb, wm_ref[:, lo:lo + half])
            ss = _group_sumsq(z)
            zn = z * lax.rsqrt(ss * (1.0 / HEAD) + EPS) * (g_ref[:, i * half:(i + 1) * half] * scale)
            for g in range(half // LANES):
                zg = zn[:, g * LANES:(g + 1) * LANES]
                for odd in range(2):
                    idx = i * (half // HEAD) + 2 * g + odd
                    src = zg if odd == 0 else pltpu.roll(zg, HEAD, 1)
                    feat = kfeat if feat_ref is None else feat_ref[idx:idx + 1, :]
                    out_ref[0, :, idx * LANES:(idx + 1) * LANES] = jnp.where(is_head, src, feat).astype(BF16)

    normed(0, gq_ref, HEAD ** -0.5 * LOG2E, fdq_ref, dq_ref)
    normed(1, gk_ref, 1.0, None, dk_ref)
    plain(2, dv_ref)
    normed(3, gsq_ref, HEAD ** -0.5 * LOG2E, fsq_ref, sq_ref)
    normed(4, gsk_ref, 1.0, None, sk_ref)
    plain(5, sv_ref)
    plain(6, iq_ref)

    t = _dot(hb, wt_ref[...])
    ikraw = jnp.where(lane < IDX_DIM, t, 0.0)
    ss = jnp.sum(ikraw * ikraw, axis=-1, keepdims=True) * (1.0 / IDX_DIM)
    ikn = ikraw * lax.rsqrt(ss + EPS) * gik_ref[...]
    ikl_ref[0] = ikn.astype(BF16)
    ikh_ref[0] = pltpu.roll(ikn, IDX_DIM, 1).astype(BF16)
    iwraw = jnp.where((lane >= IDX_DIM) & (lane < IDX_DIM + IDX_HEADS), t, 0.0)
    iw_ref[0] = pltpu.roll(iwraw * (IDX_HEADS ** -0.5), LANES - IDX_DIM, 1) * (IDX_DIM ** -0.5)


def _inproj(x, mod3, g1, wm, wt, gq, gk, gsq, gsk, gik, fdq, fsq, tm):
    B, S, D = x.shape
    ns = S // tm
    tok = lambda b, i: (b, i, 0)
    const2 = lambda b, i: (0, 0)
    seg_spec = pl.BlockSpec((1, tm, SEG), tok)
    wide_spec = pl.BlockSpec((1, tm, WIDE), tok)
    lane_spec = pl.BlockSpec((1, tm, LANES), tok)
    seg_shape = jax.ShapeDtypeStruct((B, S, SEG), BF16)
    wide_shape = jax.ShapeDtypeStruct((B, S, WIDE), BF16)
    return pl.pallas_call(
        _inproj_kernel,
        grid=(B, ns),
        in_specs=[pl.BlockSpec((1, tm, D), tok),
                  pl.BlockSpec((1, 6, D), lambda b, i: (b, 0, 0)),
                  pl.BlockSpec((1, D), const2),
                  pl.BlockSpec(wm.shape, const2),
                  pl.BlockSpec(wt.shape, const2),
                  pl.BlockSpec((1, SEG), const2), pl.BlockSpec((1, SEG), const2),
                  pl.BlockSpec((1, SEG), const2), pl.BlockSpec((1, SEG), const2),
                  pl.BlockSpec((1, LANES), const2),
                  pl.BlockSpec((N_MAPS, LANES), const2), pl.BlockSpec((N_MAPS, LANES), const2)],
        out_specs=[wide_spec, wide_spec, seg_spec, wide_spec, wide_spec, seg_spec, seg_spec,
                   lane_spec, lane_spec, lane_spec],
        out_shape=[wide_shape, wide_shape, seg_shape, wide_shape, wide_shape, seg_shape, seg_shape,
                   jax.ShapeDtypeStruct((B, S, LANES), BF16), jax.ShapeDtypeStruct((B, S, LANES), BF16),
                   jax.ShapeDtypeStruct((B, S, LANES), F32)],
        compiler_params=_params(("parallel", "parallel")),
        name="inproj",
    )(x, mod3, g1, wm, wt, gq, gk, gsq, gsk, gik, fdq, fsq)


def _diff_kernel(q_ref, k_ref, v_ref, lam_ref, g_ref, o_ref, s_ref, m_ref, l_ref, acc_ref, *, tq):
    qi = pl.program_id(1)
    ki = pl.program_id(2)
    n_maps = 2 * DIFF_HEADS
    reps = tq // LANES

    @pl.when(ki == 0)
    def _():
        m_ref[...] = jnp.full(m_ref.shape, NEG_BIG, F32)
        l_ref[...] = jnp.zeros(l_ref.shape, F32)
        acc_ref[...] = jnp.zeros(acc_ref.shape, F32)

    def step(diag):
        if diag:
            row = lax.broadcasted_iota(I32, (tq, tq), 0)
            col = lax.broadcasted_iota(I32, (tq, tq), 1)
            causal_bias = jnp.where(col <= row, 0.0, NEG_BIG)
        m_olds, m_news = [], []
        for idx in range(n_maps):
            s = _nt(q_ref[0, :, idx * LANES:(idx + 1) * LANES], k_ref[0, :, idx * LANES:(idx + 1) * LANES])
            if diag:
                s = s + causal_bias
            s_ref[idx] = s
            m_old = m_ref[idx]
            m_olds.append(m_old)
            m_news.append(jnp.maximum(m_old, jnp.max(s, axis=-1, keepdims=True)))
        for idx in range(n_maps):
            h = idx // 2
            p = jnp.exp2(s_ref[idx] - _rep(m_news[idx], reps))
            alpha = jnp.exp2(m_olds[idx] - m_news[idx])
            psum = p[:, 0:LANES]
            for g in range(1, reps):
                psum = psum + p[:, g * LANES:(g + 1) * LANES]
            l_ref[idx] = alpha * l_ref[idx] + psum
            acc_ref[idx] = alpha * acc_ref[idx] + _dot(p.astype(BF16), v_ref[0, :, h * LANES:(h + 1) * LANES])
            m_ref[idx] = m_news[idx]

    @pl.when(ki < qi)
    def _():
        step(False)

    @pl.when(ki == qi)
    def _():
        step(True)
        lv = lam_ref[...]
        lam = (jnp.exp(jnp.sum(lv[0:1] * lv[1:2], axis=-1, keepdims=True))
               - jnp.exp(jnp.sum(lv[2:3] * lv[3:4], axis=-1, keepdims=True)) + LAM_INIT)
        for h in range(DIFF_HEADS):
            o1 = acc_ref[2 * h] / jnp.sum(l_ref[2 * h], axis=-1, keepdims=True)
            o2 = acc_ref[2 * h + 1] / jnp.sum(l_ref[2 * h + 1], axis=-1, keepdims=True)
            o = o1 - lam * o2
            ms = jnp.mean(o * o, axis=-1, keepdims=True)
            on = o * lax.rsqrt(ms + EPS) * g_ref[...]
            o_ref[0, :, h * LANES:(h + 1) * LANES] = (on * (1.0 - LAM_INIT)).astype(BF16)


def _diff_attention(dq, dk, dv, lamv, subln_g, tq):
    B, S, W = dq.shape
    V = dv.shape[2]
    nq = S // tq
    n_maps = 2 * DIFF_HEADS
    return pl.pallas_call(
        functools.partial(_diff_kernel, tq=tq),
        grid=(B, nq, nq),
        in_specs=[pl.BlockSpec((1, tq, W), lambda b, i, j: (b, i, 0)),
                  pl.BlockSpec((1, tq, W), lambda b, i, j: (b, jnp.minimum(i, j), 0)),
                  pl.BlockSpec((1, tq, V), lambda b, i, j: (b, jnp.minimum(i, j), 0)),
                  pl.BlockSpec(lamv.shape, lambda b, i, j: (0, 0)),
                  pl.BlockSpec((1, LANES), lambda b, i, j: (0, 0))],
        out_specs=pl.BlockSpec((1, tq, V), lambda b, i, j: (b, i, 0)),
        out_shape=jax.ShapeDtypeStruct((B, S, V), BF16),
        scratch_shapes=[pltpu.VMEM((n_maps, tq, tq), F32),
                        pltpu.VMEM((n_maps, tq, LANES), F32),
                        pltpu.VMEM((n_maps, tq, LANES), F32),
                        pltpu.VMEM((n_maps, tq, LANES), F32)],
        compiler_params=_params(("parallel", "parallel", "arbitrary")),
        name="diff_attention",
    )(dq, dk, dv, lamv, subln_g)


def _score_key(v):
    bits = lax.bitcast_convert_type(v, I32)
    return bits ^ ((bits >> 31) & 0x7FFFFFFF)


def _key_score(k):
    return lax.bitcast_convert_type(k ^ ((k >> 31) & 0x7FFFFFFF), F32)


_SUM_LANE = (LANES - 1, 0)
N_CAND = 12
CAND_ROWS = 16


def _dsa_kernel(iq_ref, iw_ref, ikl_ref, ikh_ref, q_ref, k_ref, v_ref, o_ref,
                key_ref, cand_ref, thr_ref, nties_ref, nge_ref, mb_ref, s_ref, m_ref, acc_ref, *, tq, tk, rs, topk):
    qi = pl.program_id(1)
    q0 = qi * tq
    nkc = (q0 + tq + tk - 1) // tk
    row = q0 + lax.broadcasted_iota(I32, (tq, tk), 0)
    col0 = lax.broadcasted_iota(I32, (tq, tk), 1)
    low = lax.broadcasted_iota(I32, (tq, LANES), 1) < HEAD
    reps = tk // LANES
    iw = iw_ref[0]

    def score_chunk(kc, carry):
        k0 = pl.multiple_of(kc * tk, tk)
        ikl = ikl_ref[0, pl.ds(k0, tk), :]
        ikh = ikh_ref[0, pl.ds(k0, tk), :]
        sc = jnp.zeros((tq, tk), F32)
        for j in range(IDX_HEADS // 2):
            iqp = iq_ref[0, :, j * LANES:(j + 1) * LANES]
            sc = sc + iw[:, 2 * j:2 * j + 1] * jnp.maximum(_nt(iqp, ikl), 0.0)
            sc = sc + iw[:, 2 * j + 1:2 * j + 2] * jnp.maximum(_nt(iqp, ikh), 0.0)
        sc = jnp.where(sc == 0.0, 0.0, sc)
        sc = jnp.where(col0 + k0 <= row, sc, -jnp.inf)
        key_ref[:, pl.ds(k0, tk)] = _score_key(sc)
        return carry

    lax.fori_loop(0, nkc, score_chunk, 0)

    groups = [slice(r0, r0 + rs) for r0 in range(0, tq, rs)]
    kf = float(topk)

    def candidates(g, carry):
        rows = pl.ds(pl.multiple_of(g * CAND_ROWS, CAND_ROWS), CAND_ROWS)

        def insert_chunk(kc, best):
            k0 = pl.multiple_of(kc * tk, tk)
            x = _key_score(key_ref[rows, pl.ds(k0, tk)])
            best = list(best)
            for lg in range(reps):
                v = x[:, lg * LANES:(lg + 1) * LANES]
                for i in range(N_CAND):
                    best[i], v = jnp.maximum(best[i], v), jnp.minimum(best[i], v)
            return tuple(best)

        lowest = jnp.full((CAND_ROWS, LANES), -jnp.inf, F32)
        best = lax.fori_loop(0, nkc, insert_chunk, (lowest,) * N_CAND)
        for i in range(N_CAND):
            cand_ref[rows, i * LANES:(i + 1) * LANES] = _score_key(best[i])
        return carry

    lax.fori_loop(0, tq // CAND_ROWS, candidates, 0)

    def count(ref, n_chunks, thrs, strict):
        accs = []
        for rows, thr in zip(groups, thrs):
            thr_t = _rep(thr, reps)

            def body(kc, acc, rows=rows, thr_t=thr_t):
                k0 = pl.multiple_of(kc * tk, tk)
                keyc = ref[rows, pl.ds(k0, tk)]
                hit = jnp.where((keyc > thr_t) if strict else (keyc >= thr_t), 1.0, 0.0)
                for g in range(reps):
                    acc = acc + hit[:, g * LANES:(g + 1) * LANES]
                return acc

            accs.append(lax.fori_loop(0, n_chunks, body, jnp.zeros((rs, LANES), F32)))
        return [jnp.broadcast_to(jnp.sum(acc, axis=-1, keepdims=True), (rs, LANES)) for acc in accs]

    def search(ref, n_chunks):
        def bit_step(i, tus):
            bit = jnp.left_shift(jnp.int32(1), 31 - i)
            cands = [tu | bit for tu in tus]
            cnts = count(ref, n_chunks, [c ^ INT_MIN for c in cands], False)
            return tuple(jnp.where(cnt >= kf, c, tu) for cnt, c, tu in zip(cnts, cands, tus))

        tus = lax.fori_loop(0, 32, bit_step, tuple(jnp.zeros((rs, LANES), I32) for _ in groups))
        return [tu ^ INT_MIN for tu in tus]

    def publish(thrs, n_above, n_reach):
        for rows, thr, n_gt, n_ge in zip(groups, thrs, n_above, n_reach):
            thr_ref[rows, :] = thr
            nties_ref[rows, :] = kf - n_gt
            nge_ref[rows, :] = n_ge

    cand_chunks = N_CAND * LANES // tk
    thrs = search(cand_ref, cand_chunks)
    inside = count(cand_ref, cand_chunks, thrs, False)
    publish(thrs, count(cand_ref, cand_chunks, thrs, True), inside)
    overall = count(key_ref, nkc, thrs, False)
    missed = [jnp.max(jnp.where(a != b, 1.0, 0.0)) for a, b in zip(inside, overall)]

    @pl.when(functools.reduce(jnp.maximum, missed) > 0.0)
    def _():
        full = search(key_ref, nkc)
        publish(full, count(key_ref, nkc, full, True), count(key_ref, nkc, full, False))

    m_ref[...] = jnp.full(m_ref.shape, NEG_BIG, F32)
    acc_ref[...] = jnp.zeros(acc_ref.shape, F32)
    tie_rows = jnp.max(jnp.where(nge_ref[...] != kf, 1.0, 0.0))

    def attend_chunk(kc, ties_before, ranked):
        k0 = pl.multiple_of(kc * tk, tk)
        keyc = key_ref[:, pl.ds(k0, tk)]
        thr_t = _rep(thr_ref[...], reps)
        if ranked:
            tri = jnp.where(lax.broadcasted_iota(I32, (tk, tk), 0) < lax.broadcasted_iota(I32, (tk, tk), 1),
                            1.0, 0.0).astype(BF16)
            eq = jnp.where(keyc == thr_t, 1.0, 0.0)
            rank = _dot(eq.astype(BF16), tri) + _rep(ties_before, reps)
            take = jnp.where(keyc > thr_t, 1.0, jnp.where(rank < _rep(nties_ref[...], reps), eq, 0.0))
            psum = eq[:, 0:LANES]
            for g in range(1, reps):
                psum = psum + eq[:, g * LANES:(g + 1) * LANES]
            ties_before = ties_before + jnp.sum(psum, axis=-1, keepdims=True)
            picked = take > 0.5
        else:
            picked = keyc >= thr_t
        mb_ref[...] = jnp.where(col0 + k0 <= row, jnp.where(picked, 0.0, NEG_BIG), NEG_BIG)
        mb = mb_ref[...]
        m_olds, m_news = [], []
        for hd in range(DSA_HEADS):
            s = _nt(q_ref[0, :, hd * LANES:(hd + 1) * LANES],
                    k_ref[0, pl.ds(k0, tk), hd * LANES:(hd + 1) * LANES]) + mb
            s_ref[hd] = s
            m_old = m_ref[hd]
            m_olds.append(m_old)
            m_news.append(jnp.maximum(m_old, jnp.max(s, axis=-1, keepdims=True)))
        for j in range(DSA_HEADS // 2):
            vp = v_ref[0, pl.ds(k0, tk), j * LANES:(j + 1) * LANES]
            vlane = lax.broadcasted_iota(I32, vp.shape, 1)
            for a in range(2):
                hd = 2 * j + a
                keep = (vlane < HEAD) if a == 0 else (vlane >= HEAD)
                va = jnp.where(vlane == _SUM_LANE[a], jnp.ones_like(vp), jnp.where(keep, vp, jnp.zeros_like(vp)))
                p = jnp.exp2((s_ref[hd] - _rep(m_news[hd], reps)).astype(BF16))
                acc_ref[hd] = jnp.exp2(m_olds[hd] - m_news[hd]) * acc_ref[hd] + _dot(p, va)
                m_ref[hd] = m_news[hd]
        return ties_before

    for ranked in (False, True):
        @pl.when((tie_rows > 0.0) == ranked)
        def _(ranked=ranked):
            lax.fori_loop(0, nkc, functools.partial(attend_chunk, ranked=ranked), jnp.zeros((tq, LANES), F32))

    lane = lax.broadcasted_iota(I32, (tq, LANES), 1)
    for j in range(DSA_HEADS // 2):
        oa, ob = acc_ref[2 * j], acc_ref[2 * j + 1]
        la = jnp.sum(jnp.where(lane == _SUM_LANE[0], oa, 0.0), axis=-1, keepdims=True)
        lb = jnp.sum(jnp.where(lane == _SUM_LANE[1], ob, 0.0), axis=-1, keepdims=True)
        o_ref[0, :, j * LANES:(j + 1) * LANES] = jnp.where(low, oa / la, ob / lb).astype(BF16)


def _dsa_attention(iq, iw, ikl, ikh, sq, sk, sv, tq, tk, rs, topk):
    B, S, W = sq.shape
    V = sv.shape[2]
    nq = S // tq
    assert (N_CAND * LANES) % tk == 0 and tq % CAND_ROWS == 0 and tq % rs == 0
    qblk = lambda b, i: (b, i, 0)
    full = lambda b, i: (b, 0, 0)
    once = pl.Buffered(1)
    return pl.pallas_call(
        functools.partial(_dsa_kernel, tq=tq, tk=tk, rs=rs, topk=topk),
        grid=(B, nq),
        in_specs=[pl.BlockSpec((1, tq, iq.shape[2]), qblk),
                  pl.BlockSpec((1, tq, LANES), qblk),
                  pl.BlockSpec((1, S, LANES), full, pipeline_mode=once),
                  pl.BlockSpec((1, S, LANES), full, pipeline_mode=once),
                  pl.BlockSpec((1, tq, W), qblk),
                  pl.BlockSpec((1, S, W), full, pipeline_mode=once),
                  pl.BlockSpec((1, S, V), full, pipeline_mode=once)],
        out_specs=pl.BlockSpec((1, tq, V), qblk),
        out_shape=jax.ShapeDtypeStruct((B, S, V), BF16),
        scratch_shapes=[pltpu.VMEM((tq, S), I32),
                        pltpu.VMEM((tq, N_CAND * LANES), I32),
                        pltpu.VMEM((tq, LANES), I32),
                        pltpu.VMEM((tq, LANES), F32),
                        pltpu.VMEM((tq, LANES), F32),
                        pltpu.VMEM((tq, tk), F32),
                        pltpu.VMEM((DSA_HEADS, tq, tk), F32),
                        pltpu.VMEM((DSA_HEADS, tq, LANES), F32),
                        pltpu.VMEM((DSA_HEADS, tq, LANES), F32)],
        compiler_params=_params(("parallel", "arbitrary")),
        name="dsa_attention",
    )(iq, iw, ikl, ikh, sq, sk, sv)


def _mix_kernel(do_ref, so_ref, x_ref, mod_ref, g2_ref, wo1_ref, wo2_ref, wsg_ref, wsu_ref, wsd_ref,
                rwt_ref, base_ref, h2_ref, lg_ref):
    mix = _dot(do_ref[0], wo1_ref[...]) + _dot(so_ref[0], wo2_ref[...])
    mod = mod_ref[0]
    x1 = x_ref[0] + mod[2:3] * mix
    ms = jnp.mean(x1 * x1, axis=-1, keepdims=True)
    h2 = x1 * lax.rsqrt(ms + EPS) * g2_ref[...] * (1.0 + mod[4:5]) + mod[3:4]
    hb = h2.astype(BF16)
    gate = _dot(hb, wsg_ref[...])
    up = _dot(hb, wsu_ref[...])
    act = gate / (1.0 + jnp.exp(-gate)) * up
    shared = _dot(act.astype(BF16), wsd_ref[...])
    base_ref[0] = x1 + mod[5:6] * shared
    lg_ref[...] = _nt(rwt_ref[...], hb)
    hf = hb.astype(F32)
    tm = hf.shape[0]
    bits = lax.bitcast_convert_type(hf, I32)
    half = D_MODEL // 2
    packed = (bits[:, half:] & -65536) | lax.shift_right_logical(bits[:, :half], 16)
    for j in range(PACK_TILES):
        h2_ref[pl.ds(j, tm, stride=PACK_TILES), :] = packed[:, j * LANES:(j + 1) * LANES]


def _mix(diff_out, dsa_out, x, mod3, g2, wo1, wo2, wsg, wsu, wsd, rwt, tm):
    B, S, D = x.shape
    ns = S // tm
    T = B * S
    tok = lambda b, i: (b, i, 0)
    c2 = lambda b, i: (0, 0)
    return pl.pallas_call(
        _mix_kernel,
        grid=(B, ns),
        in_specs=[pl.BlockSpec((1, tm, SEG), tok), pl.BlockSpec((1, tm, SEG), tok),
                  pl.BlockSpec((1, tm, D), tok),
                  pl.BlockSpec((1, 6, D), lambda b, i: (b, 0, 0)),
                  pl.BlockSpec((1, D), c2),
                  pl.BlockSpec(wo1.shape, c2), pl.BlockSpec(wo2.shape, c2),
                  pl.BlockSpec(wsg.shape, c2), pl.BlockSpec(wsu.shape, c2), pl.BlockSpec(wsd.shape, c2),
                  pl.BlockSpec(rwt.shape, c2)],
        out_specs=[pl.BlockSpec((1, tm, D), tok),
                   pl.BlockSpec((tm * PACK_TILES, LANES), lambda b, i: (b * ns + i, 0)),
                   pl.BlockSpec((N_EXPERTS, tm), lambda b, i: (0, b * ns + i))],
        out_shape=[jax.ShapeDtypeStruct((B, S, D), F32),
                   jax.ShapeDtypeStruct((T * PACK_TILES, LANES), I32),
                   jax.ShapeDtypeStruct((N_EXPERTS, T), F32)],
        compiler_params=_params(("parallel", "parallel")),
        name="mix_shared_router",
    )(diff_out, dsa_out, x, mod3, g2, wo1, wo2, wsg, wsu, wsd, rwt)


def _first_max(v, idx, sentinel):
    m = jnp.max(v, axis=0, keepdims=True)
    i = jnp.min(jnp.where(v == m, idx, sentinel), axis=0, keepdims=True)
    return m, i


def _route_kernel(lg_ref, bias_ref, eidx_ref, gate_ref):
    lg = lg_ref[...]
    tt = lg.shape[1]
    scores = 1.0 / (1.0 + jnp.exp(-lg))
    biased = scores + bias_ref[...]
    gi = lax.broadcasted_iota(I32, (GROUP_SIZE, tt), 0).astype(F32)
    gscore = []
    for g in range(N_GROUPS):
        blk = biased[g * GROUP_SIZE:(g + 1) * GROUP_SIZE, :]
        m1, i1 = _first_max(blk, gi, float(GROUP_SIZE))
        m2 = jnp.max(jnp.where(gi == i1, -jnp.inf, blk), axis=0, keepdims=True)
        gscore.append(m1 + m2)
    gs = jnp.concatenate(gscore, axis=0)
    gidx = lax.broadcasted_iota(I32, (N_GROUPS, tt), 0).astype(F32)
    chosen = jnp.zeros((N_GROUPS, tt), F32)
    for _ in range(TOPK_GROUPS):
        _, ig = _first_max(gs, gidx, float(N_GROUPS))
        hit = gidx == ig
        chosen = jnp.where(hit, 1.0, chosen)
        gs = jnp.where(hit, -jnp.inf, gs)
    masked = jnp.concatenate(
        [jnp.where(chosen[g:g + 1, :] > 0.5, biased[g * GROUP_SIZE:(g + 1) * GROUP_SIZE, :], -jnp.inf)
         for g in range(N_GROUPS)], axis=0)
    ei = lax.broadcasted_iota(I32, (N_EXPERTS, tt), 0).astype(F32)
    ids, ws = [], []
    for _ in range(TOP_K):
        _, ie = _first_max(masked, ei, float(N_EXPERTS))
        hit = ei == ie
        ws.append(jnp.sum(jnp.where(hit, scores, 0.0), axis=0, keepdims=True))
        ids.append(ie)
        masked = jnp.where(hit, -jnp.inf, masked)
    w = jnp.concatenate(ws, axis=0)
    gate_ref[...] = w / jnp.sum(w, axis=0, keepdims=True) * ROUTED_SCALE
    eidx_ref[...] = jnp.concatenate(ids, axis=0).astype(I32)


def _route(logits_t, bias_col, tt):
    E, T = logits_t.shape
    return pl.pallas_call(
        _route_kernel,
        grid=(T // tt,),
        in_specs=[pl.BlockSpec((E, tt), lambda i: (0, i)),
                  pl.BlockSpec((E, 1), lambda i: (0, 0))],
        out_specs=[pl.BlockSpec((TOP_K, tt), lambda i: (0, i)),
                   pl.BlockSpec((TOP_K, tt), lambda i: (0, i))],
        out_shape=[jax.ShapeDtypeStruct((TOP_K, T), I32),
                   jax.ShapeDtypeStruct((TOP_K, T), F32)],
        compiler_params=_params(("parallel",)),
        name="route",
    )(logits_t, bias_col)


def _plan_kernel(eidx_ref, dest_ref, bexp_ref, nused_ref, cnt_col, cnt_row, slot_base, *, blk, nb_pad):
    ph = pl.program_id(0)
    i = pl.program_id(1)
    tt = eidx_ref.shape[1]
    eidx = eidx_ref[...]
    ei = lax.broadcasted_iota(I32, (N_EXPERTS, tt), 0)
    onehot = jnp.zeros((N_EXPERTS, tt), F32)
    for k in range(TOP_K):
        onehot = onehot + jnp.where(ei == eidx[k:k + 1, :], 1.0, 0.0)
    oh = onehot.astype(BF16)

    @pl.when((ph == 0) & (i == 0))
    def _():
        cnt_col[...] = jnp.zeros(cnt_col.shape, F32)
        cnt_row[...] = jnp.zeros(cnt_row.shape, F32)

    @pl.when(ph == 0)
    def _():
        cnt_col[...] += _dot(oh, jnp.ones((tt, LANES), BF16))
        cnt_row[...] += _nt(jnp.ones((8, tt), BF16), oh)

    @pl.when((ph == 1) & (i == 0))
    def _():
        inv = 1.0 / blk
        nb_col = jnp.floor((cnt_col[:, 0:1] + (blk - 1)) * inv)
        nb_row = jnp.floor((cnt_row[0:1, :] + (blk - 1)) * inv)
        r = lax.broadcasted_iota(I32, (N_EXPERTS, N_EXPERTS), 0)
        c = lax.broadcasted_iota(I32, (N_EXPERTS, N_EXPERTS), 1)
        bstart = jnp.sum(jnp.where(c < r, nb_row, 0.0), axis=-1, keepdims=True)
        bend = bstart + nb_col
        slot_base[...] = bstart * blk
        jb = lax.broadcasted_iota(I32, (N_EXPERTS, nb_pad), 1).astype(F32)
        be = jnp.sum(jnp.where(bend <= jb, 1.0, 0.0), axis=0, keepdims=True)
        bexp_ref[...] = jnp.minimum(be, N_EXPERTS - 1.0).astype(I32)
        nused_ref[...] = jnp.broadcast_to(jnp.sum(nb_row, axis=-1, keepdims=True), nused_ref.shape).astype(I32)

    @pl.when(ph == 1)
    def _():
        tri = jnp.where(lax.broadcasted_iota(I32, (tt, tt), 0) < lax.broadcasted_iota(I32, (tt, tt), 1),
                        1.0, 0.0).astype(BF16)
        slot = _dot(oh, tri) + slot_base[...]
        for k in range(TOP_K):
            dk = jnp.sum(jnp.where(ei == eidx[k:k + 1, :], slot, 0.0), axis=0, keepdims=True)
            dest_ref[k:k + 1, :] = dk.astype(I32)
        slot_base[...] += jnp.sum(onehot, axis=-1, keepdims=True)


def _plan(eidx, tt, blk, nb_pad):
    K, T = eidx.shape
    nt = T // tt
    return pl.pallas_call(
        functools.partial(_plan_kernel, blk=blk, nb_pad=nb_pad),
        grid=(2, nt),
        in_specs=[pl.BlockSpec((K, tt), lambda p, i: (0, i))],
        out_specs=[pl.BlockSpec((K, tt), lambda p, i: (0, i * p)),
                   pl.BlockSpec((1, nb_pad), lambda p, i: (0, 0)),
                   pl.BlockSpec((1, LANES), lambda p, i: (0, 0))],
        out_shape=[jax.ShapeDtypeStruct((K, T), I32),
                   jax.ShapeDtypeStruct((1, nb_pad), I32),
                   jax.ShapeDtypeStruct((1, LANES), I32)],
        scratch_shapes=[pltpu.VMEM((N_EXPERTS, LANES), F32),
                        pltpu.VMEM((8, N_EXPERTS), F32),
                        pltpu.VMEM((N_EXPERTS, 1), F32)],
        compiler_params=_params(("arbitrary", "arbitrary")),
        name="plan",
    )(eidx)


def _row(ref, r):
    return ref.at[pl.ds(pl.multiple_of(r * PACK_TILES, PACK_TILES), PACK_TILES), :]


def _dispatch_kernel(dest_ref, h_ref, xs_in_ref, xs_ref, sem):
    del xs_in_ref
    tt = h_ref.shape[0] // PACK_TILES

    def issue(t, c):
        for k in range(TOP_K):
            pltpu.make_async_copy(_row(h_ref, t), _row(xs_ref, dest_ref[k, t]), sem).start()
        return c

    lax.fori_loop(0, tt, issue, 0)

    def drain(t, c):
        for k in range(TOP_K):
            pltpu.make_async_copy(_row(h_ref, 0), _row(xs_ref, 0), sem).wait()
        return c

    lax.fori_loop(0, tt, drain, 0)


def _dispatch(dest, h2rows, xs_init, tt):
    T = h2rows.shape[0] // PACK_TILES
    return pl.pallas_call(
        _dispatch_kernel,
        grid=(T // tt,),
        in_specs=[pl.BlockSpec((TOP_K, tt), lambda i: (0, i), memory_space=pltpu.SMEM),
                  pl.BlockSpec((tt * PACK_TILES, LANES), lambda i: (i, 0)),
                  pl.BlockSpec(memory_space=pl.ANY)],
        out_specs=pl.BlockSpec(memory_space=pl.ANY),
        out_shape=jax.ShapeDtypeStruct(xs_init.shape, xs_init.dtype),
        scratch_shapes=[pltpu.SemaphoreType.DMA(())],
        input_output_aliases={2: 0},
        compiler_params=_params(("arbitrary",), has_side_effects=True),
        name="dispatch",
    )(dest, h2rows, xs_init)


def _experts_kernel(bexp_ref, nused_ref, xs_ref, wg_ref, wu_ref, wd_ref, y_ref, xb_ref, wgb_ref, wub_ref, wdb_ref):
    j = pl.program_id(0)

    @pl.when(j < nused_ref[0])
    def _():
        @pl.when((j == 0) | (bexp_ref[j] != bexp_ref[jnp.maximum(j - 1, 0)]))
        def _():
            wgb_ref[...] = wg_ref[0].astype(BF16)
            wub_ref[...] = wu_ref[0].astype(BF16)
            wdb_ref[...] = wd_ref[0].astype(BF16)

        blk = xb_ref.shape[0]
        half = D_MODEL // 2
        for c in range(PACK_TILES):
            w = xs_ref[pl.ds(c, blk, stride=PACK_TILES), :]
            cols = slice(c * LANES, (c + 1) * LANES)
            xb_ref[:, cols] = lax.bitcast_convert_type(w << 16, F32).astype(BF16)
            xb_ref[:, half + c * LANES:half + (c + 1) * LANES] = lax.bitcast_convert_type(w & -65536, F32).astype(BF16)
        xb = xb_ref[...]
        gate = _dot(xb, wgb_ref[...])
        up = _dot(xb, wub_ref[...])
        act = gate / (1.0 + jnp.exp(-gate)) * up
        y = _dot(act.astype(BF16), wdb_ref[...])
        bits = lax.bitcast_convert_type(y.astype(BF16).astype(F32), I32)
        packed = (bits[:, half:] & -65536) | lax.shift_right_logical(bits[:, :half], 16)
        for c in range(PACK_TILES):
            y_ref[pl.ds(c, blk, stride=PACK_TILES), :] = packed[:, c * LANES:(c + 1) * LANES]


def _experts(bexp, nused, xs, wg, wu, wd, blk, n_blocks):
    live = lambda j, be, nu: jnp.minimum(j, nu[0] - 1)
    row_spec = pl.BlockSpec((blk * PACK_TILES, LANES), lambda j, be, nu: (live(j, be, nu), 0))
    wspec = lambda w: pl.BlockSpec((1,) + w.shape[1:], lambda j, be, nu: (be[live(j, be, nu)], 0, 0))
    return pl.pallas_call(
        _experts_kernel,
        grid_spec=pltpu.PrefetchScalarGridSpec(
            num_scalar_prefetch=2,
            grid=(n_blocks,),
            in_specs=[row_spec, wspec(wg), wspec(wu), wspec(wd)],
            out_specs=row_spec,
            scratch_shapes=[pltpu.VMEM((blk, D_MODEL), BF16),
                            pltpu.VMEM(wg.shape[1:], BF16), pltpu.VMEM(wu.shape[1:], BF16),
                            pltpu.VMEM(wd.shape[1:], BF16)]),
        out_shape=jax.ShapeDtypeStruct(xs.shape, I32),
        compiler_params=_params(("arbitrary",)),
        name="experts",
    )(bexp, nused, xs, wg, wu, wd)


def _combine_kernel(dest_ref, gate_ref, base_ref, mod_ref, y_ref, o_ref, buf, sem):
    tt = base_ref.shape[1]

    def issue(t, c):
        for k in range(TOP_K):
            pltpu.make_async_copy(_row(y_ref, dest_ref[k, t]), _row(buf, k * tt + t), sem).start()
        return c

    lax.fori_loop(0, tt, issue, 0)

    def drain(t, c):
        for k in range(TOP_K):
            pltpu.make_async_copy(_row(y_ref, 0), _row(buf, 0), sem).wait()
        return c

    lax.fori_loop(0, tt, drain, 0)

    gates = gate_ref[...]
    g2 = mod_ref[0][5:6]
    half = D_MODEL // 2
    for j in range(PACK_TILES):
        lo = jnp.zeros((tt, LANES), F32)
        hi = jnp.zeros((tt, LANES), F32)
        for k in range(TOP_K):
            w = buf[pl.ds(k * tt * PACK_TILES + j, tt, stride=PACK_TILES), :]
            lo = lo + gates[:, k:k + 1] * lax.bitcast_convert_type(w << 16, F32)
            hi = hi + gates[:, k:k + 1] * lax.bitcast_convert_type(w & -65536, F32)
        for off, acc in ((0, lo), (half, hi)):
            cols = slice(off + j * LANES, off + (j + 1) * LANES)
            o_ref[0, :, cols] = base_ref[0, :, cols] + g2[:, cols] * acc


def _combine(dest, gates_tk, base, mod3, y, tt):
    B, S, D = base.shape
    ns = S // tt
    return pl.pallas_call(
        _combine_kernel,
        grid=(B, ns),
        in_specs=[pl.BlockSpec((TOP_K, tt), lambda b, i: (0, b * ns + i), memory_space=pltpu.SMEM),
                  pl.BlockSpec((tt, TOP_K), lambda b, i: (b * ns + i, 0)),
                  pl.BlockSpec((1, tt, D), lambda b, i: (b, i, 0)),
                  pl.BlockSpec((1, 6, D), lambda b, i: (b, 0, 0)),
                  pl.BlockSpec(memory_space=pl.ANY)],
        out_specs=pl.BlockSpec((1, tt, D), lambda b, i: (b, i, 0)),
        out_shape=jax.ShapeDtypeStruct((B, S, D), F32),
        scratch_shapes=[pltpu.VMEM((TOP_K * tt * PACK_TILES, LANES), I32),
                        pltpu.SemaphoreType.DMA(())],
        compiler_params=_params(("arbitrary", "arbitrary")),
        name="combine",
    )(dest, gates_tk, base, mod3, y)


def _alibi_q_features(n_heads, maps_per_head):
    slopes = 2.0 ** (-8.0 * jnp.arange(1, n_heads + 1, dtype=F32) / n_heads)
    c = jnp.repeat(slopes, maps_per_head) * LOG2E * POS_RADIX
    pieces = []
    rest = c
    for _ in range(N_SPLIT):
        p = rest.astype(BF16).astype(F32)
        pieces.append(p)
        rest = rest - p
    hi = jnp.stack(pieces, axis=1)
    feat = jnp.concatenate([hi, hi / POS_RADIX], axis=1)
    return jnp.pad(feat, ((0, 0), (HEAD, LANES - HEAD - 2 * N_SPLIT)))


def kernel(x, c, ada_w, ada_b, norm1_g, norm2_g, w_in, diff_q_norm_g, diff_k_norm_g, lam_q1, lam_k1, lam_q2, lam_k2, diff_subln_g, dsa_q_norm_g, dsa_k_norm_g, idx_k_norm_g, w_out, router_w, router_bias, exp_w_gate, exp_w_up, exp_w_down, shared_w_gate, shared_w_up, shared_w_down):
    B, S, D = x.shape
    assert D == D_MODEL and ada_w.shape[0] == 1 and S <= POS_RADIX * 128
    T = B * S
    topk = min(DSA_TOPK, S // 4)
    tm = min(512, S)
    tq_diff = min(512, S)
    tq_dsa = min(256, S)
    tk_dsa = min(512, S)
    rs_dsa = min(128, tq_dsa)
    tt_route = min(512, T)
    tt_move = min(256, S)
    blk = 512
    n_blocks = (T * TOP_K) // blk + N_EXPERTS
    nb_pad = -(-n_blocks // LANES) * LANES

    n_main = 7 * SEG
    wm = w_in[0, :, :n_main].astype(BF16)
    wt = jnp.pad(w_in[0, :, n_main:], ((0, 0), (0, LANES - (IDX_DIM + IDX_HEADS)))).astype(BF16)
    tile8 = lambda g: jnp.tile(g[0], SEG // g.shape[1]).reshape(1, SEG)
    gik = jnp.pad(idx_k_norm_g[0], (0, LANES - IDX_DIM)).reshape(1, LANES)
    lamv = jnp.concatenate([lam_q1, lam_k1, lam_q2, lam_k2], axis=0)
    wo1 = w_out[0, :SEG].astype(BF16)
    wo2 = w_out[0, SEG:].astype(BF16)
    rwt = router_w[0].T.astype(BF16)
    wg, wu, wd = exp_w_gate[0], exp_w_up[0], exp_w_down[0]

    mod3 = _ada(c, ada_w[0], ada_b[0]).reshape(B, 6, D)

    dq, dk, dv, sq, sk, sv, iq, ikl, ikh, iw = _inproj(
        x, mod3, norm1_g, wm, wt, tile8(diff_q_norm_g), tile8(diff_k_norm_g),
        tile8(dsa_q_norm_g), tile8(dsa_k_norm_g), gik,
        _alibi_q_features(DIFF_HEADS, 2), _alibi_q_features(DSA_HEADS, 1), tm)

    diff_out = _diff_attention(dq, dk, dv, lamv, diff_subln_g, tq_diff)
    dsa_out = _dsa_attention(iq, iw, ikl, ikh, sq, sk, sv, tq_dsa, tk_dsa, rs_dsa, topk)

    base, h2rows, logits_t = _mix(diff_out, dsa_out, x, mod3, norm2_g, wo1, wo2,
                                  shared_w_gate[0].astype(BF16), shared_w_up[0].astype(BF16),
                                  shared_w_down[0].astype(BF16), rwt, tm)

    eidx, gates = _route(logits_t, router_bias[0].reshape(N_EXPERTS, 1), tt_route)
    dest, bexp, nused = _plan(eidx, tt_route, blk, nb_pad)

    xs = _dispatch(dest, h2rows, jnp.zeros((n_blocks * blk * PACK_TILES, LANES), I32), tt_move)
    y = _experts(bexp.reshape(nb_pad), nused[0, :1], xs, wg, wu, wd, blk, n_blocks)
    return _combine(dest, gates.T, base, mod3, y, tt_move)
```

```python
import functools
import math

import jax
import jax.numpy as jnp
from jax import lax
from jax.experimental import pallas as pl
from jax.experimental.pallas import tpu as pltpu

F32 = jnp.float32
BF16 = jnp.bfloat16
I32 = jnp.int32

D_MODEL = 1024
DIFF_HEADS = 4
DIFF_HEAD_DIM = 64
DSA_HEADS = 8
DSA_HEAD_DIM = 64
IDX_HEADS = 8
IDX_DIM = 64
DSA_TOPK = 256
N_EXPERTS = 256
TOP_K = 8
N_GROUPS = 8
GROUP_SIZE = N_EXPERTS // N_GROUPS
TOPK_GROUPS = 4
D_EXPERT = 256
D_SHARED = 256
ROUTED_SCALE = 2.5
EPS = 1e-6
LAM_INIT = 0.2

LANES = 128
PACK_TILES = D_MODEL // (2 * LANES)
SEG = 512
HEAD = 64
N_MAPS = SEG // HEAD
WIDE = N_MAPS * LANES
POS_RADIX = 64
N_SPLIT = 3
NEG_BIG = -1e30
INT_MIN = -2147483648
LOG2E = math.log2(math.e)
VMEM_LIMIT = 56 * 1024 * 1024

NT_DIMS = (((1,), (1,)), ((), ()))


def _nt(a, b):
    return lax.dot_general(a, b, NT_DIMS, preferred_element_type=F32)


def _dot(a, b):
    return jnp.dot(a, b, preferred_element_type=F32)


def _rep(x, reps):
    return jnp.concatenate([x] * reps, axis=1)


def _params(sem, vmem=VMEM_LIMIT, **kw):
    return pltpu.CompilerParams(dimension_semantics=sem, vmem_limit_bytes=vmem, **kw)


def _ada_kernel(c_ref, w_ref, b_ref, o_ref):
    c = c_ref[...]
    s = c / (1.0 + jnp.exp(-c))
    o_ref[...] = jnp.dot(s, w_ref[...], preferred_element_type=F32,
                         precision=lax.Precision.HIGHEST) + b_ref[...]


def _ada(c, w, b):
    B, D = c.shape
    N = w.shape[1]
    tn = D
    return pl.pallas_call(
        _ada_kernel,
        grid=(N // tn,),
        in_specs=[pl.BlockSpec((B, D), lambda j: (0, 0)),
                  pl.BlockSpec((D, tn), lambda j: (0, j)),
                  pl.BlockSpec((1, tn), lambda j: (0, j))],
        out_specs=pl.BlockSpec((B, tn), lambda j: (0, j)),
        out_shape=jax.ShapeDtypeStruct((B, N), F32),
        compiler_params=_params(("arbitrary",)),
        name="ada",
    )(c, w, b.reshape(1, N))


def _group_sumsq(z):
    n = z.shape[1]
    r = lax.broadcasted_iota(I32, (n, n), 0) // HEAD
    c = lax.broadcasted_iota(I32, (n, n), 1) // HEAD
    bd = jnp.where(r == c, 1.0, 0.0).astype(BF16)
    zz = z * z
    hi = zz.astype(BF16)
    lo = (zz - hi.astype(F32)).astype(BF16)
    return _dot(hi, bd) + _dot(lo, bd)


def _inproj_kernel(x_ref, mod_ref, g1_ref, wm_ref, wt_ref, gq_ref, gk_ref, gsq_ref, gsk_ref, gik_ref,
                   fdq_ref, fsq_ref,
                   dq_ref, dk_ref, dv_ref, sq_ref, sk_ref, sv_ref, iq_ref, ikl_ref, ikh_ref, iw_ref):
    x = x_ref[0]
    tm = x.shape[0]
    ms = jnp.mean(x * x, axis=-1, keepdims=True)
    y = x * lax.rsqrt(ms + EPS) * g1_ref[...]
    mod = mod_ref[0]
    h = y * (1.0 + mod[1:2]) + mod[0:1]
    hb = h.astype(BF16)

    lane = lax.broadcasted_iota(I32, (tm, LANES), 1)
    is_head = lane < HEAD
    kpos = pl.program_id(1) * tm + lax.broadcasted_iota(I32, (tm, LANES), 0)
    hi_digit = (kpos // POS_RADIX).astype(F32)
    lo_digit = (kpos % POS_RADIX).astype(F32)
    kfeat = jnp.where(lane < HEAD + N_SPLIT, hi_digit, jnp.where(lane < HEAD + 2 * N_SPLIT, lo_digit, 0.0))

    def plain(seg_idx, out_ref):
        out_ref[0] = _dot(hb, wm_ref[:, seg_idx * SEG:(seg_idx + 1) * SEG]).astype(BF16)

    def normed(seg_idx, g_ref, scale, feat_ref, out_ref):
        half = SEG // 2
        for i in range(2):
            lo = seg_idx * SEG + i * half
            z = _dot(hb, wm_ref[:, lo:lo + half])
            ss = _group_sumsq(z)
            zn = z * lax.rsqrt(ss * (1.0 / HEAD) + EPS) * (g_ref[:, i * half:(i + 1) * half] * scale)
            for g in range(half // LANES):
                zg = zn[:, g * LANES:(g + 1) * LANES]
                for odd in range(2):
                    idx = i * (half // HEAD) + 2 * g + odd
                    src = zg if odd == 0 else pltpu.roll(zg, HEAD, 1)
                    feat = kfeat if feat_ref is None else feat_ref[idx:idx + 1, :]
                    out_ref[0, :, idx * LANES:(idx + 1) * LANES] = jnp.where(is_head, src, feat).astype(BF16)

    normed(0, gq_ref, HEAD ** -0.5 * LOG2E, fdq_ref, dq_ref)
    normed(1, gk_ref, 1.0, None, dk_ref)
    plain(2, dv_ref)
    normed(3, gsq_ref, HEAD ** -0.5 * LOG2E, fsq_ref, sq_ref)
    normed(4, gsk_ref, 1.0, None, sk_ref)
    plain(5, sv_ref)
    plain(6, iq_ref)

    t = _dot(hb, wt_ref[...])
    ikraw = jnp.where(lane < IDX_DIM, t, 0.0)
    ss = jnp.sum(ikraw * ikraw, axis=-1, keepdims=True) * (1.0 / IDX_DIM)
    ikn = ikraw * lax.rsqrt(ss + EPS) * gik_ref[...]
    ikl_ref[0] = ikn.astype(BF16)
    ikh_ref[0] = pltpu.roll(ikn, IDX_DIM, 1).astype(BF16)
    iwraw = jnp.where((lane >= IDX_DIM) & (lane < IDX_DIM + IDX_HEADS), t, 0.0)
    iw_ref[0] = pltpu.roll(iwraw * (IDX_HEADS ** -0.5), LANES - IDX_DIM, 1) * (IDX_DIM ** -0.5)


def _inproj(x, mod3, g1, wm, wt, gq, gk, gsq, gsk, gik, fdq, fsq, tm):
    B, S, D = x.shape
    ns = S // tm
    tok = lambda b, i: (b, i, 0)
    const2 = lambda b, i: (0, 0)
    seg_spec = pl.BlockSpec((1, tm, SEG), tok)
    wide_spec = pl.BlockSpec((1, tm, WIDE), tok)
    lane_spec = pl.BlockSpec((1, tm, LANES), tok)
    seg_shape = jax.ShapeDtypeStruct((B, S, SEG), BF16)
    wide_shape = jax.ShapeDtypeStruct((B, S, WIDE), BF16)
    return pl.pallas_call(
        _inproj_kernel,
        grid=(B, ns),
        in_specs=[pl.BlockSpec((1, tm, D), tok),
                  pl.BlockSpec((1, 6, D), lambda b, i: (b, 0, 0)),
                  pl.BlockSpec((1, D), const2),
                  pl.BlockSpec(wm.shape, const2),
                  pl.BlockSpec(wt.shape, const2),
                  pl.BlockSpec((1, SEG), const2), pl.BlockSpec((1, SEG), const2),
                  pl.BlockSpec((1, SEG), const2), pl.BlockSpec((1, SEG), const2),
                  pl.BlockSpec((1, LANES), const2),
                  pl.BlockSpec((N_MAPS, LANES), const2), pl.BlockSpec((N_MAPS, LANES), const2)],
        out_specs=[wide_spec, wide_spec, seg_spec, wide_spec, wide_spec, seg_spec, seg_spec,
                   lane_spec, lane_spec, lane_spec],
        out_shape=[wide_shape, wide_shape, seg_shape, wide_shape, wide_shape, seg_shape, seg_shape,
                   jax.ShapeDtypeStruct((B, S, LANES), BF16), jax.ShapeDtypeStruct((B, S, LANES), BF16),
                   jax.ShapeDtypeStruct((B, S, LANES), F32)],
        compiler_params=_params(("parallel", "parallel")),
        name="inproj",
    )(x, mod3, g1, wm, wt, gq, gk, gsq, gsk, gik, fdq, fsq)


def _diff_kernel(q_ref, k_ref, v_ref, lam_ref, g_ref, o_ref, s_ref, m_ref, l_ref, acc_ref, *, tq):
    qi = pl.program_id(1)
    ki = pl.program_id(2)
    n_maps = 2 * DIFF_HEADS
    reps = tq // LANES

    @pl.when(ki == 0)
    def _():
        m_ref[...] = jnp.full(m_ref.shape, NEG_BIG, F32)
        l_ref[...] = jnp.zeros(l_ref.shape, F32)
        acc_ref[...] = jnp.zeros(acc_ref.shape, F32)

    def step(diag):
        if diag:
            row = lax.broadcasted_iota(I32, (tq, tq), 0)
            col = lax.broadcasted_iota(I32, (tq, tq), 1)
            causal_bias = jnp.where(col <= row, 0.0, NEG_BIG)
        m_olds, m_news = [], []
        for idx in range(n_maps):
            s = _nt(q_ref[0, :, idx * LANES:(idx + 1) * LANES], k_ref[0, :, idx * LANES:(idx + 1) * LANES])
            if diag:
                s = s + causal_bias
            s_ref[idx] = s
            m_old = m_ref[idx]
            m_olds.append(m_old)
            m_news.append(jnp.maximum(m_old, jnp.max(s, axis=-1, keepdims=True)))
        for idx in range(n_maps):
            h = idx // 2
            p = jnp.exp2(s_ref[idx] - _rep(m_news[idx], reps))
            alpha = jnp.exp2(m_olds[idx] - m_news[idx])
            psum = p[:, 0:LANES]
            for g in range(1, reps):
                psum = psum + p[:, g * LANES:(g + 1) * LANES]
            l_ref[idx] = alpha * l_ref[idx] + psum
            acc_ref[idx] = alpha * acc_ref[idx] + _dot(p.astype(BF16), v_ref[0, :, h * LANES:(h + 1) * LANES])
            m_ref[idx] = m_news[idx]

    @pl.when(ki < qi)
    def _():
        step(False)

    @pl.when(ki == qi)
    def _():
        step(True)
        lv = lam_ref[...]
        lam = (jnp.exp(jnp.sum(lv[0:1] * lv[1:2], axis=-1, keepdims=True))
               - jnp.exp(jnp.sum(lv[2:3] * lv[3:4], axis=-1, keepdims=True)) + LAM_INIT)
        for h in range(DIFF_HEADS):
            o1 = acc_ref[2 * h] / jnp.sum(l_ref[2 * h], axis=-1, keepdims=True)
            o2 = acc_ref[2 * h + 1] / jnp.sum(l_ref[2 * h + 1], axis=-1, keepdims=True)
            o = o1 - lam * o2
            ms = jnp.mean(o * o, axis=-1, keepdims=True)
            on = o * lax.rsqrt(ms + EPS) * g_ref[...]
            o_ref[0, :, h * LANES:(h + 1) * LANES] = (on * (1.0 - LAM_INIT)).astype(BF16)


def _diff_attention(dq, dk, dv, lamv, subln_g, tq):
    B, S, W = dq.shape
    V = dv.shape[2]
    nq = S // tq
    n_maps = 2 * DIFF_HEADS
    return pl.pallas_call(
        functools.partial(_diff_kernel, tq=tq),
        grid=(B, nq, nq),
        in_specs=[pl.BlockSpec((1, tq, W), lambda b, i, j: (b, i, 0)),
                  pl.BlockSpec((1, tq, W), lambda b, i, j: (b, jnp.minimum(i, j), 0)),
                  pl.BlockSpec((1, tq, V), lambda b, i, j: (b, jnp.minimum(i, j), 0)),
                  pl.BlockSpec(lamv.shape, lambda b, i, j: (0, 0)),
                  pl.BlockSpec((1, LANES), lambda b, i, j: (0, 0))],
        out_specs=pl.BlockSpec((1, tq, V), lambda b, i, j: (b, i, 0)),
        out_shape=jax.ShapeDtypeStruct((B, S, V), BF16),
        scratch_shapes=[pltpu.VMEM((n_maps, tq, tq), F32),
                        pltpu.VMEM((n_maps, tq, LANES), F32),
                        pltpu.VMEM((n_maps, tq, LANES), F32),
                        pltpu.VMEM((n_maps, tq, LANES), F32)],
        compiler_params=_params(("parallel", "parallel", "arbitrary")),
        name="diff_attention",
    )(dq, dk, dv, lamv, subln_g)


def _score_key(v):
    bits = lax.bitcast_convert_type(v, I32)
    return bits ^ ((bits >> 31) & 0x7FFFFFFF)


def _key_score(k):
    return lax.bitcast_convert_type(k ^ ((k >> 31) & 0x7FFFFFFF), F32)


_SUM_LANE = (LANES - 1, 0)
N_CAND = 12
CAND_ROWS = 16


def _dsa_kernel(iq_ref, iw_ref, ikl_ref, ikh_ref, q_ref, k_ref, v_ref, o_ref,
                key_ref, cand_ref, thr_ref, nties_ref, nge_ref, mb_ref, s_ref, m_ref, acc_ref, *, tq, tk, rs, topk):
    qi = pl.program_id(1)
    q0 = qi * tq
    nkc = (q0 + tq + tk - 1) // tk
    def causal(k0):
        row = q0 + lax.broadcasted_iota(I32, (tq, tk), 0)
        return lax.broadcasted_iota(I32, (tq, tk), 1) + k0 <= row

    low =lax.broadcasted_iota(I32, (tq, LANES), 1) < HEAD
    reps = tk // LANES
    iw = iw_ref[0]

    def score_chunk(kc, carry):
        k0 = pl.multiple_of(kc * tk, tk)
        ikl = ikl_ref[0, pl.ds(k0, tk), :]
        ikh = ikh_ref[0, pl.ds(k0, tk), :]
        sc = jnp.zeros((tq, tk), F32)
        for j in range(IDX_HEADS // 2):
            iqp = iq_ref[0, :, j * LANES:(j + 1) * LANES]
            sc = sc + iw[:, 2 * j:2 * j + 1] * jnp.maximum(_nt(iqp, ikl), 0.0)
            sc = sc + iw[:, 2 * j + 1:2 * j + 2] * jnp.maximum(_nt(iqp, ikh), 0.0)
        sc = jnp.where(sc == 0.0, 0.0, sc)
        sc = jnp.where(causal(k0), sc, -jnp.inf)
        key_ref[:, pl.ds(k0, tk)] = _score_key(sc)
        return carry

    lax.fori_loop(0, nkc, score_chunk, 0)

    groups = [slice(r0, r0 + rs) for r0 in range(0, tq, rs)]
    kf = float(topk)

    def candidates(g, carry):
        rows = pl.ds(pl.multiple_of(g * CAND_ROWS, CAND_ROWS), CAND_ROWS)

        def insert_chunk(kc, best):
            k0 = pl.multiple_of(kc * tk, tk)
            x = _key_score(key_ref[rows, pl.ds(k0, tk)])
            best = list(best)
            for lg in range(reps):
                v = x[:, lg * LANES:(lg + 1) * LANES]
                for i in range(N_CAND):
                    best[i], v = jnp.maximum(best[i], v), jnp.minimum(best[i], v)
            return tuple(best)

        lowest = jnp.full((CAND_ROWS, LANES), -jnp.inf, F32)
        best = lax.fori_loop(0, nkc, insert_chunk, (lowest,) * N_CAND)
        for i in range(N_CAND):
            cand_ref[rows, i * LANES:(i + 1) * LANES] = _score_key(best[i])
        return carry

    lax.fori_loop(0, tq // CAND_ROWS, candidates, 0)

    def count(ref, n_chunks, thrs, strict):
        accs = []
        for rows, thr in zip(groups, thrs):
            thr_t = _rep(thr, reps)

            def body(kc, acc, rows=rows, thr_t=thr_t):
                k0 = kc * tk if isinstance(kc, int) else pl.multiple_of(kc * tk, tk)
                keyc = ref[rows, pl.ds(k0, tk)]
                hit = jnp.where((keyc > thr_t) if strict else (keyc >= thr_t), 1.0, 0.0)
                for g in range(reps):
                    acc = acc + hit[:, g * LANES:(g + 1) * LANES]
                return acc

            acc = jnp.zeros((rs, LANES), F32)
            if isinstance(n_chunks, int):
                for kc in range(n_chunks):
                    acc = body(kc, acc)
            else:
                acc = lax.fori_loop(0, n_chunks, body, acc)
            accs.append(acc)
        return [jnp.broadcast_to(jnp.sum(acc, axis=-1, keepdims=True), (rs, LANES)) for acc in accs]

    def search(ref, n_chunks):
        def bit_step(i, tus):
            bit = jnp.left_shift(jnp.int32(1), 31 - i)
            cands = [tu | bit for tu in tus]
            cnts = count(ref, n_chunks, [c ^ INT_MIN for c in cands], False)
            return tuple(jnp.where(cnt >= kf, c, tu) for cnt, c, tu in zip(cnts, cands, tus))

        tus = lax.fori_loop(0, 32, bit_step, tuple(jnp.zeros((rs, LANES), I32) for _ in groups))
        return [tu ^ INT_MIN for tu in tus]

    def publish(thrs, n_above, n_reach):
        for rows, thr, n_gt, n_ge in zip(groups, thrs, n_above, n_reach):
            thr_ref[rows, :] = thr
            nties_ref[rows, :] = kf - n_gt
            nge_ref[rows, :] = n_ge

    cand_chunks = N_CAND * LANES // tk
    thrs = search(cand_ref, cand_chunks)
    inside = count(cand_ref, cand_chunks, thrs, False)
    publish(thrs, count(cand_ref, cand_chunks, thrs, True), inside)
    overall = count(key_ref, nkc, thrs, False)
    missed = [jnp.max(jnp.where(a != b, 1.0, 0.0)) for a, b in zip(inside, overall)]

    @pl.when(functools.reduce(jnp.maximum, missed) > 0.0)
    def _():
        full = search(key_ref, nkc)
        publish(full, count(key_ref, nkc, full, True), count(key_ref, nkc, full, False))

    m_ref[...] = jnp.full(m_ref.shape, NEG_BIG, F32)
    acc_ref[...] = jnp.zeros(acc_ref.shape, F32)
    tie_rows = jnp.max(jnp.where(nge_ref[...] != kf, 1.0, 0.0))

    def attend_chunk(kc, ties_before, ranked):
        k0 = pl.multiple_of(kc * tk, tk)
        keyc = key_ref[:, pl.ds(k0, tk)]
        thr_t = _rep(thr_ref[...], reps)
        if ranked:
            tri = jnp.where(lax.broadcasted_iota(I32, (tk, tk), 0) < lax.broadcasted_iota(I32, (tk, tk), 1),
                            1.0, 0.0).astype(BF16)
            eq = jnp.where(keyc == thr_t, 1.0, 0.0)
            rank = _dot(eq.astype(BF16), tri) + _rep(ties_before, reps)
            take = jnp.where(keyc > thr_t, 1.0, jnp.where(rank < _rep(nties_ref[...], reps), eq, 0.0))
            psum = eq[:, 0:LANES]
            for g in range(1, reps):
                psum = psum + eq[:, g * LANES:(g + 1) * LANES]
            ties_before = ties_before + jnp.sum(psum, axis=-1, keepdims=True)
            mb_ref[...] = jnp.where(causal(k0), jnp.where(take > 0.5, 0.0, NEG_BIG), NEG_BIG)
        else:
            mb_ref[...] = jnp.where(keyc >= thr_t, 0.0, NEG_BIG)
        mb = mb_ref[...]
        m_olds, m_news = [], []
        for hd in range(DSA_HEADS):
            s = _nt(q_ref[0, :, hd * LANES:(hd + 1) * LANES],
                    k_ref[0, pl.ds(k0, tk), hd * LANES:(hd + 1) * LANES]) + mb
            s_ref[hd] = s
            m_old = m_ref[hd]
            m_olds.append(m_old)
            m_news.append(jnp.maximum(m_old, jnp.max(s, axis=-1, keepdims=True)))
        for j in range(DSA_HEADS // 2):
            vp = v_ref[0, pl.ds(k0, tk), j * LANES:(j + 1) * LANES]
            vlane = lax.broadcasted_iota(I32, vp.shape, 1)
            for a in range(2):
                hd = 2 * j + a
                keep = (vlane < HEAD) if a == 0 else (vlane >= HEAD)
                va = jnp.where(vlane == _SUM_LANE[a], jnp.ones_like(vp), jnp.where(keep, vp, jnp.zeros_like(vp)))
                p = jnp.exp2((s_ref[hd] - _rep(m_news[hd], reps)).astype(BF16))
                acc_ref[hd] = jnp.exp2(m_olds[hd] - m_news[hd]) * acc_ref[hd] + _dot(p, va)
                m_ref[hd] = m_news[hd]
        return ties_before

    for ranked in (False, True):
        @pl.when((tie_rows > 0.0) == ranked)
        def _(ranked=ranked):
            lax.fori_loop(0, nkc, functools.partial(attend_chunk, ranked=ranked), jnp.zeros((tq, LANES), F32))

    lane = lax.broadcasted_iota(I32, (tq, LANES), 1)
    for j in range(DSA_HEADS // 2):
        oa, ob = acc_ref[2 * j], acc_ref[2 * j + 1]
        la = jnp.sum(jnp.where(lane == _SUM_LANE[0], oa, 0.0), axis=-1, keepdims=True)
        lb = jnp.sum(jnp.where(lane == _SUM_LANE[1], ob, 0.0), axis=-1, keepdims=True)
        o_ref[0, :, j * LANES:(j + 1) * LANES] = jnp.where(low, oa / la, ob / lb).astype(BF16)


def _dsa_attention(iq, iw, ikl, ikh, sq, sk, sv, tq, tk, rs, topk):
    B, S, W = sq.shape
    V = sv.shape[2]
    nq = S // tq
    assert (N_CAND * LANES) % tk == 0 and tq % CAND_ROWS == 0 and tq % rs == 0 and tk > topk
    qblk = lambda b, i: (b, i, 0)
    full = lambda b, i: (b, 0, 0)
    once = pl.Buffered(1)
    return pl.pallas_call(
        functools.partial(_dsa_kernel, tq=tq, tk=tk, rs=rs, topk=topk),
        grid=(B, nq),
        in_specs=[pl.BlockSpec((1, tq, iq.shape[2]), qblk),
                  pl.BlockSpec((1, tq, LANES), qblk),
                  pl.BlockSpec((1, S, LANES), full, pipeline_mode=once),
                  pl.BlockSpec((1, S, LANES), full, pipeline_mode=once),
                  pl.BlockSpec((1, tq, W), qblk),
                  pl.BlockSpec((1, S, W), full, pipeline_mode=once),
                  pl.BlockSpec((1, S, V), full, pipeline_mode=once)],
        out_specs=pl.BlockSpec((1, tq, V), qblk),
        out_shape=jax.ShapeDtypeStruct((B, S, V), BF16),
        scratch_shapes=[pltpu.VMEM((tq, S), I32),
                        pltpu.VMEM((tq, N_CAND * LANES), I32),
                        pltpu.VMEM((tq, LANES), I32),
                        pltpu.VMEM((tq, LANES), F32),
                        pltpu.VMEM((tq, LANES), F32),
                        pltpu.VMEM((tq, tk), F32),
                        pltpu.VMEM((DSA_HEADS, tq, tk), F32),
                        pltpu.VMEM((DSA_HEADS, tq, LANES), F32),
                        pltpu.VMEM((DSA_HEADS, tq, LANES), F32)],
        compiler_params=_params(("parallel", "arbitrary")),
        name="dsa_attention",
    )(iq, iw, ikl, ikh, sq, sk, sv)


def _mix_kernel(do_ref, so_ref, x_ref, mod_ref, g2_ref, wo1_ref, wo2_ref, wsg_ref, wsu_ref, wsd_ref,
                rwt_ref, base_ref, h2_ref, lg_ref):
    mix = _dot(do_ref[0], wo1_ref[...]) + _dot(so_ref[0], wo2_ref[...])
    mod = mod_ref[0]
    x1 = x_ref[0] + mod[2:3] * mix
    ms = jnp.mean(x1 * x1, axis=-1, keepdims=True)
    h2 = x1 * lax.rsqrt(ms + EPS) * g2_ref[...] * (1.0 + mod[4:5]) + mod[3:4]
    hb = h2.astype(BF16)
    gate = _dot(hb, wsg_ref[...])
    up = _dot(hb, wsu_ref[...])
    act = gate / (1.0 + jnp.exp(-gate)) * up
    shared = _dot(act.astype(BF16), wsd_ref[...])
    base_ref[0] = x1 + mod[5:6] * shared
    lg_ref[...] = _nt(rwt_ref[...], hb)
    hf = hb.astype(F32)
    tm = hf.shape[0]
    bits = lax.bitcast_convert_type(hf, I32)
    half = D_MODEL // 2
    packed = (bits[:, half:] & -65536) | lax.shift_right_logical(bits[:, :half], 16)
    for j in range(PACK_TILES):
        h2_ref[pl.ds(j, tm, stride=PACK_TILES), :] = packed[:, j * LANES:(j + 1) * LANES]


def _mix(diff_out, dsa_out, x, mod3, g2, wo1, wo2, wsg, wsu, wsd, rwt, tm):
    B, S, D = x.shape
    ns = S // tm
    T = B * S
    tok = lambda b, i: (b, i, 0)
    c2 = lambda b, i: (0, 0)
    return pl.pallas_call(
        _mix_kernel,
        grid=(B, ns),
        in_specs=[pl.BlockSpec((1, tm, SEG), tok), pl.BlockSpec((1, tm, SEG), tok),
                  pl.BlockSpec((1, tm, D), tok),
                  pl.BlockSpec((1, 6, D), lambda b, i: (b, 0, 0)),
                  pl.BlockSpec((1, D), c2),
                  pl.BlockSpec(wo1.shape, c2), pl.BlockSpec(wo2.shape, c2),
                  pl.BlockSpec(wsg.shape, c2), pl.BlockSpec(wsu.shape, c2), pl.BlockSpec(wsd.shape, c2),
                  pl.BlockSpec(rwt.shape, c2)],
        out_specs=[pl.BlockSpec((1, tm, D), tok),
                   pl.BlockSpec((tm * PACK_TILES, LANES), lambda b, i: (b * ns + i, 0)),
                   pl.BlockSpec((N_EXPERTS, tm), lambda b, i: (0, b * ns + i))],
        out_shape=[jax.ShapeDtypeStruct((B, S, D), F32),
                   jax.ShapeDtypeStruct((T * PACK_TILES, LANES), I32),
                   jax.ShapeDtypeStruct((N_EXPERTS, T), F32)],
        compiler_params=_params(("parallel", "parallel")),
        name="mix_shared_router",
    )(diff_out, dsa_out, x, mod3, g2, wo1, wo2, wsg, wsu, wsd, rwt)


def _first_max(v, idx, sentinel):
    m = jnp.max(v, axis=0, keepdims=True)
    i = jnp.min(jnp.where(v == m, idx, sentinel), axis=0, keepdims=True)
    return m, i


def _route_kernel(lg_ref, bias_ref, eidx_ref, gate_ref):
    lg = lg_ref[...]
    tt = lg.shape[1]
    scores = 1.0 / (1.0 + jnp.exp(-lg))
    biased = scores + bias_ref[...]
    gi = lax.broadcasted_iota(I32, (GROUP_SIZE, tt), 0).astype(F32)
    gscore = []
    for g in range(N_GROUPS):
        blk = biased[g * GROUP_SIZE:(g + 1) * GROUP_SIZE, :]
        m1, i1 = _first_max(blk, gi, float(GROUP_SIZE))
        m2 = jnp.max(jnp.where(gi == i1, -jnp.inf, blk), axis=0, keepdims=True)
        gscore.append(m1 + m2)
    gs = jnp.concatenate(gscore, axis=0)
    gidx = lax.broadcasted_iota(I32, (N_GROUPS, tt), 0).astype(F32)
    chosen = jnp.zeros((N_GROUPS, tt), F32)
    for _ in range(TOPK_GROUPS):
        _, ig = _first_max(gs, gidx, float(N_GROUPS))
        hit = gidx == ig
        chosen = jnp.where(hit, 1.0, chosen)
        gs = jnp.where(hit, -jnp.inf, gs)
    masked = jnp.concatenate(
        [jnp.where(chosen[g:g + 1, :] > 0.5, biased[g * GROUP_SIZE:(g + 1) * GROUP_SIZE, :], -jnp.inf)
         for g in range(N_GROUPS)], axis=0)
    ei = lax.broadcasted_iota(I32, (N_EXPERTS, tt), 0).astype(F32)
    ids, ws = [], []
    for _ in range(TOP_K):
        _, ie = _first_max(masked, ei, float(N_EXPERTS))
        hit = ei == ie
        ws.append(jnp.sum(jnp.where(hit, scores, 0.0), axis=0, keepdims=True))
        ids.append(ie)
        masked = jnp.where(hit, -jnp.inf, masked)
    w = jnp.concatenate(ws, axis=0)
    gate_ref[...] = w / jnp.sum(w, axis=0, keepdims=True) * ROUTED_SCALE
    eidx_ref[...] = jnp.concatenate(ids, axis=0).astype(I32)


def _route(logits_t, bias_col, tt):
    E, T = logits_t.shape
    return pl.pallas_call(
        _route_kernel,
        grid=(T // tt,),
        in_specs=[pl.BlockSpec((E, tt), lambda i: (0, i)),
                  pl.BlockSpec((E, 1), lambda i: (0, 0))],
        out_specs=[pl.BlockSpec((TOP_K, tt), lambda i: (0, i)),
                   pl.BlockSpec((TOP_K, tt), lambda i: (0, i))],
        out_shape=[jax.ShapeDtypeStruct((TOP_K, T), I32),
                   jax.ShapeDtypeStruct((TOP_K, T), F32)],
        compiler_params=_params(("parallel",)),
        name="route",
    )(logits_t, bias_col)


def _plan_kernel(eidx_ref, dest_ref, bexp_ref, nused_ref, cnt_col, cnt_row, slot_base, *, blk, nb_pad):
    ph = pl.program_id(0)
    i = pl.program_id(1)
    tt = eidx_ref.shape[1]
    eidx = eidx_ref[...]
    ei = lax.broadcasted_iota(I32, (N_EXPERTS, tt), 0)
    onehot = jnp.zeros((N_EXPERTS, tt), F32)
    for k in range(TOP_K):
        onehot = onehot + jnp.where(ei == eidx[k:k + 1, :], 1.0, 0.0)
    oh = onehot.astype(BF16)

    @pl.when((ph == 0) & (i == 0))
    def _():
        cnt_col[...] = jnp.zeros(cnt_col.shape, F32)
        cnt_row[...] = jnp.zeros(cnt_row.shape, F32)

    @pl.when(ph == 0)
    def _():
        cnt_col[...] += _dot(oh, jnp.ones((tt, LANES), BF16))
        cnt_row[...] += _nt(jnp.ones((8, tt), BF16), oh)

    @pl.when((ph == 1) & (i == 0))
    def _():
        inv = 1.0 / blk
        nb_col = jnp.floor((cnt_col[:, 0:1] + (blk - 1)) * inv)
        nb_row = jnp.floor((cnt_row[0:1, :] + (blk - 1)) * inv)
        r = lax.broadcasted_iota(I32, (N_EXPERTS, N_EXPERTS), 0)
        c = lax.broadcasted_iota(I32, (N_EXPERTS, N_EXPERTS), 1)
        bstart = jnp.sum(jnp.where(c < r, nb_row, 0.0), axis=-1, keepdims=True)
        bend = bstart + nb_col
        slot_base[...] = bstart * blk
        jb = lax.broadcasted_iota(I32, (N_EXPERTS, nb_pad), 1).astype(F32)
        be = jnp.sum(jnp.where(bend <= jb, 1.0, 0.0), axis=0, keepdims=True)
        bexp_ref[...] = jnp.minimum(be, N_EXPERTS - 1.0).astype(I32)
        nused_ref[...] = jnp.broadcast_to(jnp.sum(nb_row, axis=-1, keepdims=True), nused_ref.shape).astype(I32)

    @pl.when(ph == 1)
    def _():
        tri = jnp.where(lax.broadcasted_iota(I32, (tt, tt), 0) < lax.broadcasted_iota(I32, (tt, tt), 1),
                        1.0, 0.0).astype(BF16)
        slot = _dot(oh, tri) + slot_base[...]
        for k in range(TOP_K):
            dk = jnp.sum(jnp.where(ei == eidx[k:k + 1, :], slot, 0.0), axis=0, keepdims=True)
            dest_ref[k:k + 1, :] = dk.astype(I32)
        slot_base[...] += jnp.sum(onehot, axis=-1, keepdims=True)


def _plan(eidx, tt, blk, nb_pad):
    K, T = eidx.shape
    nt = T // tt
    return pl.pallas_call(
        functools.partial(_plan_kernel, blk=blk, nb_pad=nb_pad),
        grid=(2, nt),
        in_specs=[pl.BlockSpec((K, tt), lambda p, i: (0, i))],
        out_specs=[pl.BlockSpec((K, tt), lambda p, i: (0, i * p)),
                   pl.BlockSpec((1, nb_pad), lambda p, i: (0, 0)),
                   pl.BlockSpec((1, LANES), lambda p, i: (0, 0))],
        out_shape=[jax.ShapeDtypeStruct((K, T), I32),
                   jax.ShapeDtypeStruct((1, nb_pad), I32),
                   jax.ShapeDtypeStruct((1, LANES), I32)],
        scratch_shapes=[pltpu.VMEM((N_EXPERTS, LANES), F32),
                        pltpu.VMEM((8, N_EXPERTS), F32),
                        pltpu.VMEM((N_EXPERTS, 1), F32)],
        compiler_params=_params(("arbitrary", "arbitrary")),
        name="plan",
    )(eidx)


def _row(ref, r):
    return ref.at[pl.ds(pl.multiple_of(r * PACK_TILES, PACK_TILES), PACK_TILES), :]


def _dispatch_kernel(dest_ref, h_ref, xs_in_ref, xs_ref, sem):
    del xs_in_ref
    tt = h_ref.shape[0] // PACK_TILES

    def issue(t, c):
        for k in range(TOP_K):
            pltpu.make_async_copy(_row(h_ref, t), _row(xs_ref, dest_ref[k, t]), sem).start()
        return c

    lax.fori_loop(0, tt, issue, 0)

    def drain(t, c):
        for k in range(TOP_K):
            pltpu.make_async_copy(_row(h_ref, 0), _row(xs_ref, 0), sem).wait()
        return c

    lax.fori_loop(0, tt, drain, 0)


def _dispatch(dest, h2rows, xs_init, tt):
    T = h2rows.shape[0] // PACK_TILES
    return pl.pallas_call(
        _dispatch_kernel,
        grid=(T // tt,),
        in_specs=[pl.BlockSpec((TOP_K, tt), lambda i: (0, i), memory_space=pltpu.SMEM),
                  pl.BlockSpec((tt * PACK_TILES, LANES), lambda i: (i, 0)),
                  pl.BlockSpec(memory_space=pl.ANY)],
        out_specs=pl.BlockSpec(memory_space=pl.ANY),
        out_shape=jax.ShapeDtypeStruct(xs_init.shape, xs_init.dtype),
        scratch_shapes=[pltpu.SemaphoreType.DMA(())],
        input_output_aliases={2: 0},
        compiler_params=_params(("arbitrary",), has_side_effects=True),
        name="dispatch",
    )(dest, h2rows, xs_init)


def _experts_kernel(bexp_ref, nused_ref, xs_ref, wg_ref, wu_ref, wd_ref, y_ref, xb_ref, wgb_ref, wub_ref, wdb_ref):
    j = pl.program_id(0)

    @pl.when(j < nused_ref[0])
    def _():
        @pl.when((j == 0) | (bexp_ref[j] != bexp_ref[jnp.maximum(j - 1, 0)]))
        def _():
            wgb_ref[...] = wg_ref[0].astype(BF16)
            wub_ref[...] = wu_ref[0].astype(BF16)
            wdb_ref[...] = wd_ref[0].astype(BF16)

        blk = xb_ref.shape[0]
        half = D_MODEL // 2
        for c in range(PACK_TILES):
            w = xs_ref[pl.ds(c, blk, stride=PACK_TILES), :]
            cols = slice(c * LANES, (c + 1) * LANES)
            xb_ref[:, cols] = lax.bitcast_convert_type(w << 16, F32).astype(BF16)
            xb_ref[:, half + c * LANES:half + (c + 1) * LANES] = lax.bitcast_convert_type(w & -65536, F32).astype(BF16)
        xb = xb_ref[...]
        gate = _dot(xb, wgb_ref[...])
        up = _dot(xb, wub_ref[...])
        act = gate / (1.0 + jnp.exp(-gate)) * up
        y = _dot(act.astype(BF16), wdb_ref[...])
        bits = lax.bitcast_convert_type(y.astype(BF16).astype(F32), I32)
        packed = (bits[:, half:] & -65536) | lax.shift_right_logical(bits[:, :half], 16)
        for c in range(PACK_TILES):
            y_ref[pl.ds(c, blk, stride=PACK_TILES), :] = packed[:, c * LANES:(c + 1) * LANES]


def _experts(bexp, nused, xs, wg, wu, wd, blk, n_blocks):
    live = lambda j, be, nu: jnp.minimum(j, nu[0] - 1)
    row_spec = pl.BlockSpec((blk * PACK_TILES, LANES), lambda j, be, nu: (live(j, be, nu), 0))
    wspec = lambda w: pl.BlockSpec((1,) + w.shape[1:], lambda j, be, nu: (be[live(j, be, nu)], 0, 0))
    return pl.pallas_call(
        _experts_kernel,
        grid_spec=pltpu.PrefetchScalarGridSpec(
            num_scalar_prefetch=2,
            grid=(n_blocks,),
            in_specs=[row_spec, wspec(wg), wspec(wu), wspec(wd)],
            out_specs=row_spec,
            scratch_shapes=[pltpu.VMEM((blk, D_MODEL), BF16),
                            pltpu.VMEM(wg.shape[1:], BF16), pltpu.VMEM(wu.shape[1:], BF16),
                            pltpu.VMEM(wd.shape[1:], BF16)]),
        out_shape=jax.ShapeDtypeStruct(xs.shape, I32),
        compiler_params=_params(("arbitrary",)),
        name="experts",
    )(bexp, nused, xs, wg, wu, wd)


def _combine_kernel(dest_ref, gate_ref, base_ref, mod_ref, y_ref, o_ref, buf, sem):
    tt = base_ref.shape[1]

    def issue(t, c):
        for k in range(TOP_K):
            pltpu.make_async_copy(_row(y_ref, dest_ref[k, t]), _row(buf, k * tt + t), sem).start()
        return c

    lax.fori_loop(0, tt, issue, 0)

    def drain(t, c):
        for k in range(TOP_K):
            pltpu.make_async_copy(_row(y_ref, 0), _row(buf, 0), sem).wait()
        return c

    lax.fori_loop(0, tt, drain, 0)

    gates = gate_ref[...]
    g2 = mod_ref[0][5:6]
    half = D_MODEL // 2
    for j in range(PACK_TILES):
        lo = jnp.zeros((tt, LANES), F32)
        hi = jnp.zeros((tt, LANES), F32)
        for k in range(TOP_K):
            w = buf[pl.ds(k * tt * PACK_TILES + j, tt, stride=PACK_TILES), :]
            lo = lo + gates[:, k:k + 1] * lax.bitcast_convert_type(w << 16, F32)
            hi = hi + gates[:, k:k + 1] * lax.bitcast_convert_type(w & -65536, F32)
        for off, acc in ((0, lo), (half, hi)):
            cols = slice(off + j * LANES, off + (j + 1) * LANES)
            o_ref[0, :, cols] = base_ref[0, :, cols] + g2[:, cols] * acc


def _combine(dest, gates_tk, base, mod3, y, tt):
    B, S, D = base.shape
    ns = S // tt
    return pl.pallas_call(
        _combine_kernel,
        grid=(B, ns),
        in_specs=[pl.BlockSpec((TOP_K, tt), lambda b, i: (0, b * ns + i), memory_space=pltpu.SMEM),
                  pl.BlockSpec((tt, TOP_K), lambda b, i: (b * ns + i, 0)),
                  pl.BlockSpec((1, tt, D), lambda b, i: (b, i, 0)),
                  pl.BlockSpec((1, 6, D), lambda b, i: (b, 0, 0)),
                  pl.BlockSpec(memory_space=pl.ANY)],
        out_specs=pl.BlockSpec((1, tt, D), lambda b, i: (b, i, 0)),
        out_shape=jax.ShapeDtypeStruct((B, S, D), F32),
        scratch_shapes=[pltpu.VMEM((TOP_K * tt * PACK_TILES, LANES), I32),
                        pltpu.SemaphoreType.DMA(())],
        compiler_params=_params(("arbitrary", "arbitrary")),
        name="combine",
    )(dest, gates_tk, base, mod3, y)


def _alibi_q_features(n_heads, maps_per_head):
    slopes = 2.0 ** (-8.0 * jnp.arange(1, n_heads + 1, dtype=F32) / n_heads)
    c = jnp.repeat(slopes, maps_per_head) * LOG2E * POS_RADIX
    pieces = []
    rest = c
    for _ in range(N_SPLIT):
        p = rest.astype(BF16).astype(F32)
        pieces.append(p)
        rest = rest - p
    hi = jnp.stack(pieces, axis=1)
    feat = jnp.concatenate([hi, hi / POS_RADIX], axis=1)
    return jnp.pad(feat, ((0, 0), (HEAD, LANES - HEAD - 2 * N_SPLIT)))


def kernel(x, c, ada_w, ada_b, norm1_g, norm2_g, w_in, diff_q_norm_g, diff_k_norm_g, lam_q1, lam_k1, lam_q2, lam_k2, diff_subln_g, dsa_q_norm_g, dsa_k_norm_g, idx_k_norm_g, w_out, router_w, router_bias, exp_w_gate, exp_w_up, exp_w_down, shared_w_gate, shared_w_up, shared_w_down):
    B, S, D = x.shape
    assert D == D_MODEL and ada_w.shape[0] == 1 and S <= POS_RADIX * 128
    T = B * S
    topk = min(DSA_TOPK, S // 4)
    tm = min(512, S)
    tq_diff = min(512, S)
    tq_dsa = min(256, S)
    tk_dsa = min(512, S)
    rs_dsa = min(128, tq_dsa)
    tt_route = min(512, T)
    tt_move = min(256, S)
    blk = 512
    n_blocks = (T * TOP_K) // blk + N_EXPERTS
    nb_pad = -(-n_blocks // LANES) * LANES

    n_main = 7 * SEG
    wm = w_in[0, :, :n_main].astype(BF16)
    wt = jnp.pad(w_in[0, :, n_main:], ((0, 0), (0, LANES - (IDX_DIM + IDX_HEADS)))).astype(BF16)
    tile8 = lambda g: jnp.tile(g[0], SEG // g.shape[1]).reshape(1, SEG)
    gik = jnp.pad(idx_k_norm_g[0], (0, LANES - IDX_DIM)).reshape(1, LANES)
    lamv = jnp.concatenate([lam_q1, lam_k1, lam_q2, lam_k2], axis=0)
    wo1 = w_out[0, :SEG].astype(BF16)
    wo2 = w_out[0, SEG:].astype(BF16)
    rwt = router_w[0].T.astype(BF16)
    wg, wu, wd = exp_w_gate[0], exp_w_up[0], exp_w_down[0]

    mod3 = _ada(c, ada_w[0], ada_b[0]).reshape(B, 6, D)

    dq, dk, dv, sq, sk, sv, iq, ikl, ikh, iw = _inproj(
        x, mod3, norm1_g, wm, wt, tile8(diff_q_norm_g), tile8(diff_k_norm_g),
        tile8(dsa_q_norm_g), tile8(dsa_k_norm_g), gik,
        _alibi_q_features(DIFF_HEADS, 2), _alibi_q_features(DSA_HEADS, 1), tm)

    diff_out = _diff_attention(dq, dk, dv, lamv, diff_subln_g, tq_diff)
    dsa_out = _dsa_attention(iq, iw, ikl, ikh, sq, sk, sv, tq_dsa, tk_dsa, rs_dsa, topk)

    base, h2rows, logits_t = _mix(diff_out, dsa_out, x, mod3, norm2_g, wo1, wo2,
                                  shared_w_gate[0].astype(BF16), shared_w_up[0].astype(BF16),
                                  shared_w_down[0].astype(BF16), rwt, tm)

    eidx, gates = _route(logits_t, router_bias[0].reshape(N_EXPERTS, 1), tt_route)
    dest, bexp, nused = _plan(eidx, tt_route, blk, nb_pad)

    xs = _dispatch(dest, h2rows, jnp.zeros((n_blocks * blk * PACK_TILES, LANES), I32), tt_move)
    y = _experts(bexp.reshape(nb_pad), nused[0, :1], xs, wg, wu, wd, blk, n_blocks)
    return _combine(dest, gates.T, base, mod3, y, tt_move)
```

```python
import functools
import math

import jax
import jax.numpy as jnp
from jax import lax
from jax.experimental import pallas as pl
from jax.experimental.pallas import tpu as pltpu

F32 = jnp.float32
BF16 = jnp.bfloat16
I32 = jnp.int32

D_MODEL = 1024
DIFF_HEADS = 4
DSA_HEADS = 8
IDX_HEADS = 8
IDX_DIM = 64
DSA_TOPK = 256
N_EXPERTS = 256
TOP_K = 8
N_GROUPS = 8
GROUP_SIZE = N_EXPERTS // N_GROUPS
TOPK_GROUPS = 4
ROUTED_SCALE = 2.5
EPS = 1e-6
LAM_INIT = 0.2

LANES = 128
PACK_TILES = D_MODEL // (2 * LANES)
SEG = 512
HEAD = 64
N_MAPS = SEG // HEAD
WIDE = N_MAPS * LANES
POS_RADIX = 64
N_SPLIT = 3
NEG_BIG = -1e30
INT_MIN = -2147483648
LOG2E = math.log2(math.e)
VMEM_LIMIT = 56 * 1024 * 1024

NT_DIMS = (((1,), (1,)), ((), ()))


def _nt(a, b):
    return lax.dot_general(a, b, NT_DIMS, preferred_element_type=F32)


def _dot(a, b):
    return jnp.dot(a, b, preferred_element_type=F32)


def _rep(x, reps):
    return jnp.concatenate([x] * reps, axis=1)


def _params(sem, vmem=VMEM_LIMIT, **kw):
    return pltpu.CompilerParams(dimension_semantics=sem, vmem_limit_bytes=vmem, **kw)


def _ada_kernel(c_ref, w_ref, b_ref, o_ref):
    c = c_ref[...]
    s = c / (1.0 + jnp.exp(-c))
    o_ref[...] = jnp.dot(s, w_ref[...], preferred_element_type=F32,
                         precision=lax.Precision.HIGHEST) + b_ref[...]


def _ada(c, w, b):
    B, D = c.shape
    N = w.shape[1]
    tn = D
    return pl.pallas_call(
        _ada_kernel,
        grid=(N // tn,),
        in_specs=[pl.BlockSpec((B, D), lambda j: (0, 0)),
                  pl.BlockSpec((D, tn), lambda j: (0, j)),
                  pl.BlockSpec((1, tn), lambda j: (0, j))],
        out_specs=pl.BlockSpec((B, tn), lambda j: (0, j)),
        out_shape=jax.ShapeDtypeStruct((B, N), F32),
        compiler_params=_params(("arbitrary",)),
        name="ada",
    )(c, w, b.reshape(1, N))


def _group_sumsq(z):
    n = z.shape[1]
    r = lax.broadcasted_iota(I32, (n, n), 0) // HEAD
    c = lax.broadcasted_iota(I32, (n, n), 1) // HEAD
    bd = jnp.where(r == c, 1.0, 0.0).astype(BF16)
    zz = z * z
    hi = zz.astype(BF16)
    lo = (zz - hi.astype(F32)).astype(BF16)
    return _dot(hi, bd) + _dot(lo, bd)


def _inproj_kernel(x_ref, mod_ref, g1_ref, wm_ref, wt_ref, gq_ref, gk_ref, gsq_ref, gsk_ref, gik_ref,
                   fdq_ref, fsq_ref,
                   dq_ref, dk_ref, dv_ref, sq_ref, sk_ref, sv_ref, iq_ref, ikl_ref, ikh_ref, iw_ref):
    x = x_ref[0]
    tm = x.shape[0]
    ms = jnp.mean(x * x, axis=-1, keepdims=True)
    y = x * lax.rsqrt(ms + EPS) * g1_ref[...]
    mod = mod_ref[0]
    h = y * (1.0 + mod[1:2]) + mod[0:1]
    hb = h.astype(BF16)

    lane = lax.broadcasted_iota(I32, (tm, LANES), 1)
    is_head = lane < HEAD
    kpos = pl.program_id(1) * tm + lax.broadcasted_iota(I32, (tm, LANES), 0)
    hi_digit = (kpos // POS_RADIX).astype(F32)
    lo_digit = (kpos % POS_RADIX).astype(F32)
    kfeat = jnp.where(lane < HEAD + N_SPLIT, hi_digit, jnp.where(lane < HEAD + 2 * N_SPLIT, lo_digit, 0.0))

    def plain(seg_idx, out_ref):
        out_ref[0] = _dot(hb, wm_ref[:, seg_idx * SEG:(seg_idx + 1) * SEG]).astype(BF16)

    def normed(seg_idx, g_ref, scale, feat_ref, out_ref):
        half = SEG // 2
        for i in range(2):
            lo = seg_idx * SEG + i * half
            z = _dot(hb, wm_ref[:, lo:lo + half])
            ss = _group_sumsq(z)
            zn = z * lax.rsqrt(ss * (1.0 / HEAD) + EPS) * (g_ref[:, i * half:(i + 1) * half] * scale)
            for g in range(half // LANES):
                zg = zn[:, g * LANES:(g + 1) * LANES]
                for odd in range(2):
                    idx = i * (half // HEAD) + 2 * g + odd
                    src = zg if odd == 0 else pltpu.roll(zg, HEAD, 1)
                    feat = kfeat if feat_ref is None else feat_ref[idx:idx + 1, :]
                    out_ref[0, :, idx * LANES:(idx + 1) * LANES] = jnp.where(is_head, src, feat).astype(BF16)

    normed(0, gq_ref, HEAD ** -0.5 * LOG2E, fdq_ref, dq_ref)
    normed(1, gk_ref, 1.0, None, dk_ref)
    plain(2, dv_ref)
    normed(3, gsq_ref, HEAD ** -0.5 * LOG2E, fsq_ref, sq_ref)
    normed(4, gsk_ref, 1.0, None, sk_ref)
    plain(5, sv_ref)
    plain(6, iq_ref)

    t = _dot(hb, wt_ref[...])
    ikraw = jnp.where(lane < IDX_DIM, t, 0.0)
    ss = jnp.sum(ikraw * ikraw, axis=-1, keepdims=True) * (1.0 / IDX_DIM)
    ikn = ikraw * lax.rsqrt(ss + EPS) * gik_ref[...]
    ikl_ref[0] = ikn.astype(BF16)
    ikh_ref[0] = pltpu.roll(ikn, IDX_DIM, 1).astype(BF16)
    iwraw = jnp.where((lane >= IDX_DIM) & (lane < IDX_DIM + IDX_HEADS), t, 0.0)
    iw_ref[0] = pltpu.roll(iwraw * (IDX_HEADS ** -0.5), LANES - IDX_DIM, 1) * (IDX_DIM ** -0.5)


def _inproj(x, mod3, g1, wm, wt, gq, gk, gsq, gsk, gik, fdq, fsq, tm):
    B, S, D = x.shape
    ns = S // tm
    tok = lambda b, i: (b, i, 0)
    const2 = lambda b, i: (0, 0)
    seg_spec = pl.BlockSpec((1, tm, SEG), tok)
    wide_spec = pl.BlockSpec((1, tm, WIDE), tok)
    lane_spec = pl.BlockSpec((1, tm, LANES), tok)
    seg_shape = jax.ShapeDtypeStruct((B, S, SEG), BF16)
    wide_shape = jax.ShapeDtypeStruct((B, S, WIDE), BF16)
    return pl.pallas_call(
        _inproj_kernel,
        grid=(B, ns),
        in_specs=[pl.BlockSpec((1, tm, D), tok),
                  pl.BlockSpec((1, 6, D), lambda b, i: (b, 0, 0)),
                  pl.BlockSpec((1, D), const2),
                  pl.BlockSpec(wm.shape, const2),
                  pl.BlockSpec(wt.shape, const2),
                  pl.BlockSpec((1, SEG), const2), pl.BlockSpec((1, SEG), const2),
                  pl.BlockSpec((1, SEG), const2), pl.BlockSpec((1, SEG), const2),
                  pl.BlockSpec((1, LANES), const2),
                  pl.BlockSpec((N_MAPS, LANES), const2), pl.BlockSpec((N_MAPS, LANES), const2)],
        out_specs=[wide_spec, wide_spec, seg_spec, wide_spec, wide_spec, seg_spec, seg_spec,
                   lane_spec, lane_spec, lane_spec],
        out_shape=[wide_shape, wide_shape, seg_shape, wide_shape, wide_shape, seg_shape, seg_shape,
                   jax.ShapeDtypeStruct((B, S, LANES), BF16), jax.ShapeDtypeStruct((B, S, LANES), BF16),
                   jax.ShapeDtypeStruct((B, S, LANES), F32)],
        compiler_params=_params(("parallel", "parallel")),
        name="inproj",
    )(x, mod3, g1, wm, wt, gq, gk, gsq, gsk, gik, fdq, fsq)


def _diff_kernel(q_ref, k_ref, v_ref, lam_ref, g_ref, o_ref, s_ref, m_ref, l_ref, acc_ref, *, tq):
    qi = pl.program_id(1)
    ki = pl.program_id(2)
    n_maps = 2 * DIFF_HEADS
    reps = tq // LANES

    @pl.when(ki == 0)
    def _():
        m_ref[...] = jnp.full(m_ref.shape, NEG_BIG, F32)
        l_ref[...] = jnp.zeros(l_ref.shape, F32)
        acc_ref[...] = jnp.zeros(acc_ref.shape, F32)

    def step(diag):
        if diag:
            row = lax.broadcasted_iota(I32, (tq, tq), 0)
            col = lax.broadcasted_iota(I32, (tq, tq), 1)
            causal_bias = jnp.where(col <= row, 0.0, NEG_BIG)
        m_olds, m_news = [], []
        for idx in range(n_maps):
            s = _nt(q_ref[0, :, idx * LANES:(idx + 1) * LANES], k_ref[0, :, idx * LANES:(idx + 1) * LANES])
            if diag:
                s = s + causal_bias
            s_ref[idx] = s
            m_old = m_ref[idx]
            m_olds.append(m_old)
            m_news.append(jnp.maximum(m_old, jnp.max(s, axis=-1, keepdims=True)))
        for idx in range(n_maps):
            h = idx // 2
            p = jnp.exp2(s_ref[idx] - _rep(m_news[idx], reps))
            alpha = jnp.exp2(m_olds[idx] - m_news[idx])
            psum = p[:, 0:LANES]
            for g in range(1, reps):
                psum = psum + p[:, g * LANES:(g + 1) * LANES]
            l_ref[idx] = alpha * l_ref[idx] + psum
            acc_ref[idx] = alpha * acc_ref[idx] + _dot(p.astype(BF16), v_ref[0, :, h * LANES:(h + 1) * LANES])
            m_ref[idx] = m_news[idx]

    @pl.when(ki < qi)
    def _():
        step(False)

    @pl.when(ki == qi)
    def _():
        step(True)
        lv = lam_ref[...]
        lam = (jnp.exp(jnp.sum(lv[0:1] * lv[1:2], axis=-1, keepdims=True))
               - jnp.exp(jnp.sum(lv[2:3] * lv[3:4], axis=-1, keepdims=True)) + LAM_INIT)
        for h in range(DIFF_HEADS):
            o1 = acc_ref[2 * h] / jnp.sum(l_ref[2 * h], axis=-1, keepdims=True)
            o2 = acc_ref[2 * h + 1] / jnp.sum(l_ref[2 * h + 1], axis=-1, keepdims=True)
            o = o1 - lam * o2
            ms = jnp.mean(o * o, axis=-1, keepdims=True)
            on = o * lax.rsqrt(ms + EPS) * g_ref[...]
            o_ref[0, :, h * LANES:(h + 1) * LANES] = (on * (1.0 - LAM_INIT)).astype(BF16)


def _diff_attention(dq, dk, dv, lamv, subln_g, tq):
    B, S, W = dq.shape
    V = dv.shape[2]
    nq = S // tq
    n_maps = 2 * DIFF_HEADS
    return pl.pallas_call(
        functools.partial(_diff_kernel, tq=tq),
        grid=(B, nq, nq),
        in_specs=[pl.BlockSpec((1, tq, W), lambda b, i, j: (b, i, 0)),
                  pl.BlockSpec((1, tq, W), lambda b, i, j: (b, jnp.minimum(i, j), 0)),
                  pl.BlockSpec((1, tq, V), lambda b, i, j: (b, jnp.minimum(i, j), 0)),
                  pl.BlockSpec(lamv.shape, lambda b, i, j: (0, 0)),
                  pl.BlockSpec((1, LANES), lambda b, i, j: (0, 0))],
        out_specs=pl.BlockSpec((1, tq, V), lambda b, i, j: (b, i, 0)),
        out_shape=jax.ShapeDtypeStruct((B, S, V), BF16),
        scratch_shapes=[pltpu.VMEM((n_maps, tq, tq), F32),
                        pltpu.VMEM((n_maps, tq, LANES), F32),
                        pltpu.VMEM((n_maps, tq, LANES), F32),
                        pltpu.VMEM((n_maps, tq, LANES), F32)],
        compiler_params=_params(("parallel", "parallel", "arbitrary")),
        name="diff_attention",
    )(dq, dk, dv, lamv, subln_g)


def _score_key(v):
    bits = lax.bitcast_convert_type(v, I32)
    return bits ^ ((bits >> 31) & 0x7FFFFFFF)


def _key_score(k):
    return lax.bitcast_convert_type(k ^ ((k >> 31) & 0x7FFFFFFF), F32)


_SUM_LANE = (LANES - 1, 0)
N_CAND = 12
CAND_ROWS = 32


def _dsa_kernel(iq_ref, iw_ref, ikl_ref, ikh_ref, q_ref, k_ref, v_ref, o_ref,
                key_ref, cand_ref, thr_ref, nties_ref, nge_ref, mb_ref, s_ref, m_ref, acc_ref, *, tq, tk, rs, topk):
    qi = pl.program_id(1)
    q0 = qi * tq
    nkc = (q0 + tq + tk - 1) // tk
    def causal(k0):
        row = q0 + lax.broadcasted_iota(I32, (tq, tk), 0)
        return lax.broadcasted_iota(I32, (tq, tk), 1) + k0 <= row

    low =lax.broadcasted_iota(I32, (tq, LANES), 1) < HEAD
    reps = tk // LANES
    iw = iw_ref[0]

    def score_chunk(kc, carry):
        k0 = pl.multiple_of(kc * tk, tk)
        ikl = ikl_ref[0, pl.ds(k0, tk), :]
        ikh = ikh_ref[0, pl.ds(k0, tk), :]
        sc = jnp.zeros((tq, tk), F32)
        for j in range(IDX_HEADS // 2):
            iqp = iq_ref[0, :, j * LANES:(j + 1) * LANES]
            sc = sc + iw[:, 2 * j:2 * j + 1] * jnp.maximum(_nt(iqp, ikl), 0.0)
            sc = sc + iw[:, 2 * j + 1:2 * j + 2] * jnp.maximum(_nt(iqp, ikh), 0.0)
        sc = jnp.where(sc == 0.0, 0.0, sc)
        sc = jnp.where(causal(k0), sc, -jnp.inf)
        key_ref[:, pl.ds(k0, tk)] = _score_key(sc)
        return carry

    lax.fori_loop(0, nkc, score_chunk, 0)

    groups = [slice(r0, r0 + rs) for r0 in range(0, tq, rs)]
    kf = float(topk)

    def candidates(g, carry):
        rows = pl.ds(pl.multiple_of(g * CAND_ROWS, CAND_ROWS), CAND_ROWS)

        def insert_chunk(kc, best):
            k0 = pl.multiple_of(kc * tk, tk)
            x = _key_score(key_ref[rows, pl.ds(k0, tk)])
            best = list(best)
            for lg in range(reps):
                v = x[:, lg * LANES:(lg + 1) * LANES]
                for i in range(N_CAND):
                    best[i], v = jnp.maximum(best[i], v), jnp.minimum(best[i], v)
            return tuple(best)

        lowest = jnp.full((CAND_ROWS, LANES), -jnp.inf, F32)
        best = lax.fori_loop(0, nkc, insert_chunk, (lowest,) * N_CAND)
        for i in range(N_CAND):
            cand_ref[rows, i * LANES:(i + 1) * LANES] = _score_key(best[i])
        return carry

    lax.fori_loop(0, tq // CAND_ROWS, candidates, 0)

    def count(ref, n_chunks, thrs, strict):
        accs = []
        for rows, thr in zip(groups, thrs):
            thr_t = _rep(thr, reps)

            def body(kc, acc, rows=rows, thr_t=thr_t):
                k0 = kc * tk if isinstance(kc, int) else pl.multiple_of(kc * tk, tk)
                keyc = ref[rows, pl.ds(k0, tk)]
                hit = jnp.where((keyc > thr_t) if strict else (keyc >= thr_t), 1.0, 0.0)
                for g in range(reps):
                    acc = acc + hit[:, g * LANES:(g + 1) * LANES]
                return acc

            acc = jnp.zeros((rs, LANES), F32)
            if isinstance(n_chunks, int):
                for kc in range(n_chunks):
                    acc = body(kc, acc)
            else:
                acc = lax.fori_loop(0, n_chunks, body, acc)
            accs.append(acc)
        return [jnp.broadcast_to(jnp.sum(acc, axis=-1, keepdims=True), (rs, LANES)) for acc in accs]

    def search(ref, n_chunks):
        def bit_step(i, tus):
            bit = jnp.left_shift(jnp.int32(1), 31 - i)
            cands = [tu | bit for tu in tus]
            cnts = count(ref, n_chunks, [c ^ INT_MIN for c in cands], False)
            return tuple(jnp.where(cnt >= kf, c, tu) for cnt, c, tu in zip(cnts, cands, tus))

        tus = lax.fori_loop(0, 32, bit_step, tuple(jnp.zeros((rs, LANES), I32) for _ in groups))
        return [tu ^ INT_MIN for tu in tus]

    def publish(thrs, n_above, n_reach):
        for rows, thr, n_gt, n_ge in zip(groups, thrs, n_above, n_reach):
            thr_ref[rows, :] = thr
            nties_ref[rows, :] = kf - n_gt
            nge_ref[rows, :] = n_ge

    cand_chunks = N_CAND * LANES // tk
    thrs = search(cand_ref, cand_chunks)
    inside = count(cand_ref, cand_chunks, thrs, False)
    publish(thrs, count(cand_ref, cand_chunks, thrs, True), inside)
    overall = count(key_ref, nkc, thrs, False)
    missed = [jnp.max(jnp.where(a != b, 1.0, 0.0)) for a, b in zip(inside, overall)]

    @pl.when(functools.reduce(jnp.maximum, missed) > 0.0)
    def _():
        full = search(key_ref, nkc)
        publish(full, count(key_ref, nkc, full, True), count(key_ref, nkc, full, False))

    m_ref[...] = jnp.full(m_ref.shape, NEG_BIG, F32)
    acc_ref[...] = jnp.zeros(acc_ref.shape, F32)
    tie_rows = jnp.max(jnp.where(nge_ref[...] != kf, 1.0, 0.0))

    def attend_chunk(kc, ties_before, ranked):
        k0 = pl.multiple_of(kc * tk, tk)
        keyc = key_ref[:, pl.ds(k0, tk)]
        thr_t = _rep(thr_ref[...], reps)
        if ranked:
            tri = jnp.where(lax.broadcasted_iota(I32, (tk, tk), 0) < lax.broadcasted_iota(I32, (tk, tk), 1),
                            1.0, 0.0).astype(BF16)
            eq = jnp.where(keyc == thr_t, 1.0, 0.0)
            rank = _dot(eq.astype(BF16), tri) + _rep(ties_before, reps)
            take = jnp.where(keyc > thr_t, 1.0, jnp.where(rank < _rep(nties_ref[...], reps), eq, 0.0))
            psum = eq[:, 0:LANES]
            for g in range(1, reps):
                psum = psum + eq[:, g * LANES:(g + 1) * LANES]
            ties_before = ties_before + jnp.sum(psum, axis=-1, keepdims=True)
            mb_ref[...] = jnp.where(causal(k0), jnp.where(take > 0.5, 0.0, NEG_BIG), NEG_BIG)
        else:
            mb_ref[...] = jnp.where(keyc >= thr_t, 0.0, NEG_BIG)
        mb = mb_ref[...]
        m_olds, m_news = [], []
        for hd in range(DSA_HEADS):
            s = _nt(q_ref[0, :, hd * LANES:(hd + 1) * LANES],
                    k_ref[0, pl.ds(k0, tk), hd * LANES:(hd + 1) * LANES]) + mb
            s_ref[hd] = s
            m_old = m_ref[hd]
            m_olds.append(m_old)
            m_news.append(jnp.maximum(m_old, jnp.max(s, axis=-1, keepdims=True)))
        for j in range(DSA_HEADS // 2):
            vp = v_ref[0, pl.ds(k0, tk), j * LANES:(j + 1) * LANES]
            vlane = lax.broadcasted_iota(I32, vp.shape, 1)
            for a in range(2):
                hd = 2 * j + a
                keep = (vlane < HEAD) if a == 0 else (vlane >= HEAD)
                va = jnp.where(vlane == _SUM_LANE[a], jnp.ones_like(vp), jnp.where(keep, vp, jnp.zeros_like(vp)))
                p = jnp.exp2((s_ref[hd] - _rep(m_news[hd], reps)).astype(BF16))
                acc_ref[hd] = jnp.exp2(m_olds[hd] - m_news[hd]) * acc_ref[hd] + _dot(p, va)
                m_ref[hd] = m_news[hd]
        return ties_before

    for ranked in (False, True):
        @pl.when((tie_rows > 0.0) == ranked)
        def _(ranked=ranked):
            lax.fori_loop(0, nkc, functools.partial(attend_chunk, ranked=ranked), jnp.zeros((tq, LANES), F32))

    lane = lax.broadcasted_iota(I32, (tq, LANES), 1)
    for j in range(DSA_HEADS // 2):
        oa, ob = acc_ref[2 * j], acc_ref[2 * j + 1]
        la = jnp.sum(jnp.where(lane == _SUM_LANE[0], oa, 0.0), axis=-1, keepdims=True)
        lb = jnp.sum(jnp.where(lane == _SUM_LANE[1], ob, 0.0), axis=-1, keepdims=True)
        o_ref[0, :, j * LANES:(j + 1) * LANES] = jnp.where(low, oa / la, ob / lb).astype(BF16)


def _dsa_attention(iq, iw, ikl, ikh, sq, sk, sv, tq, tk, rs, topk):
    B, S, W = sq.shape
    V = sv.shape[2]
    nq = S // tq
    assert (N_CAND * LANES) % tk == 0 and tq % CAND_ROWS == 0 and tq % rs == 0 and tk > topk
    qblk = lambda b, i: (b, i, 0)
    full = lambda b, i: (b, 0, 0)
    once = pl.Buffered(1)
    return pl.pallas_call(
        functools.partial(_dsa_kernel, tq=tq, tk=tk, rs=rs, topk=topk),
        grid=(B, nq),
        in_specs=[pl.BlockSpec((1, tq, iq.shape[2]), qblk),
                  pl.BlockSpec((1, tq, LANES), qblk),
                  pl.BlockSpec((1, S, LANES), full, pipeline_mode=once),
                  pl.BlockSpec((1, S, LANES), full, pipeline_mode=once),
                  pl.BlockSpec((1, tq, W), qblk),
                  pl.BlockSpec((1, S, W), full, pipeline_mode=once),
                  pl.BlockSpec((1, S, V), full, pipeline_mode=once)],
        out_specs=pl.BlockSpec((1, tq, V), qblk),
        out_shape=jax.ShapeDtypeStruct((B, S, V), BF16),
        scratch_shapes=[pltpu.VMEM((tq, S), I32),
                        pltpu.VMEM((tq, N_CAND * LANES), I32),
                        pltpu.VMEM((tq, LANES), I32),
                        pltpu.VMEM((tq, LANES), F32),
                        pltpu.VMEM((tq, LANES), F32),
                        pltpu.VMEM((tq, tk), F32),
                        pltpu.VMEM((DSA_HEADS, tq, tk), F32),
                        pltpu.VMEM((DSA_HEADS, tq, LANES), F32),
                        pltpu.VMEM((DSA_HEADS, tq, LANES), F32)],
        compiler_params=_params(("parallel", "arbitrary")),
        name="dsa_attention",
    )(iq, iw, ikl, ikh, sq, sk, sv)


def _mix_kernel(do_ref, so_ref, x_ref, mod_ref, g2_ref, wo1_ref, wo2_ref, wsg_ref, wsu_ref, wsd_ref,
                rwt_ref, base_ref, h2_ref, lg_ref):
    mix = _dot(do_ref[0], wo1_ref[...]) + _dot(so_ref[0], wo2_ref[...])
    mod = mod_ref[0]
    x1 = x_ref[0] + mod[2:3] * mix
    ms = jnp.mean(x1 * x1, axis=-1, keepdims=True)
    h2 = x1 * lax.rsqrt(ms + EPS) * g2_ref[...] * (1.0 + mod[4:5]) + mod[3:4]
    hb = h2.astype(BF16)
    gate = _dot(hb, wsg_ref[...])
    up = _dot(hb, wsu_ref[...])
    act = gate / (1.0 + jnp.exp(-gate)) * up
    shared = _dot(act.astype(BF16), wsd_ref[...])
    base_ref[0] = x1 + mod[5:6] * shared
    lg_ref[...] = _nt(rwt_ref[...], hb)
    hf = hb.astype(F32)
    tm = hf.shape[0]
    bits = lax.bitcast_convert_type(hf, I32)
    half = D_MODEL // 2
    packed = (bits[:, half:] & -65536) | lax.shift_right_logical(bits[:, :half], 16)
    for j in range(PACK_TILES):
        h2_ref[pl.ds(j, tm, stride=PACK_TILES), :] = packed[:, j * LANES:(j + 1) * LANES]


def _mix(diff_out, dsa_out, x, mod3, g2, wo1, wo2, wsg, wsu, wsd, rwt, tm):
    B, S, D = x.shape
    ns = S // tm
    T = B * S
    tok = lambda b, i: (b, i, 0)
    c2 = lambda b, i: (0, 0)
    return pl.pallas_call(
        _mix_kernel,
        grid=(B, ns),
        in_specs=[pl.BlockSpec((1, tm, SEG), tok), pl.BlockSpec((1, tm, SEG), tok),
                  pl.BlockSpec((1, tm, D), tok),
                  pl.BlockSpec((1, 6, D), lambda b, i: (b, 0, 0)),
                  pl.BlockSpec((1, D), c2),
                  pl.BlockSpec(wo1.shape, c2), pl.BlockSpec(wo2.shape, c2),
                  pl.BlockSpec(wsg.shape, c2), pl.BlockSpec(wsu.shape, c2), pl.BlockSpec(wsd.shape, c2),
                  pl.BlockSpec(rwt.shape, c2)],
        out_specs=[pl.BlockSpec((1, tm, D), tok),
                   pl.BlockSpec((tm * PACK_TILES, LANES), lambda b, i: (b * ns + i, 0)),
                   pl.BlockSpec((N_EXPERTS, tm), lambda b, i: (0, b * ns + i))],
        out_shape=[jax.ShapeDtypeStruct((B, S, D), F32),
                   jax.ShapeDtypeStruct((T * PACK_TILES, LANES), I32),
                   jax.ShapeDtypeStruct((N_EXPERTS, T), F32)],
        compiler_params=_params(("parallel", "parallel")),
        name="mix_shared_router",
    )(diff_out, dsa_out, x, mod3, g2, wo1, wo2, wsg, wsu, wsd, rwt)


def _first_max(v, idx, sentinel):
    m = jnp.max(v, axis=0, keepdims=True)
    i = jnp.min(jnp.where(v == m, idx, sentinel), axis=0, keepdims=True)
    return m, i


def _route_kernel(lg_ref, bias_ref, eidx_ref, gate_ref):
    lg = lg_ref[...]
    tt = lg.shape[1]
    scores = 1.0 / (1.0 + jnp.exp(-lg))
    biased = scores + bias_ref[...]
    gi = lax.broadcasted_iota(I32, (GROUP_SIZE, tt), 0).astype(F32)
    gscore = []
    for g in range(N_GROUPS):
        blk = biased[g * GROUP_SIZE:(g + 1) * GROUP_SIZE, :]
        m1, i1 = _first_max(blk, gi, float(GROUP_SIZE))
        m2 = jnp.max(jnp.where(gi == i1, -jnp.inf, blk), axis=0, keepdims=True)
        gscore.append(m1 + m2)
    gs = jnp.concatenate(gscore, axis=0)
    gidx = lax.broadcasted_iota(I32, (N_GROUPS, tt), 0).astype(F32)
    chosen = jnp.zeros((N_GROUPS, tt), F32)
    for _ in range(TOPK_GROUPS):
        _, ig = _first_max(gs, gidx, float(N_GROUPS))
        hit = gidx == ig
        chosen = jnp.where(hit, 1.0, chosen)
        gs = jnp.where(hit, -jnp.inf, gs)
    masked = jnp.concatenate(
        [jnp.where(chosen[g:g + 1, :] > 0.5, biased[g * GROUP_SIZE:(g + 1) * GROUP_SIZE, :], -jnp.inf)
         for g in range(N_GROUPS)], axis=0)
    ei = lax.broadcasted_iota(I32, (N_EXPERTS, tt), 0).astype(F32)
    ids, ws = [], []
    for _ in range(TOP_K):
        _, ie = _first_max(masked, ei, float(N_EXPERTS))
        hit = ei == ie
        ws.append(jnp.sum(jnp.where(hit, scores, 0.0), axis=0, keepdims=True))
        ids.append(ie)
        masked = jnp.where(hit, -jnp.inf, masked)
    w = jnp.concatenate(ws, axis=0)
    gate_ref[...] = w / jnp.sum(w, axis=0, keepdims=True) * ROUTED_SCALE
    eidx_ref[...] = jnp.concatenate(ids, axis=0).astype(I32)


def _route(logits_t, bias_col, tt):
    E, T = logits_t.shape
    return pl.pallas_call(
        _route_kernel,
        grid=(T // tt,),
        in_specs=[pl.BlockSpec((E, tt), lambda i: (0, i)),
                  pl.BlockSpec((E, 1), lambda i: (0, 0))],
        out_specs=[pl.BlockSpec((TOP_K, tt), lambda i: (0, i)),
                   pl.BlockSpec((TOP_K, tt), lambda i: (0, i))],
        out_shape=[jax.ShapeDtypeStruct((TOP_K, T), I32),
                   jax.ShapeDtypeStruct((TOP_K, T), F32)],
        compiler_params=_params(("parallel",)),
        name="route",
    )(logits_t, bias_col)


def _plan_kernel(eidx_ref, dest_ref, bexp_ref, nused_ref, cnt_col, cnt_row, slot_base, *, blk, nb_pad):
    ph = pl.program_id(0)
    i = pl.program_id(1)
    tt = eidx_ref.shape[1]
    eidx = eidx_ref[...]
    ei = lax.broadcasted_iota(I32, (N_EXPERTS, tt), 0)
    onehot = jnp.zeros((N_EXPERTS, tt), F32)
    for k in range(TOP_K):
        onehot = onehot + jnp.where(ei == eidx[k:k + 1, :], 1.0, 0.0)
    oh = onehot.astype(BF16)

    @pl.when((ph == 0) & (i == 0))
    def _():
        cnt_col[...] = jnp.zeros(cnt_col.shape, F32)
        cnt_row[...] = jnp.zeros(cnt_row.shape, F32)

    @pl.when(ph == 0)
    def _():
        cnt_col[...] += _dot(oh, jnp.ones((tt, LANES), BF16))
        cnt_row[...] += _nt(jnp.ones((8, tt), BF16), oh)

    @pl.when((ph == 1) & (i == 0))
    def _():
        inv = 1.0 / blk
        nb_col = jnp.floor((cnt_col[:, 0:1] + (blk - 1)) * inv)
        nb_row = jnp.floor((cnt_row[0:1, :] + (blk - 1)) * inv)
        r = lax.broadcasted_iota(I32, (N_EXPERTS, N_EXPERTS), 0)
        c = lax.broadcasted_iota(I32, (N_EXPERTS, N_EXPERTS), 1)
        bstart = jnp.sum(jnp.where(c < r, nb_row, 0.0), axis=-1, keepdims=True)
        bend = bstart + nb_col
        slot_base[...] = bstart * blk
        jb = lax.broadcasted_iota(I32, (N_EXPERTS, nb_pad), 1).astype(F32)
        be = jnp.sum(jnp.where(bend <= jb, 1.0, 0.0), axis=0, keepdims=True)
        bexp_ref[...] = jnp.minimum(be, N_EXPERTS - 1.0).astype(I32)
        nused_ref[...] = jnp.broadcast_to(jnp.sum(nb_row, axis=-1, keepdims=True), nused_ref.shape).astype(I32)

    @pl.when(ph == 1)
    def _():
        tri = jnp.where(lax.broadcasted_iota(I32, (tt, tt), 0) < lax.broadcasted_iota(I32, (tt, tt), 1),
                        1.0, 0.0).astype(BF16)
        slot = _dot(oh, tri) + slot_base[...]
        for k in range(TOP_K):
            dk = jnp.sum(jnp.where(ei == eidx[k:k + 1, :], slot, 0.0), axis=0, keepdims=True)
            dest_ref[k:k + 1, :] = dk.astype(I32)
        slot_base[...] += jnp.sum(onehot, axis=-1, keepdims=True)


def _plan(eidx, tt, blk, nb_pad):
    K, T = eidx.shape
    nt = T // tt
    return pl.pallas_call(
        functools.partial(_plan_kernel, blk=blk, nb_pad=nb_pad),
        grid=(2, nt),
        in_specs=[pl.BlockSpec((K, tt), lambda p, i: (0, i))],
        out_specs=[pl.BlockSpec((K, tt), lambda p, i: (0, i * p)),
                   pl.BlockSpec((1, nb_pad), lambda p, i: (0, 0)),
                   pl.BlockSpec((1, LANES), lambda p, i: (0, 0))],
        out_shape=[jax.ShapeDtypeStruct((K, T), I32),
                   jax.ShapeDtypeStruct((1, nb_pad), I32),
                   jax.ShapeDtypeStruct((1, LANES), I32)],
        scratch_shapes=[pltpu.VMEM((N_EXPERTS, LANES), F32),
                        pltpu.VMEM((8, N_EXPERTS), F32),
                        pltpu.VMEM((N_EXPERTS, 1), F32)],
        compiler_params=_params(("arbitrary", "arbitrary")),
        name="plan",
    )(eidx)


def _row(ref, r):
    return ref.at[pl.ds(pl.multiple_of(r * PACK_TILES, PACK_TILES), PACK_TILES), :]


def _dispatch_kernel(dest_ref, h_ref, xs_in_ref, xs_ref, sem):
    del xs_in_ref
    tt = h_ref.shape[0] // PACK_TILES

    def issue(t, c):
        for k in range(TOP_K):
            pltpu.make_async_copy(_row(h_ref, t), _row(xs_ref, dest_ref[k, t]), sem).start()
        return c

    lax.fori_loop(0, tt, issue, 0)

    def drain(t, c):
        for k in range(TOP_K):
            pltpu.make_async_copy(_row(h_ref, 0), _row(xs_ref, 0), sem).wait()
        return c

    lax.fori_loop(0, tt, drain, 0)


def _dispatch(dest, h2rows, xs_init, tt):
    T = h2rows.shape[0] // PACK_TILES
    return pl.pallas_call(
        _dispatch_kernel,
        grid=(T // tt,),
        in_specs=[pl.BlockSpec((TOP_K, tt), lambda i: (0, i), memory_space=pltpu.SMEM),
                  pl.BlockSpec((tt * PACK_TILES, LANES), lambda i: (i, 0)),
                  pl.BlockSpec(memory_space=pl.ANY)],
        out_specs=pl.BlockSpec(memory_space=pl.ANY),
        out_shape=jax.ShapeDtypeStruct(xs_init.shape, xs_init.dtype),
        scratch_shapes=[pltpu.SemaphoreType.DMA(())],
        input_output_aliases={2: 0},
        compiler_params=_params(("arbitrary",), has_side_effects=True),
        name="dispatch",
    )(dest, h2rows, xs_init)


def _experts_kernel(bexp_ref, nused_ref, xs_ref, wg_ref, wu_ref, wd_ref, y_ref, xb_ref, wgb_ref, wub_ref, wdb_ref):
    j = pl.program_id(0)

    @pl.when(j < nused_ref[0])
    def _():
        @pl.when((j == 0) | (bexp_ref[j] != bexp_ref[jnp.maximum(j - 1, 0)]))
        def _():
            wgb_ref[...] = wg_ref[0].astype(BF16)
            wub_ref[...] = wu_ref[0].astype(BF16)
            wdb_ref[...] = wd_ref[0].astype(BF16)

        blk = xb_ref.shape[0]
        half = D_MODEL // 2
        for c in range(PACK_TILES):
            w = xs_ref[pl.ds(c, blk, stride=PACK_TILES), :]
            cols = slice(c * LANES, (c + 1) * LANES)
            xb_ref[:, cols] = lax.bitcast_convert_type(w << 16, F32).astype(BF16)
            xb_ref[:, half + c * LANES:half + (c + 1) * LANES] = lax.bitcast_convert_type(w & -65536, F32).astype(BF16)
        xb = xb_ref[...]
        gate = _dot(xb, wgb_ref[...])
        up = _dot(xb, wub_ref[...])
        act = gate / (1.0 + jnp.exp(-gate)) * up
        y = _dot(act.astype(BF16), wdb_ref[...])
        bits = lax.bitcast_convert_type(y.astype(BF16).astype(F32), I32)
        packed = (bits[:, half:] & -65536) | lax.shift_right_logical(bits[:, :half], 16)
        for c in range(PACK_TILES):
            y_ref[pl.ds(c, blk, stride=PACK_TILES), :] = packed[:, c * LANES:(c + 1) * LANES]


def _experts(bexp, nused, xs, wg, wu, wd, blk, n_blocks):
    live = lambda j, be, nu: jnp.minimum(j, nu[0] - 1)
    row_spec = pl.BlockSpec((blk * PACK_TILES, LANES), lambda j, be, nu: (live(j, be, nu), 0))
    wspec = lambda w: pl.BlockSpec((1,) + w.shape[1:], lambda j, be, nu: (be[live(j, be, nu)], 0, 0))
    return pl.pallas_call(
        _experts_kernel,
        grid_spec=pltpu.PrefetchScalarGridSpec(
            num_scalar_prefetch=2,
            grid=(n_blocks,),
            in_specs=[row_spec, wspec(wg), wspec(wu), wspec(wd)],
            out_specs=row_spec,
            scratch_shapes=[pltpu.VMEM((blk, D_MODEL), BF16),
                            pltpu.VMEM(wg.shape[1:], BF16), pltpu.VMEM(wu.shape[1:], BF16),
                            pltpu.VMEM(wd.shape[1:], BF16)]),
        out_shape=jax.ShapeDtypeStruct(xs.shape, I32),
        compiler_params=_params(("arbitrary",)),
        name="experts",
    )(bexp, nused, xs, wg, wu, wd)


def _combine_kernel(dest_ref, gate_ref, base_ref, mod_ref, y_ref, o_ref, buf, sem):
    tt = base_ref.shape[1]

    def issue(t, c):
        for k in range(TOP_K):
            pltpu.make_async_copy(_row(y_ref, dest_ref[k, t]), _row(buf, k * tt + t), sem).start()
        return c

    lax.fori_loop(0, tt, issue, 0)

    def drain(t, c):
        for k in range(TOP_K):
            pltpu.make_async_copy(_row(y_ref, 0), _row(buf, 0), sem).wait()
        return c

    lax.fori_loop(0, tt, drain, 0)

    gates = gate_ref[...]
    g2 = mod_ref[0][5:6]
    half = D_MODEL // 2
    for j in range(PACK_TILES):
        lo = jnp.zeros((tt, LANES), F32)
        hi = jnp.zeros((tt, LANES), F32)
        for k in range(TOP_K):
            w = buf[pl.ds(k * tt * PACK_TILES + j, tt, stride=PACK_TILES), :]
            lo = lo + gates[:, k:k + 1] * lax.bitcast_convert_type(w << 16, F32)
            hi = hi + gates[:, k:k + 1] * lax.bitcast_convert_type(w & -65536, F32)
        for off, acc in ((0, lo), (half, hi)):
            cols = slice(off + j * LANES, off + (j + 1) * LANES)
            o_ref[0, :, cols] = base_ref[0, :, cols] + g2[:, cols] * acc


def _combine(dest, gates_tk, base, mod3, y, tt):
    B, S, D = base.shape
    ns = S // tt
    return pl.pallas_call(
        _combine_kernel,
        grid=(B, ns),
        in_specs=[pl.BlockSpec((TOP_K, tt), lambda b, i: (0, b * ns + i), memory_space=pltpu.SMEM),
                  pl.BlockSpec((tt, TOP_K), lambda b, i: (b * ns + i, 0)),
                  pl.BlockSpec((1, tt, D), lambda b, i: (b, i, 0)),
                  pl.BlockSpec((1, 6, D), lambda b, i: (b, 0, 0)),
                  pl.BlockSpec(memory_space=pl.ANY)],
        out_specs=pl.BlockSpec((1, tt, D), lambda b, i: (b, i, 0)),
        out_shape=jax.ShapeDtypeStruct((B, S, D), F32),
        scratch_shapes=[pltpu.VMEM((TOP_K * tt * PACK_TILES, LANES), I32),
                        pltpu.SemaphoreType.DMA(())],
        compiler_params=_params(("arbitrary", "arbitrary")),
        name="combine",
    )(dest, gates_tk, base, mod3, y)


def _alibi_q_features(n_heads, maps_per_head):
    slopes = 2.0 ** (-8.0 * jnp.arange(1, n_heads + 1, dtype=F32) / n_heads)
    c = jnp.repeat(slopes, maps_per_head) * LOG2E * POS_RADIX
    pieces = []
    rest = c
    for _ in range(N_SPLIT):
        p = rest.astype(BF16).astype(F32)
        pieces.append(p)
        rest = rest - p
    hi = jnp.stack(pieces, axis=1)
    feat = jnp.concatenate([hi, hi / POS_RADIX], axis=1)
    return jnp.pad(feat, ((0, 0), (HEAD, LANES - HEAD - 2 * N_SPLIT)))


def kernel(x, c, ada_w, ada_b, norm1_g, norm2_g, w_in, diff_q_norm_g, diff_k_norm_g, lam_q1, lam_k1, lam_q2, lam_k2, diff_subln_g, dsa_q_norm_g, dsa_k_norm_g, idx_k_norm_g, w_out, router_w, router_bias, exp_w_gate, exp_w_up, exp_w_down, shared_w_gate, shared_w_up, shared_w_down):
    B, S, D = x.shape
    assert D == D_MODEL and ada_w.shape[0] == 1 and S <= POS_RADIX * 128
    T = B * S
    topk = min(DSA_TOPK, S // 4)
    tm = min(512, S)
    tq_diff = min(512, S)
    tq_dsa = min(256, S)
    tk_dsa = min(512, S)
    rs_dsa = min(128, tq_dsa)
    tt_route = min(512, T)
    tt_move = min(256, S)
    blk = 512
    n_blocks = (T * TOP_K) // blk + N_EXPERTS
    nb_pad = -(-n_blocks // LANES) * LANES

    n_main = 7 * SEG
    wm = w_in[0, :, :n_main].astype(BF16)
    wt = jnp.pad(w_in[0, :, n_main:], ((0, 0), (0, LANES - (IDX_DIM + IDX_HEADS)))).astype(BF16)
    tile8 = lambda g: jnp.tile(g[0], SEG // g.shape[1]).reshape(1, SEG)
    gik = jnp.pad(idx_k_norm_g[0], (0, LANES - IDX_DIM)).reshape(1, LANES)
    lamv = jnp.concatenate([lam_q1, lam_k1, lam_q2, lam_k2], axis=0)
    wo1 = w_out[0, :SEG].astype(BF16)
    wo2 = w_out[0, SEG:].astype(BF16)
    rwt = router_w[0].T.astype(BF16)
    wg, wu, wd = exp_w_gate[0], exp_w_up[0], exp_w_down[0]

    mod3 = _ada(c, ada_w[0], ada_b[0]).reshape(B, 6, D)

    dq, dk, dv, sq, sk, sv, iq, ikl, ikh, iw = _inproj(
        x, mod3, norm1_g, wm, wt, tile8(diff_q_norm_g), tile8(diff_k_norm_g),
        tile8(dsa_q_norm_g), tile8(dsa_k_norm_g), gik,
        _alibi_q_features(DIFF_HEADS, 2), _alibi_q_features(DSA_HEADS, 1), tm)

    diff_out = _diff_attention(dq, dk, dv, lamv, diff_subln_g, tq_diff)
    dsa_out = _dsa_attention(iq, iw, ikl, ikh, sq, sk, sv, tq_dsa, tk_dsa, rs_dsa, topk)

    base, h2rows, logits_t = _mix(diff_out, dsa_out, x, mod3, norm2_g, wo1, wo2,
                                  shared_w_gate[0].astype(BF16), shared_w_up[0].astype(BF16),
                                  shared_w_down[0].astype(BF16), rwt, tm)

    eidx, gates = _route(logits_t, router_bias[0].reshape(N_EXPERTS, 1), tt_route)
    dest, bexp, nused = _plan(eidx, tt_route, blk, nb_pad)

    xs = _dispatch(dest, h2rows, jnp.zeros((n_blocks * blk * PACK_TILES, LANES), I32), tt_move)
    y = _experts(bexp.reshape(nb_pad), nused[0, :1], xs, wg, wu, wd, blk, n_blocks)
    return _combine(dest, gates.T, base, mod3, y, tt_move)
```

```python
import functools
import math

import jax
import jax.numpy as jnp
from jax import lax
from jax.experimental import pallas as pl
from jax.experimental.pallas import tpu as pltpu

F32 = jnp.float32
BF16 = jnp.bfloat16
I32 = jnp.int32

D_MODEL = 1024
DIFF_HEADS = 4
DSA_HEADS = 8
IDX_HEADS = 8
IDX_DIM = 64
DSA_TOPK = 256
N_EXPERTS = 256
TOP_K = 8
N_GROUPS = 8
GROUP_SIZE = N_EXPERTS // N_GROUPS
TOPK_GROUPS = 4
ROUTED_SCALE = 2.5
EPS = 1e-6
LAM_INIT = 0.2

LANES = 128
PACK_TILES = D_MODEL // (2 * LANES)
SEG = 512
HEAD = 64
N_MAPS = SEG // HEAD
WIDE = N_MAPS * LANES
POS_RADIX = 64
N_SPLIT = 3
NEG_BIG = -1e30
INT_MIN = -2147483648
LOG2E = math.log2(math.e)
VMEM_LIMIT = 56 * 1024 * 1024

NT_DIMS = (((1,), (1,)), ((), ()))


def _nt(a, b):
    return lax.dot_general(a, b, NT_DIMS, preferred_element_type=F32)


def _dot(a, b):
    return jnp.dot(a, b, preferred_element_type=F32)


def _rep(x, reps):
    return jnp.concatenate([x] * reps, axis=1)


def _params(sem, vmem=VMEM_LIMIT, **kw):
    return pltpu.CompilerParams(dimension_semantics=sem, vmem_limit_bytes=vmem, **kw)


def _ada_kernel(c_ref, w_ref, b_ref, o_ref):
    c = c_ref[...]
    s = c / (1.0 + jnp.exp(-c))
    o_ref[...] = jnp.dot(s, w_ref[...], preferred_element_type=F32,
                         precision=lax.Precision.HIGHEST) + b_ref[...]


def _ada(c, w, b):
    B, D = c.shape
    N = w.shape[1]
    tn = D
    return pl.pallas_call(
        _ada_kernel,
        grid=(N // tn,),
        in_specs=[pl.BlockSpec((B, D), lambda j: (0, 0)),
                  pl.BlockSpec((D, tn), lambda j: (0, j)),
                  pl.BlockSpec((1, tn), lambda j: (0, j))],
        out_specs=pl.BlockSpec((B, tn), lambda j: (0, j)),
        out_shape=jax.ShapeDtypeStruct((B, N), F32),
        compiler_params=_params(("arbitrary",)),
        name="ada",
    )(c, w, b.reshape(1, N))


def _group_sumsq(z):
    n = z.shape[1]
    r = lax.broadcasted_iota(I32, (n, n), 0) // HEAD
    c = lax.broadcasted_iota(I32, (n, n), 1) // HEAD
    bd = jnp.where(r == c, 1.0, 0.0).astype(BF16)
    zz = z * z
    hi = zz.astype(BF16)
    lo = (zz - hi.astype(F32)).astype(BF16)
    return _dot(hi, bd) + _dot(lo, bd)


def _inproj_kernel(x_ref, mod_ref, g1_ref, wm_ref, wt_ref, gq_ref, gk_ref, gsq_ref, gsk_ref, gik_ref,
                   fdq_ref, fsq_ref,
                   dq_ref, dk_ref, dv_ref, sq_ref, sk_ref, sv_ref, iq_ref, ikl_ref, ikh_ref, iw_ref):
    x = x_ref[0]
    tm = x.shape[0]
    ms = jnp.mean(x * x, axis=-1, keepdims=True)
    y = x * lax.rsqrt(ms + EPS) * g1_ref[...]
    mod = mod_ref[0]
    h = y * (1.0 + mod[1:2]) + mod[0:1]
    hb = h.astype(BF16)

    lane = lax.broadcasted_iota(I32, (tm, LANES), 1)
    is_head = lane < HEAD
    kpos = pl.program_id(1) * tm + lax.broadcasted_iota(I32, (tm, LANES), 0)
    hi_digit = (kpos // POS_RADIX).astype(F32)
    lo_digit = (kpos % POS_RADIX).astype(F32)
    kfeat = jnp.where(lane < HEAD + N_SPLIT, hi_digit, jnp.where(lane < HEAD + 2 * N_SPLIT, lo_digit, 0.0))

    def plain(seg_idx, out_ref):
        out_ref[0] = _dot(hb, wm_ref[:, seg_idx * SEG:(seg_idx + 1) * SEG]).astype(BF16)

    def normed(seg_idx, g_ref, scale, feat_ref, out_ref):
        half = SEG // 2
        for i in range(2):
            lo = seg_idx * SEG + i * half
            z = _dot(hb, wm_ref[:, lo:lo + half])
            ss = _group_sumsq(z)
            zn = z * lax.rsqrt(ss * (1.0 / HEAD) + EPS) * (g_ref[:, i * half:(i + 1) * half] * scale)
            for g in range(half // LANES):
                zg = zn[:, g * LANES:(g + 1) * LANES]
                for odd in range(2):
                    idx = i * (half // HEAD) + 2 * g + odd
                    src = zg if odd == 0 else pltpu.roll(zg, HEAD, 1)
                    feat = kfeat if feat_ref is None else feat_ref[idx:idx + 1, :]
                    out_ref[0, :, idx * LANES:(idx + 1) * LANES] = jnp.where(is_head, src, feat).astype(BF16)

    normed(0, gq_ref, HEAD ** -0.5 * LOG2E, fdq_ref, dq_ref)
    normed(1, gk_ref, 1.0, None, dk_ref)
    plain(2, dv_ref)
    normed(3, gsq_ref, HEAD ** -0.5 * LOG2E, fsq_ref, sq_ref)
    normed(4, gsk_ref, 1.0, None, sk_ref)
    plain(5, sv_ref)
    plain(6, iq_ref)

    t = _dot(hb, wt_ref[...])
    ikraw = jnp.where(lane < IDX_DIM, t, 0.0)
    ss = jnp.sum(ikraw * ikraw, axis=-1, keepdims=True) * (1.0 / IDX_DIM)
    ikn = ikraw * lax.rsqrt(ss + EPS) * gik_ref[...]
    ikl_ref[0] = ikn.astype(BF16)
    ikh_ref[0] = pltpu.roll(ikn, IDX_DIM, 1).astype(BF16)
    iwraw = jnp.where((lane >= IDX_DIM) & (lane < IDX_DIM + IDX_HEADS), t, 0.0)
    iw_ref[0] = pltpu.roll(iwraw * (IDX_HEADS ** -0.5), LANES - IDX_DIM, 1) * (IDX_DIM ** -0.5)


def _inproj(x, mod3, g1, wm, wt, gq, gk, gsq, gsk, gik, fdq, fsq, tm):
    B, S, D = x.shape
    ns = S // tm
    tok = lambda b, i: (b, i, 0)
    const2 = lambda b, i: (0, 0)
    seg_spec = pl.BlockSpec((1, tm, SEG), tok)
    wide_spec = pl.BlockSpec((1, tm, WIDE), tok)
    lane_spec = pl.BlockSpec((1, tm, LANES), tok)
    seg_shape = jax.ShapeDtypeStruct((B, S, SEG), BF16)
    wide_shape = jax.ShapeDtypeStruct((B, S, WIDE), BF16)
    return pl.pallas_call(
        _inproj_kernel,
        grid=(B, ns),
        in_specs=[pl.BlockSpec((1, tm, D), tok),
                  pl.BlockSpec((1, 6, D), lambda b, i: (b, 0, 0)),
                  pl.BlockSpec((1, D), const2),
                  pl.BlockSpec(wm.shape, const2),
                  pl.BlockSpec(wt.shape, const2),
                  pl.BlockSpec((1, SEG), const2), pl.BlockSpec((1, SEG), const2),
                  pl.BlockSpec((1, SEG), const2), pl.BlockSpec((1, SEG), const2),
                  pl.BlockSpec((1, LANES), const2),
                  pl.BlockSpec((N_MAPS, LANES), const2), pl.BlockSpec((N_MAPS, LANES), const2)],
        out_specs=[wide_spec, wide_spec, seg_spec, wide_spec, wide_spec, seg_spec, seg_spec,
                   lane_spec, lane_spec, lane_spec],
        out_shape=[wide_shape, wide_shape, seg_shape, wide_shape, wide_shape, seg_shape, seg_shape,
                   jax.ShapeDtypeStruct((B, S, LANES), BF16), jax.ShapeDtypeStruct((B, S, LANES), BF16),
                   jax.ShapeDtypeStruct((B, S, LANES), F32)],
        compiler_params=_params(("parallel", "parallel")),
        name="inproj",
    )(x, mod3, g1, wm, wt, gq, gk, gsq, gsk, gik, fdq, fsq)


def _diff_kernel(q_ref, k_ref, v_ref, lam_ref, g_ref, o_ref, s_ref, m_ref, l_ref, acc_ref, *, tq):
    qi = pl.program_id(1)
    ki = pl.program_id(2)
    n_maps = 2 * DIFF_HEADS
    reps = tq // LANES

    @pl.when(ki == 0)
    def _():
        m_ref[...] = jnp.full(m_ref.shape, NEG_BIG, F32)
        l_ref[...] = jnp.zeros(l_ref.shape, F32)
        acc_ref[...] = jnp.zeros(acc_ref.shape, F32)

    def step(diag):
        if diag:
            row = lax.broadcasted_iota(I32, (tq, tq), 0)
            col = lax.broadcasted_iota(I32, (tq, tq), 1)
            causal_bias = jnp.where(col <= row, 0.0, NEG_BIG)
        m_olds, m_news = [], []
        for idx in range(n_maps):
            s = _nt(q_ref[0, :, idx * LANES:(idx + 1) * LANES], k_ref[0, :, idx * LANES:(idx + 1) * LANES])
            if diag:
                s = s + causal_bias
            s_ref[idx] = s
            m_old = m_ref[idx]
            m_olds.append(m_old)
            m_news.append(jnp.maximum(m_old, jnp.max(s, axis=-1, keepdims=True)))
        for idx in range(n_maps):
            h = idx // 2
            p = jnp.exp2(s_ref[idx] - _rep(m_news[idx], reps))
            alpha = jnp.exp2(m_olds[idx] - m_news[idx])
            psum = p[:, 0:LANES]
            for g in range(1, reps):
                psum = psum + p[:, g * LANES:(g + 1) * LANES]
            l_ref[idx] = alpha * l_ref[idx] + psum
            acc_ref[idx] = alpha * acc_ref[idx] + _dot(p.astype(BF16), v_ref[0, :, h * LANES:(h + 1) * LANES])
            m_ref[idx] = m_news[idx]

    @pl.when(ki < qi)
    def _():
        step(False)

    @pl.when(ki == qi)
    def _():
        step(True)
        lv = lam_ref[...]
        lam = (jnp.exp(jnp.sum(lv[0:1] * lv[1:2], axis=-1, keepdims=True))
               - jnp.exp(jnp.sum(lv[2:3] * lv[3:4], axis=-1, keepdims=True)) + LAM_INIT)
        for h in range(DIFF_HEADS):
            o1 = acc_ref[2 * h] / jnp.sum(l_ref[2 * h], axis=-1, keepdims=True)
            o2 = acc_ref[2 * h + 1] / jnp.sum(l_ref[2 * h + 1], axis=-1, keepdims=True)
            o = o1 - lam * o2
            ms = jnp.mean(o * o, axis=-1, keepdims=True)
            on = o * lax.rsqrt(ms + EPS) * g_ref[...]
            o_ref[0, :, h * LANES:(h + 1) * LANES] = (on * (1.0 - LAM_INIT)).astype(BF16)


def _diff_attention(dq, dk, dv, lamv, subln_g, tq):
    B, S, W = dq.shape
    V = dv.shape[2]
    nq = S // tq
    n_maps = 2 * DIFF_HEADS
    return pl.pallas_call(
        functools.partial(_diff_kernel, tq=tq),
        grid=(B, nq, nq),
        in_specs=[pl.BlockSpec((1, tq, W), lambda b, i, j: (b, i, 0)),
                  pl.BlockSpec((1, tq, W), lambda b, i, j: (b, jnp.minimum(i, j), 0)),
                  pl.BlockSpec((1, tq, V), lambda b, i, j: (b, jnp.minimum(i, j), 0)),
                  pl.BlockSpec(lamv.shape, lambda b, i, j: (0, 0)),
                  pl.BlockSpec((1, LANES), lambda b, i, j: (0, 0))],
        out_specs=pl.BlockSpec((1, tq, V), lambda b, i, j: (b, i, 0)),
        out_shape=jax.ShapeDtypeStruct((B, S, V), BF16),
        scratch_shapes=[pltpu.VMEM((n_maps, tq, tq), F32),
                        pltpu.VMEM((n_maps, tq, LANES), F32),
                        pltpu.VMEM((n_maps, tq, LANES), F32),
                        pltpu.VMEM((n_maps, tq, LANES), F32)],
        compiler_params=_params(("parallel", "parallel", "arbitrary")),
        name="diff_attention",
    )(dq, dk, dv, lamv, subln_g)


def _score_key(v):
    bits = lax.bitcast_convert_type(v, I32)
    return bits ^ ((bits >> 31) & 0x7FFFFFFF)


def _key_score(k):
    return lax.bitcast_convert_type(k ^ ((k >> 31) & 0x7FFFFFFF), F32)


_SUM_LANE = (LANES - 1, 0)
N_CAND = 12
CAND_ROWS = 32


def _dsa_kernel(iq_ref, iw_ref, ikl_ref, ikh_ref, q_ref, k_ref, v_ref, o_ref,
                key_ref, cand_ref, thr_ref, nties_ref, nge_ref, mb_ref, s_ref, m_ref, acc_ref, *, tq, tk, rs, topk):
    qi = pl.program_id(1)
    q0 = qi * tq
    nkc = (q0 + tq + tk - 1) // tk
    def causal(k0):
        row = q0 + lax.broadcasted_iota(I32, (tq, tk), 0)
        return lax.broadcasted_iota(I32, (tq, tk), 1) + k0 <= row

    low =lax.broadcasted_iota(I32, (tq, LANES), 1) < HEAD
    reps = tk // LANES
    iw = iw_ref[0]

    def score_chunk(kc, carry):
        k0 = pl.multiple_of(kc * tk, tk)
        ikl = ikl_ref[0, pl.ds(k0, tk), :]
        ikh = ikh_ref[0, pl.ds(k0, tk), :]
        sc = jnp.zeros((tq, tk), F32)
        for j in range(IDX_HEADS // 2):
            iqp = iq_ref[0, :, j * LANES:(j + 1) * LANES]
            sc = sc + iw[:, 2 * j:2 * j + 1] * jnp.maximum(_nt(iqp, ikl), 0.0)
            sc = sc + iw[:, 2 * j + 1:2 * j + 2] * jnp.maximum(_nt(iqp, ikh), 0.0)
        sc = jnp.where(sc == 0.0, 0.0, sc)
        sc = jnp.where(causal(k0), sc, -jnp.inf)
        key_ref[:, pl.ds(k0, tk)] = _score_key(sc)
        return carry

    lax.fori_loop(0, nkc, score_chunk, 0)

    groups = [slice(r0, r0 + rs) for r0 in range(0, tq, rs)]
    kf = float(topk)

    def candidates(g, carry):
        rows = pl.ds(pl.multiple_of(g * CAND_ROWS, CAND_ROWS), CAND_ROWS)

        def insert_chunk(kc, best):
            k0 = pl.multiple_of(kc * tk, tk)
            x = _key_score(key_ref[rows, pl.ds(k0, tk)])
            best = list(best)
            for lg in range(reps):
                v = x[:, lg * LANES:(lg + 1) * LANES]
                for i in range(N_CAND):
                    best[i], v = jnp.maximum(best[i], v), jnp.minimum(best[i], v)
            return tuple(best)

        lowest = jnp.full((CAND_ROWS, LANES), -jnp.inf, F32)
        best = lax.fori_loop(0, nkc, insert_chunk, (lowest,) * N_CAND)
        for i in range(N_CAND):
            cand_ref[rows, i * LANES:(i + 1) * LANES] = _score_key(best[i])
        return carry

    lax.fori_loop(0, tq // CAND_ROWS, candidates, 0)

    def count(ref, n_chunks, thrs, strict):
        accs = []
        for rows, thr in zip(groups, thrs):
            thr_t = _rep(thr, reps)

            def body(kc, acc, rows=rows, thr_t=thr_t):
                k0 = kc * tk if isinstance(kc, int) else pl.multiple_of(kc * tk, tk)
                keyc = ref[rows, pl.ds(k0, tk)]
                hit = jnp.where((keyc > thr_t) if strict else (keyc >= thr_t), 1.0, 0.0)
                for g in range(reps):
                    acc = acc + hit[:, g * LANES:(g + 1) * LANES]
                return acc

            acc = jnp.zeros((rs, LANES), F32)
            if isinstance(n_chunks, int):
                for kc in range(n_chunks):
                    acc = body(kc, acc)
            else:
                acc = lax.fori_loop(0, n_chunks, body, acc)
            accs.append(acc)
        return [jnp.broadcast_to(jnp.sum(acc, axis=-1, keepdims=True), (rs, LANES)) for acc in accs]

    def search(ref, n_chunks):
        def bit_step(i, tus):
            bit = jnp.left_shift(jnp.int32(1), 31 - i)
            cands = [tu | bit for tu in tus]
            cnts = count(ref, n_chunks, [c ^ INT_MIN for c in cands], False)
            return tuple(jnp.where(cnt >= kf, c, tu) for cnt, c, tu in zip(cnts, cands, tus))

        tus = lax.fori_loop(0, 32, bit_step, tuple(jnp.zeros((rs, LANES), I32) for _ in groups))
        return [tu ^ INT_MIN for tu in tus]

    def publish(thrs, n_above, n_reach):
        for rows, thr, n_gt, n_ge in zip(groups, thrs, n_above, n_reach):
            thr_ref[rows, :] = thr
            nties_ref[rows, :] = kf - n_gt
            nge_ref[rows, :] = n_ge

    cand_chunks = N_CAND * LANES // tk
    thrs = search(cand_ref, cand_chunks)
    inside = count(cand_ref, cand_chunks, thrs, False)
    publish(thrs, count(cand_ref, cand_chunks, thrs, True), inside)
    overall = count(key_ref, nkc, thrs, False)
    missed = [jnp.max(jnp.where(a != b, 1.0, 0.0)) for a, b in zip(inside, overall)]

    @pl.when(functools.reduce(jnp.maximum, missed) > 0.0)
    def _():
        full = search(key_ref, nkc)
        publish(full, count(key_ref, nkc, full, True), count(key_ref, nkc, full, False))

    m_ref[...] = jnp.full(m_ref.shape, NEG_BIG, F32)
    acc_ref[...] = jnp.zeros(acc_ref.shape, F32)
    tie_rows = jnp.max(jnp.where(nge_ref[...] != kf, 1.0, 0.0))

    def attend_chunk(kc, ties_before, ranked):
        k0 = pl.multiple_of(kc * tk, tk)
        keyc = key_ref[:, pl.ds(k0, tk)]
        thr_t = _rep(thr_ref[...], reps)
        if ranked:
            tri = jnp.where(lax.broadcasted_iota(I32, (tk, tk), 0) < lax.broadcasted_iota(I32, (tk, tk), 1),
                            1.0, 0.0).astype(BF16)
            eq = jnp.where(keyc == thr_t, 1.0, 0.0)
            rank = _dot(eq.astype(BF16), tri) + _rep(ties_before, reps)
            take = jnp.where(keyc > thr_t, 1.0, jnp.where(rank < _rep(nties_ref[...], reps), eq, 0.0))
            psum = eq[:, 0:LANES]
            for g in range(1, reps):
                psum = psum + eq[:, g * LANES:(g + 1) * LANES]
            ties_before = ties_before + jnp.sum(psum, axis=-1, keepdims=True)
            mb_ref[...] = jnp.where(causal(k0), jnp.where(take > 0.5, 0.0, NEG_BIG), NEG_BIG)
        else:
            mb_ref[...] = jnp.where(keyc >= thr_t, 0.0, NEG_BIG)
        mb = mb_ref[...]
        m_olds, m_news = [], []
        for hd in range(DSA_HEADS):
            s = _nt(q_ref[0, :, hd * LANES:(hd + 1) * LANES],
                    k_ref[0, pl.ds(k0, tk), hd * LANES:(hd + 1) * LANES]) + mb
            s_ref[hd] = s
            m_old = m_ref[hd]
            m_olds.append(m_old)
            m_news.append(jnp.maximum(m_old, jnp.max(s, axis=-1, keepdims=True)))
        for j in range(DSA_HEADS // 2):
            vp = v_ref[0, pl.ds(k0, tk), j * LANES:(j + 1) * LANES]
            vlane = lax.broadcasted_iota(I32, vp.shape, 1)
            for a in range(2):
                hd = 2 * j + a
                keep = (vlane < HEAD) if a == 0 else (vlane >= HEAD)
                va = jnp.where(vlane == _SUM_LANE[a], jnp.ones_like(vp), jnp.where(keep, vp, jnp.zeros_like(vp)))
                p = jnp.exp2((s_ref[hd] - _rep(m_news[hd], reps)).astype(BF16))
                acc_ref[hd] = jnp.exp2(m_olds[hd] - m_news[hd]) * acc_ref[hd] + _dot(p, va)
                m_ref[hd] = m_news[hd]
        return ties_before

    for ranked in (False, True):
        @pl.when((tie_rows > 0.0) == ranked)
        def _(ranked=ranked):
            lax.fori_loop(0, nkc, functools.partial(attend_chunk, ranked=ranked), jnp.zeros((tq, LANES), F32))

    lane = lax.broadcasted_iota(I32, (tq, LANES), 1)
    for j in range(DSA_HEADS // 2):
        oa, ob = acc_ref[2 * j], acc_ref[2 * j + 1]
        la = jnp.sum(jnp.where(lane == _SUM_LANE[0], oa, 0.0), axis=-1, keepdims=True)
        lb = jnp.sum(jnp.where(lane == _SUM_LANE[1], ob, 0.0), axis=-1, keepdims=True)
        o_ref[0, :, j * LANES:(j + 1) * LANES] = jnp.where(low, oa / la, ob / lb).astype(BF16)


def _dsa_attention(iq, iw, ikl, ikh, sq, sk, sv, tq, tk, rs, topk):
    B, S, W = sq.shape
    V = sv.shape[2]
    nq = S // tq
    assert (N_CAND * LANES) % tk == 0 and tq % CAND_ROWS == 0 and tq % rs == 0 and tk > topk
    qblk = lambda b, i: (b, i, 0)
    full = lambda b, i: (b, 0, 0)
    once = pl.Buffered(1)
    return pl.pallas_call(
        functools.partial(_dsa_kernel, tq=tq, tk=tk, rs=rs, topk=topk),
        grid=(B, nq),
        in_specs=[pl.BlockSpec((1, tq, iq.shape[2]), qblk),
                  pl.BlockSpec((1, tq, LANES), qblk),
                  pl.BlockSpec((1, S, LANES), full, pipeline_mode=once),
                  pl.BlockSpec((1, S, LANES), full, pipeline_mode=once),
                  pl.BlockSpec((1, tq, W), qblk),
                  pl.BlockSpec((1, S, W), full, pipeline_mode=once),
                  pl.BlockSpec((1, S, V), full, pipeline_mode=once)],
        out_specs=pl.BlockSpec((1, tq, V), qblk),
        out_shape=jax.ShapeDtypeStruct((B, S, V), BF16),
        scratch_shapes=[pltpu.VMEM((tq, S), I32),
                        pltpu.VMEM((tq, N_CAND * LANES), I32),
                        pltpu.VMEM((tq, LANES), I32),
                        pltpu.VMEM((tq, LANES), F32),
                        pltpu.VMEM((tq, LANES), F32),
                        pltpu.VMEM((tq, tk), F32),
                        pltpu.VMEM((DSA_HEADS, tq, tk), F32),
                        pltpu.VMEM((DSA_HEADS, tq, LANES), F32),
                        pltpu.VMEM((DSA_HEADS, tq, LANES), F32)],
        compiler_params=_params(("parallel", "arbitrary")),
        name="dsa_attention",
    )(iq, iw, ikl, ikh, sq, sk, sv)


def _mix_kernel(do_ref, so_ref, x_ref, mod_ref, g2_ref, wo1_ref, wo2_ref, wsg_ref, wsu_ref, wsd_ref,
                rwt_ref, base_ref, h2_ref, lg_ref):
    mix = _dot(do_ref[0], wo1_ref[...]) + _dot(so_ref[0], wo2_ref[...])
    mod = mod_ref[0]
    x1 = x_ref[0] + mod[2:3] * mix
    ms = jnp.mean(x1 * x1, axis=-1, keepdims=True)
    h2 = x1 * lax.rsqrt(ms + EPS) * g2_ref[...] * (1.0 + mod[4:5]) + mod[3:4]
    hb = h2.astype(BF16)
    gate = _dot(hb, wsg_ref[...])
    up = _dot(hb, wsu_ref[...])
    act = gate / (1.0 + jnp.exp(-gate)) * up
    shared = _dot(act.astype(BF16), wsd_ref[...])
    base_ref[0] = x1 + mod[5:6] * shared
    lg_ref[...] = _nt(rwt_ref[...], hb)
    hf = hb.astype(F32)
    tm = hf.shape[0]
    bits = lax.bitcast_convert_type(hf, I32)
    half = D_MODEL // 2
    packed = (bits[:, half:] & -65536) | lax.shift_right_logical(bits[:, :half], 16)
    for j in range(PACK_TILES):
        h2_ref[pl.ds(j, tm, stride=PACK_TILES), :] = packed[:, j * LANES:(j + 1) * LANES]


def _mix(diff_out, dsa_out, x, mod3, g2, wo1, wo2, wsg, wsu, wsd, rwt, tm):
    B, S, D = x.shape
    ns = S // tm
    T = B * S
    tok = lambda b, i: (b, i, 0)
    c2 = lambda b, i: (0, 0)
    return pl.pallas_call(
        _mix_kernel,
        grid=(B, ns),
        in_specs=[pl.BlockSpec((1, tm, SEG), tok), pl.BlockSpec((1, tm, SEG), tok),
                  pl.BlockSpec((1, tm, D), tok),
                  pl.BlockSpec((1, 6, D), lambda b, i: (b, 0, 0)),
                  pl.BlockSpec((1, D), c2),
                  pl.BlockSpec(wo1.shape, c2), pl.BlockSpec(wo2.shape, c2),
                  pl.BlockSpec(wsg.shape, c2), pl.BlockSpec(wsu.shape, c2), pl.BlockSpec(wsd.shape, c2),
                  pl.BlockSpec(rwt.shape, c2)],
        out_specs=[pl.BlockSpec((1, tm, D), tok),
                   pl.BlockSpec((tm * PACK_TILES, LANES), lambda b, i: (b * ns + i, 0)),
                   pl.BlockSpec((N_EXPERTS, tm), lambda b, i: (0, b * ns + i))],
        out_shape=[jax.ShapeDtypeStruct((B, S, D), F32),
                   jax.ShapeDtypeStruct((T * PACK_TILES, LANES), I32),
                   jax.ShapeDtypeStruct((N_EXPERTS, T), F32)],
        compiler_params=_params(("parallel", "parallel")),
        name="mix_shared_router",
    )(diff_out, dsa_out, x, mod3, g2, wo1, wo2, wsg, wsu, wsd, rwt)


def _first_max(v, idx, sentinel):
    m = jnp.max(v, axis=0, keepdims=True)
    i = jnp.min(jnp.where(v == m, idx, sentinel), axis=0, keepdims=True)
    return m, i


def _route_kernel(lg_ref, bias_ref, eidx_ref, gate_ref):
    lg = lg_ref[...]
    tt = lg.shape[1]
    scores = 1.0 / (1.0 + jnp.exp(-lg))
    biased = scores + bias_ref[...]
    gi = lax.broadcasted_iota(I32, (GROUP_SIZE, tt), 0).astype(F32)
    gscore = []
    for g in range(N_GROUPS):
        blk = biased[g * GROUP_SIZE:(g + 1) * GROUP_SIZE, :]
        m1, i1 = _first_max(blk, gi, float(GROUP_SIZE))
        m2 = jnp.max(jnp.where(gi == i1, -jnp.inf, blk), axis=0, keepdims=True)
        gscore.append(m1 + m2)
    gs = jnp.concatenate(gscore, axis=0)
    gidx = lax.broadcasted_iota(I32, (N_GROUPS, tt), 0).astype(F32)
    chosen = jnp.zeros((N_GROUPS, tt), F32)
    for _ in range(TOPK_GROUPS):
        _, ig = _first_max(gs, gidx, float(N_GROUPS))
        hit = gidx == ig
        chosen = jnp.where(hit, 1.0, chosen)
        gs = jnp.where(hit, -jnp.inf, gs)
    masked = jnp.concatenate(
        [jnp.where(chosen[g:g + 1, :] > 0.5, biased[g * GROUP_SIZE:(g + 1) * GROUP_SIZE, :], -jnp.inf)
         for g in range(N_GROUPS)], axis=0)
    ei = lax.broadcasted_iota(I32, (N_EXPERTS, tt), 0).astype(F32)
    ids, ws = [], []
    for _ in range(TOP_K):
        _, ie = _first_max(masked, ei, float(N_EXPERTS))
        hit = ei == ie
        ws.append(jnp.sum(jnp.where(hit, scores, 0.0), axis=0, keepdims=True))
        ids.append(ie)
        masked = jnp.where(hit, -jnp.inf, masked)
    w = jnp.concatenate(ws, axis=0)
    gate_ref[...] = w / jnp.sum(w, axis=0, keepdims=True) * ROUTED_SCALE
    eidx_ref[...] = jnp.concatenate(ids, axis=0).astype(I32)


def _route(logits_t, bias_col, tt):
    E, T = logits_t.shape
    return pl.pallas_call(
        _route_kernel,
        grid=(T // tt,),
        in_specs=[pl.BlockSpec((E, tt), lambda i: (0, i)),
                  pl.BlockSpec((E, 1), lambda i: (0, 0))],
        out_specs=[pl.BlockSpec((TOP_K, tt), lambda i: (0, i)),
                   pl.BlockSpec((TOP_K, tt), lambda i: (0, i))],
        out_shape=[jax.ShapeDtypeStruct((TOP_K, T), I32),
                   jax.ShapeDtypeStruct((TOP_K, T), F32)],
        compiler_params=_params(("parallel",)),
        name="route",
    )(logits_t, bias_col)


def _plan_kernel(eidx_ref, dest_ref, bexp_ref, nused_ref, cnt_col, cnt_row, slot_base, *, blk, nb_pad):
    ph = pl.program_id(0)
    i = pl.program_id(1)
    tt = eidx_ref.shape[1]
    eidx = eidx_ref[...]
    ei = lax.broadcasted_iota(I32, (N_EXPERTS, tt), 0)
    onehot = jnp.zeros((N_EXPERTS, tt), F32)
    for k in range(TOP_K):
        onehot = onehot + jnp.where(ei == eidx[k:k + 1, :], 1.0, 0.0)
    oh = onehot.astype(BF16)

    @pl.when((ph == 0) & (i == 0))
    def _():
        cnt_col[...] = jnp.zeros(cnt_col.shape, F32)
        cnt_row[...] = jnp.zeros(cnt_row.shape, F32)

    @pl.when(ph == 0)
    def _():
        cnt_col[...] += _dot(oh, jnp.ones((tt, LANES), BF16))
        cnt_row[...] += _nt(jnp.ones((8, tt), BF16), oh)

    @pl.when((ph == 1) & (i == 0))
    def _():
        inv = 1.0 / blk
        nb_col = jnp.floor((cnt_col[:, 0:1] + (blk - 1)) * inv)
        nb_row = jnp.floor((cnt_row[0:1, :] + (blk - 1)) * inv)
        r = lax.broadcasted_iota(I32, (N_EXPERTS, N_EXPERTS), 0)
        c = lax.broadcasted_iota(I32, (N_EXPERTS, N_EXPERTS), 1)
        bstart = jnp.sum(jnp.where(c < r, nb_row, 0.0), axis=-1, keepdims=True)
        bend = bstart + nb_col
        slot_base[...] = bstart * blk
        jb = lax.broadcasted_iota(I32, (N_EXPERTS, nb_pad), 1).astype(F32)
        be = jnp.sum(jnp.where(bend <= jb, 1.0, 0.0), axis=0, keepdims=True)
        bexp_ref[...] = jnp.minimum(be, N_EXPERTS - 1.0).astype(I32)
        nused_ref[...] = jnp.broadcast_to(jnp.sum(nb_row, axis=-1, keepdims=True), nused_ref.shape).astype(I32)

    @pl.when(ph == 1)
    def _():
        tri = jnp.where(lax.broadcasted_iota(I32, (tt, tt), 0) < lax.broadcasted_iota(I32, (tt, tt), 1),
                        1.0, 0.0).astype(BF16)
        slot = _dot(oh, tri) + slot_base[...]
        for k in range(TOP_K):
            dk = jnp.sum(jnp.where(ei == eidx[k:k + 1, :], slot, 0.0), axis=0, keepdims=True)
            dest_ref[k:k + 1, :] = dk.astype(I32)
        slot_base[...] += jnp.sum(onehot, axis=-1, keepdims=True)


def _plan(eidx, tt, blk, nb_pad):
    K, T = eidx.shape
    nt = T // tt
    return pl.pallas_call(
        functools.partial(_plan_kernel, blk=blk, nb_pad=nb_pad),
        grid=(2, nt),
        in_specs=[pl.BlockSpec((K, tt), lambda p, i: (0, i))],
        out_specs=[pl.BlockSpec((K, tt), lambda p, i: (0, i * p)),
                   pl.BlockSpec((1, nb_pad), lambda p, i: (0, 0)),
                   pl.BlockSpec((1, LANES), lambda p, i: (0, 0))],
        out_shape=[jax.ShapeDtypeStruct((K, T), I32),
                   jax.ShapeDtypeStruct((1, nb_pad), I32),
                   jax.ShapeDtypeStruct((1, LANES), I32)],
        scratch_shapes=[pltpu.VMEM((N_EXPERTS, LANES), F32),
                        pltpu.VMEM((8, N_EXPERTS), F32),
                        pltpu.VMEM((N_EXPERTS, 1), F32)],
        compiler_params=_params(("arbitrary", "arbitrary")),
        name="plan",
    )(eidx)


def _row(ref, r):
    return ref.at[pl.ds(pl.multiple_of(r * PACK_TILES, PACK_TILES), PACK_TILES), :]


def _dispatch_kernel(dest_ref, h_ref, xs_in_ref, xs_ref, sem):
    del xs_in_ref
    tt = h_ref.shape[0] // PACK_TILES

    def issue(t, c):
        for k in range(TOP_K):
            pltpu.make_async_copy(_row(h_ref, t), _row(xs_ref, dest_ref[k, t]), sem).start()
        return c

    lax.fori_loop(0, tt, issue, 0)

    def drain(t, c):
        for k in range(TOP_K):
            pltpu.make_async_copy(_row(h_ref, 0), _row(xs_ref, 0), sem).wait()
        return c

    lax.fori_loop(0, tt, drain, 0)


def _dispatch(dest, h2rows, xs_init, tt):
    T = h2rows.shape[0] // PACK_TILES
    return pl.pallas_call(
        _dispatch_kernel,
        grid=(T // tt,),
        in_specs=[pl.BlockSpec((TOP_K, tt), lambda i: (0, i), memory_space=pltpu.SMEM),
                  pl.BlockSpec((tt * PACK_TILES, LANES), lambda i: (i, 0)),
                  pl.BlockSpec(memory_space=pl.ANY)],
        out_specs=pl.BlockSpec(memory_space=pl.ANY),
        out_shape=jax.ShapeDtypeStruct(xs_init.shape, xs_init.dtype),
        scratch_shapes=[pltpu.SemaphoreType.DMA(())],
        input_output_aliases={2: 0},
        compiler_params=_params(("arbitrary",), has_side_effects=True),
        name="dispatch",
    )(dest, h2rows, xs_init)


def _experts_kernel(bexp_ref, nused_ref, xs_ref, wg_ref, wu_ref, wd_ref, y_ref, xb_ref, wgb_ref, wub_ref, wdb_ref):
    j = pl.program_id(0)

    @pl.when(j < nused_ref[0])
    def _():
        @pl.when((j == 0) | (bexp_ref[j] != bexp_ref[jnp.maximum(j - 1, 0)]))
        def _():
            wgb_ref[...] = wg_ref[0].astype(BF16)
            wub_ref[...] = wu_ref[0].astype(BF16)
            wdb_ref[...] = wd_ref[0].astype(BF16)

        blk = xb_ref.shape[0]
        half = D_MODEL // 2
        for c in range(PACK_TILES):
            w = xs_ref[pl.ds(c, blk, stride=PACK_TILES), :]
            cols = slice(c * LANES, (c + 1) * LANES)
            xb_ref[:, cols] = lax.bitcast_convert_type(w << 16, F32).astype(BF16)
            xb_ref[:, half + c * LANES:half + (c + 1) * LANES] = lax.bitcast_convert_type(w & -65536, F32).astype(BF16)
        xb = xb_ref[...]
        gate = _dot(xb, wgb_ref[...])
        up = _dot(xb, wub_ref[...])
        act = gate / (1.0 + jnp.exp(-gate)) * up
        y = _dot(act.astype(BF16), wdb_ref[...])
        bits = lax.bitcast_convert_type(y.astype(BF16).astype(F32), I32)
        packed = (bits[:, half:] & -65536) | lax.shift_right_logical(bits[:, :half], 16)
        for c in range(PACK_TILES):
            y_ref[pl.ds(c, blk, stride=PACK_TILES), :] = packed[:, c * LANES:(c + 1) * LANES]


def _experts(bexp, nused, xs, wg, wu, wd, blk, n_blocks):
    live = lambda j, be, nu: jnp.minimum(j, nu[0] - 1)
    row_spec = pl.BlockSpec((blk * PACK_TILES, LANES), lambda j, be, nu: (live(j, be, nu), 0))
    wspec = lambda w: pl.BlockSpec((1,) + w.shape[1:], lambda j, be, nu: (be[live(j, be, nu)], 0, 0))
    return pl.pallas_call(
        _experts_kernel,
        grid_spec=pltpu.PrefetchScalarGridSpec(
            num_scalar_prefetch=2,
            grid=(n_blocks,),
            in_specs=[row_spec, wspec(wg), wspec(wu), wspec(wd)],
            out_specs=row_spec,
            scratch_shapes=[pltpu.VMEM((blk, D_MODEL), BF16),
                            pltpu.VMEM(wg.shape[1:], BF16), pltpu.VMEM(wu.shape[1:], BF16),
                            pltpu.VMEM(wd.shape[1:], BF16)]),
        out_shape=jax.ShapeDtypeStruct(xs.shape, I32),
        compiler_params=_params(("arbitrary",)),
        name="experts",
    )(bexp, nused, xs, wg, wu, wd)


def _combine_kernel(dest_ref, gate_ref, base_ref, mod_ref, y_ref, o_ref, buf, sem):
    tt = base_ref.shape[1]

    def issue(t, c):
        for k in range(TOP_K):
            pltpu.make_async_copy(_row(y_ref, dest_ref[k, t]), _row(buf, k * tt + t), sem).start()
        return c

    lax.fori_loop(0, tt, issue, 0)

    def drain(t, c):
        for k in range(TOP_K):
            pltpu.make_async_copy(_row(y_ref, 0), _row(buf, 0), sem).wait()
        return c

    lax.fori_loop(0, tt, drain, 0)

    gates = gate_ref[...]
    g2 = mod_ref[0][5:6]
    half = D_MODEL // 2
    for j in range(PACK_TILES):
        lo = jnp.zeros((tt, LANES), F32)
        hi = jnp.zeros((tt, LANES), F32)
        for k in range(TOP_K):
            w = buf[pl.ds(k * tt * PACK_TILES + j, tt, stride=PACK_TILES), :]
            lo = lo + gates[:, k:k + 1] * lax.bitcast_convert_type(w << 16, F32)
            hi = hi + gates[:, k:k + 1] * lax.bitcast_convert_type(w & -65536, F32)
        for off, acc in ((0, lo), (half, hi)):
            cols = slice(off + j * LANES, off + (j + 1) * LANES)
            o_ref[0, :, cols] = base_ref[0, :, cols] + g2[:, cols] * acc


def _combine(dest, gates_tk, base, mod3, y, tt):
    B, S, D = base.shape
    ns = S // tt
    return pl.pallas_call(
        _combine_kernel,
        grid=(B, ns),
        in_specs=[pl.BlockSpec((TOP_K, tt), lambda b, i: (0, b * ns + i), memory_space=pltpu.SMEM),
                  pl.BlockSpec((tt, TOP_K), lambda b, i: (b * ns + i, 0)),
                  pl.BlockSpec((1, tt, D), lambda b, i: (b, i, 0)),
                  pl.BlockSpec((1, 6, D), lambda b, i: (b, 0, 0)),
                  pl.BlockSpec(memory_space=pl.ANY)],
        out_specs=pl.BlockSpec((1, tt, D), lambda b, i: (b, i, 0)),
        out_shape=jax.ShapeDtypeStruct((B, S, D), F32),
        scratch_shapes=[pltpu.VMEM((TOP_K * tt * PACK_TILES, LANES), I32),
                        pltpu.SemaphoreType.DMA(())],
        compiler_params=_params(("arbitrary", "arbitrary")),
        name="combine",
    )(dest, gates_tk, base, mod3, y)


def _alibi_q_features(n_heads, maps_per_head):
    slopes = 2.0 ** (-8.0 * jnp.arange(1, n_heads + 1, dtype=F32) / n_heads)
    c = jnp.repeat(slopes, maps_per_head) * LOG2E * POS_RADIX
    pieces = []
    rest = c
    for _ in range(N_SPLIT):
        p = rest.astype(BF16).astype(F32)
        pieces.append(p)
        rest = rest - p
    hi = jnp.stack(pieces, axis=1)
    feat = jnp.concatenate([hi, hi / POS_RADIX], axis=1)
    return jnp.pad(feat, ((0, 0), (HEAD, LANES - HEAD - 2 * N_SPLIT)))


def kernel(x, c, ada_w, ada_b, norm1_g, norm2_g, w_in, diff_q_norm_g, diff_k_norm_g, lam_q1, lam_k1, lam_q2, lam_k2, diff_subln_g, dsa_q_norm_g, dsa_k_norm_g, idx_k_norm_g, w_out, router_w, router_bias, exp_w_gate, exp_w_up, exp_w_down, shared_w_gate, shared_w_up, shared_w_down):
    B, S, D = x.shape
    assert D == D_MODEL and ada_w.shape[0] == 1 and S <= POS_RADIX * 128
    T = B * S
    topk = min(DSA_TOPK, S // 4)
    tm = min(512, S)
    tq_diff = min(512, S)
    tq_dsa = min(256, S)
    tk_dsa = min(512, S)
    rs_dsa = min(128, tq_dsa)
    tt_route = min(512, T)
    tt_move = min(512, S)
    blk = 512
    n_blocks = (T * TOP_K) // blk + N_EXPERTS
    nb_pad = -(-n_blocks // LANES) * LANES

    n_main = 7 * SEG
    wm = w_in[0, :, :n_main].astype(BF16)
    wt = jnp.pad(w_in[0, :, n_main:], ((0, 0), (0, LANES - (IDX_DIM + IDX_HEADS)))).astype(BF16)
    tile8 = lambda g: jnp.tile(g[0], SEG // g.shape[1]).reshape(1, SEG)
    gik = jnp.pad(idx_k_norm_g[0], (0, LANES - IDX_DIM)).reshape(1, LANES)
    lamv = jnp.concatenate([lam_q1, lam_k1, lam_q2, lam_k2], axis=0)
    wo1 = w_out[0, :SEG].astype(BF16)
    wo2 = w_out[0, SEG:].astype(BF16)
    rwt = router_w[0].T.astype(BF16)
    wg, wu, wd = exp_w_gate[0], exp_w_up[0], exp_w_down[0]

    mod3 = _ada(c, ada_w[0], ada_b[0]).reshape(B, 6, D)

    dq, dk, dv, sq, sk, sv, iq, ikl, ikh, iw = _inproj(
        x, mod3, norm1_g, wm, wt, tile8(diff_q_norm_g), tile8(diff_k_norm_g),
        tile8(dsa_q_norm_g), tile8(dsa_k_norm_g), gik,
        _alibi_q_features(DIFF_HEADS, 2), _alibi_q_features(DSA_HEADS, 1), tm)

    diff_out = _diff_attention(dq, dk, dv, lamv, diff_subln_g, tq_diff)
    dsa_out = _dsa_attention(iq, iw, ikl, ikh, sq, sk, sv, tq_dsa, tk_dsa, rs_dsa, topk)

    base, h2rows, logits_t = _mix(diff_out, dsa_out, x, mod3, norm2_g, wo1, wo2,
                                  shared_w_gate[0].astype(BF16), shared_w_up[0].astype(BF16),
                                  shared_w_down[0].astype(BF16), rwt, tm)

    eidx, gates = _route(logits_t, router_bias[0].reshape(N_EXPERTS, 1), tt_route)
    dest, bexp, nused = _plan(eidx, tt_route, blk, nb_pad)

    xs = _dispatch(dest, h2rows, jnp.zeros((n_blocks * blk * PACK_TILES, LANES), I32), tt_move)
    y = _experts(bexp.reshape(nb_pad), nused[0, :1], xs, wg, wu, wd, blk, n_blocks)
    return _combine(dest, gates.T, base, mod3, y, tt_move)
```

```python
import functools
import math

import jax
import jax.numpy as jnp
from jax import lax
from jax.experimental import pallas as pl
from jax.experimental.pallas import tpu as pltpu

F32 = jnp.float32
BF16 = jnp.bfloat16
I32 = jnp.int32

D_MODEL = 1024
DIFF_HEADS = 4
DSA_HEADS = 8
IDX_HEADS = 8
IDX_DIM = 64
DSA_TOPK = 256
N_EXPERTS = 256
TOP_K = 8
N_GROUPS = 8
GROUP_SIZE = N_EXPERTS // N_GROUPS
TOPK_GROUPS = 4
ROUTED_SCALE = 2.5
EPS = 1e-6
LAM_INIT = 0.2

LANES = 128
PACK_TILES = D_MODEL // (2 * LANES)
SEG = 512
HEAD = 64
N_MAPS = SEG // HEAD
WIDE = N_MAPS * LANES
POS_RADIX = 64
N_SPLIT = 3
NEG_BIG = -1e30
INT_MIN = -2147483648
LOG2E = math.log2(math.e)
VMEM_LIMIT = 56 * 1024 * 1024

NT_DIMS = (((1,), (1,)), ((), ()))


def _nt(a, b):
    return lax.dot_general(a, b, NT_DIMS, preferred_element_type=F32)


def _dot(a, b):
    return jnp.dot(a, b, preferred_element_type=F32)


def _rep(x, reps):
    return jnp.concatenate([x] * reps, axis=1)


def _params(sem, vmem=VMEM_LIMIT, **kw):
    return pltpu.CompilerParams(dimension_semantics=sem, vmem_limit_bytes=vmem, **kw)


def _ada_kernel(c_ref, w_ref, b_ref, o_ref):
    c = c_ref[...]
    s = c / (1.0 + jnp.exp(-c))
    o_ref[...] = jnp.dot(s, w_ref[...], preferred_element_type=F32,
                         precision=lax.Precision.HIGHEST) + b_ref[...]


def _ada(c, w, b):
    B, D = c.shape
    N = w.shape[1]
    tn = D
    return pl.pallas_call(
        _ada_kernel,
        grid=(N // tn,),
        in_specs=[pl.BlockSpec((B, D), lambda j: (0, 0)),
                  pl.BlockSpec((D, tn), lambda j: (0, j)),
                  pl.BlockSpec((1, tn), lambda j: (0, j))],
        out_specs=pl.BlockSpec((B, tn), lambda j: (0, j)),
        out_shape=jax.ShapeDtypeStruct((B, N), F32),
        compiler_params=_params(("arbitrary",)),
        name="ada",
    )(c, w, b.reshape(1, N))


def _group_sumsq(z):
    n = z.shape[1]
    r = lax.broadcasted_iota(I32, (n, n), 0) // HEAD
    c = lax.broadcasted_iota(I32, (n, n), 1) // HEAD
    bd = jnp.where(r == c, 1.0, 0.0).astype(BF16)
    zz = z * z
    hi = zz.astype(BF16)
    lo = (zz - hi.astype(F32)).astype(BF16)
    return _dot(hi, bd) + _dot(lo, bd)


def _inproj_kernel(x_ref, mod_ref, g1_ref, wm_ref, wt_ref, gq_ref, gk_ref, gsq_ref, gsk_ref, gik_ref,
                   fdq_ref, fsq_ref,
                   dq_ref, dk_ref, dv_ref, sq_ref, sk_ref, sv_ref, iq_ref, ikl_ref, ikh_ref, iw_ref):
    x = x_ref[0]
    tm = x.shape[0]
    ms = jnp.mean(x * x, axis=-1, keepdims=True)
    y = x * lax.rsqrt(ms + EPS) * g1_ref[...]
    mod = mod_ref[0]
    h = y * (1.0 + mod[1:2]) + mod[0:1]
    hb = h.astype(BF16)

    lane = lax.broadcasted_iota(I32, (tm, LANES), 1)
    is_head = lane < HEAD
    kpos = pl.program_id(1) * tm + lax.broadcasted_iota(I32, (tm, LANES), 0)
    hi_digit = (kpos // POS_RADIX).astype(F32)
    lo_digit = (kpos % POS_RADIX).astype(F32)
    kfeat = jnp.where(lane < HEAD + N_SPLIT, hi_digit, jnp.where(lane < HEAD + 2 * N_SPLIT, lo_digit, 0.0))

    def plain(seg_idx, out_ref):
        out_ref[0] = _dot(hb, wm_ref[:, seg_idx * SEG:(seg_idx + 1) * SEG]).astype(BF16)

    def normed(seg_idx, g_ref, scale, feat_ref, out_ref):
        half = SEG // 2
        for i in range(2):
            lo = seg_idx * SEG + i * half
            z = _dot(hb, wm_ref[:, lo:lo + half])
            ss = _group_sumsq(z)
            zn = z * lax.rsqrt(ss * (1.0 / HEAD) + EPS) * (g_ref[:, i * half:(i + 1) * half] * scale)
            for g in range(half // LANES):
                zg = zn[:, g * LANES:(g + 1) * LANES]
                for odd in range(2):
                    idx = i * (half // HEAD) + 2 * g + odd
                    src = zg if odd == 0 else pltpu.roll(zg, HEAD, 1)
                    feat = kfeat if feat_ref is None else feat_ref[idx:idx + 1, :]
                    out_ref[0, :, idx * LANES:(idx + 1) * LANES] = jnp.where(is_head, src, feat).astype(BF16)

    normed(0, gq_ref, HEAD ** -0.5 * LOG2E, fdq_ref, dq_ref)
    normed(1, gk_ref, 1.0, None, dk_ref)
    plain(2, dv_ref)
    normed(3, gsq_ref, HEAD ** -0.5 * LOG2E, fsq_ref, sq_ref)
    normed(4, gsk_ref, 1.0, None, sk_ref)
    plain(5, sv_ref)
    plain(6, iq_ref)

    t = _dot(hb, wt_ref[...])
    ikraw = jnp.where(lane < IDX_DIM, t, 0.0)
    ss = jnp.sum(ikraw * ikraw, axis=-1, keepdims=True) * (1.0 / IDX_DIM)
    ikn = ikraw * lax.rsqrt(ss + EPS) * gik_ref[...]
    ikl_ref[0] = ikn.astype(BF16)
    ikh_ref[0] = pltpu.roll(ikn, IDX_DIM, 1).astype(BF16)
    iwraw = jnp.where((lane >= IDX_DIM) & (lane < IDX_DIM + IDX_HEADS), t, 0.0)
    iw_ref[0] = pltpu.roll(iwraw * (IDX_HEADS ** -0.5), LANES - IDX_DIM, 1) * (IDX_DIM ** -0.5)


def _inproj(x, mod3, g1, wm, wt, gq, gk, gsq, gsk, gik, fdq, fsq, tm):
    B, S, D = x.shape
    ns = S // tm
    tok = lambda b, i: (b, i, 0)
    const2 = lambda b, i: (0, 0)
    seg_spec = pl.BlockSpec((1, tm, SEG), tok)
    wide_spec = pl.BlockSpec((1, tm, WIDE), tok)
    lane_spec = pl.BlockSpec((1, tm, LANES), tok)
    seg_shape = jax.ShapeDtypeStruct((B, S, SEG), BF16)
    wide_shape = jax.ShapeDtypeStruct((B, S, WIDE), BF16)
    return pl.pallas_call(
        _inproj_kernel,
        grid=(B, ns),
        in_specs=[pl.BlockSpec((1, tm, D), tok),
                  pl.BlockSpec((1, 6, D), lambda b, i: (b, 0, 0)),
                  pl.BlockSpec((1, D), const2),
                  pl.BlockSpec(wm.shape, const2),
                  pl.BlockSpec(wt.shape, const2),
                  pl.BlockSpec((1, SEG), const2), pl.BlockSpec((1, SEG), const2),
                  pl.BlockSpec((1, SEG), const2), pl.BlockSpec((1, SEG), const2),
                  pl.BlockSpec((1, LANES), const2),
                  pl.BlockSpec((N_MAPS, LANES), const2), pl.BlockSpec((N_MAPS, LANES), const2)],
        out_specs=[wide_spec, wide_spec, seg_spec, wide_spec, wide_spec, seg_spec, seg_spec,
                   lane_spec, lane_spec, lane_spec],
        out_shape=[wide_shape, wide_shape, seg_shape, wide_shape, wide_shape, seg_shape, seg_shape,
                   jax.ShapeDtypeStruct((B, S, LANES), BF16), jax.ShapeDtypeStruct((B, S, LANES), BF16),
                   jax.ShapeDtypeStruct((B, S, LANES), F32)],
        compiler_params=_params(("parallel", "parallel")),
        name="inproj",
    )(x, mod3, g1, wm, wt, gq, gk, gsq, gsk, gik, fdq, fsq)


def _diff_kernel(q_ref, k_ref, v_ref, lam_ref, g_ref, o_ref, s_ref, m_ref, l_ref, acc_ref, *, tq):
    qi = pl.program_id(1)
    ki = pl.program_id(2)
    n_maps = 2 * DIFF_HEADS
    reps = tq // LANES

    @pl.when(ki == 0)
    def _():
        m_ref[...] = jnp.full(m_ref.shape, NEG_BIG, F32)
        l_ref[...] = jnp.zeros(l_ref.shape, F32)
        acc_ref[...] = jnp.zeros(acc_ref.shape, F32)

    def step(diag):
        if diag:
            row = lax.broadcasted_iota(I32, (tq, tq), 0)
            col = lax.broadcasted_iota(I32, (tq, tq), 1)
            causal_bias = jnp.where(col <= row, 0.0, NEG_BIG)
        m_olds, m_news = [], []
        for idx in range(n_maps):
            s = _nt(q_ref[0, :, idx * LANES:(idx + 1) * LANES], k_ref[0, :, idx * LANES:(idx + 1) * LANES])
            if diag:
                s = s + causal_bias
            s_ref[idx] = s
            m_old = m_ref[idx]
            m_olds.append(m_old)
            m_news.append(jnp.maximum(m_old, jnp.max(s, axis=-1, keepdims=True)))
        for idx in range(n_maps):
            h = idx // 2
            p = jnp.exp2(s_ref[idx] - _rep(m_news[idx], reps))
            alpha = jnp.exp2(m_olds[idx] - m_news[idx])
            psum = p[:, 0:LANES]
            for g in range(1, reps):
                psum = psum + p[:, g * LANES:(g + 1) * LANES]
            l_ref[idx] = alpha * l_ref[idx] + psum
            acc_ref[idx] = alpha * acc_ref[idx] + _dot(p.astype(BF16), v_ref[0, :, h * LANES:(h + 1) * LANES])
            m_ref[idx] = m_news[idx]

    @pl.when(ki < qi)
    def _():
        step(False)

    @pl.when(ki == qi)
    def _():
        step(True)
        lv = lam_ref[...]
        lam = (jnp.exp(jnp.sum(lv[0:1] * lv[1:2], axis=-1, keepdims=True))
               - jnp.exp(jnp.sum(lv[2:3] * lv[3:4], axis=-1, keepdims=True)) + LAM_INIT)
        for h in range(DIFF_HEADS):
            o1 = acc_ref[2 * h] / jnp.sum(l_ref[2 * h], axis=-1, keepdims=True)
            o2 = acc_ref[2 * h + 1] / jnp.sum(l_ref[2 * h + 1], axis=-1, keepdims=True)
            o = o1 - lam * o2
            ms = jnp.mean(o * o, axis=-1, keepdims=True)
            on = o * lax.rsqrt(ms + EPS) * g_ref[...]
            o_ref[0, :, h * LANES:(h + 1) * LANES] = (on * (1.0 - LAM_INIT)).astype(BF16)


def _diff_attention(dq, dk, dv, lamv, subln_g, tq):
    B, S, W = dq.shape
    V = dv.shape[2]
    nq = S // tq
    n_maps = 2 * DIFF_HEADS
    return pl.pallas_call(
        functools.partial(_diff_kernel, tq=tq),
        grid=(B, nq, nq),
        in_specs=[pl.BlockSpec((1, tq, W), lambda b, i, j: (b, i, 0)),
                  pl.BlockSpec((1, tq, W), lambda b, i, j: (b, jnp.minimum(i, j), 0)),
                  pl.BlockSpec((1, tq, V), lambda b, i, j: (b, jnp.minimum(i, j), 0)),
                  pl.BlockSpec(lamv.shape, lambda b, i, j: (0, 0)),
                  pl.BlockSpec((1, LANES), lambda b, i, j: (0, 0))],
        out_specs=pl.BlockSpec((1, tq, V), lambda b, i, j: (b, i, 0)),
        out_shape=jax.ShapeDtypeStruct((B, S, V), BF16),
        scratch_shapes=[pltpu.VMEM((n_maps, tq, tq), F32),
                        pltpu.VMEM((n_maps, tq, LANES), F32),
                        pltpu.VMEM((n_maps, tq, LANES), F32),
                        pltpu.VMEM((n_maps, tq, LANES), F32)],
        compiler_params=_params(("parallel", "parallel", "arbitrary")),
        name="diff_attention",
    )(dq, dk, dv, lamv, subln_g)


def _score_key(v):
    bits = lax.bitcast_convert_type(v, I32)
    return bits ^ ((bits >> 31) & 0x7FFFFFFF)


def _key_score(k):
    return lax.bitcast_convert_type(k ^ ((k >> 31) & 0x7FFFFFFF), F32)


_SUM_LANE = (LANES - 1, 0)
N_CAND = 12
CAND_ROWS = 32


def _dsa_kernel(iq_ref, iw_ref, ikl_ref, ikh_ref, q_ref, k_ref, v_ref, o_ref,
                key_ref, cand_ref, thr_ref, nties_ref, nge_ref, mb_ref, s_ref, m_ref, acc_ref, *, tq, tk, rs, topk):
    qi = pl.program_id(1)
    q0 = qi * tq
    nkc = (q0 + tq + tk - 1) // tk
    def causal(k0):
        row = q0 + lax.broadcasted_iota(I32, (tq, tk), 0)
        return lax.broadcasted_iota(I32, (tq, tk), 1) + k0 <= row

    low =lax.broadcasted_iota(I32, (tq, LANES), 1) < HEAD
    reps = tk // LANES
    iw = iw_ref[0]

    def score_chunk(kc, carry):
        k0 = pl.multiple_of(kc * tk, tk)
        ikl = ikl_ref[0, pl.ds(k0, tk), :]
        ikh = ikh_ref[0, pl.ds(k0, tk), :]
        sc = jnp.zeros((tq, tk), F32)
        for j in range(IDX_HEADS // 2):
            iqp = iq_ref[0, :, j * LANES:(j + 1) * LANES]
            sc = sc + iw[:, 2 * j:2 * j + 1] * jnp.maximum(_nt(iqp, ikl), 0.0)
            sc = sc + iw[:, 2 * j + 1:2 * j + 2] * jnp.maximum(_nt(iqp, ikh), 0.0)
        sc = jnp.where(sc == 0.0, 0.0, sc)
        sc = jnp.where(causal(k0), sc, -jnp.inf)
        key_ref[:, pl.ds(k0, tk)] = _score_key(sc)
        return carry

    lax.fori_loop(0, nkc, score_chunk, 0)

    groups = [slice(r0, r0 + rs) for r0 in range(0, tq, rs)]
    kf = float(topk)

    def candidates(g, carry):
        rows = pl.ds(pl.multiple_of(g * CAND_ROWS, CAND_ROWS), CAND_ROWS)

        def insert_chunk(kc, best):
            k0 = pl.multiple_of(kc * tk, tk)
            x = _key_score(key_ref[rows, pl.ds(k0, tk)])
            best = list(best)
            for lg in range(reps):
                v = x[:, lg * LANES:(lg + 1) * LANES]
                for i in range(N_CAND):
                    best[i], v = jnp.maximum(best[i], v), jnp.minimum(best[i], v)
            return tuple(best)

        lowest = jnp.full((CAND_ROWS, LANES), -jnp.inf, F32)
        best = lax.fori_loop(0, nkc, insert_chunk, (lowest,) * N_CAND)
        for i in range(N_CAND):
            cand_ref[rows, i * LANES:(i + 1) * LANES] = _score_key(best[i])
        return carry

    lax.fori_loop(0, tq // CAND_ROWS, candidates, 0)

    def count(ref, n_chunks, thrs, strict):
        accs = []
        for rows, thr in zip(groups, thrs):
            thr_t = _rep(thr, reps)

            def body(kc, acc, rows=rows, thr_t=thr_t):
                k0 = kc * tk if isinstance(kc, int) else pl.multiple_of(kc * tk, tk)
                keyc = ref[rows, pl.ds(k0, tk)]
                hit = jnp.where((keyc > thr_t) if strict else (keyc >= thr_t), 1.0, 0.0)
                for g in range(reps):
                    acc = acc + hit[:, g * LANES:(g + 1) * LANES]
                return acc

            acc = jnp.zeros((rs, LANES), F32)
            if isinstance(n_chunks, int):
                for kc in range(n_chunks):
                    acc = body(kc, acc)
            else:
                acc = lax.fori_loop(0, n_chunks, body, acc)
            accs.append(acc)
        return [jnp.broadcast_to(jnp.sum(acc, axis=-1, keepdims=True), (rs, LANES)) for acc in accs]

    def search(ref, n_chunks):
        def bit_step(i, tus):
            bit = jnp.left_shift(jnp.int32(1), 31 - i)
            cands = [tu | bit for tu in tus]
            cnts = count(ref, n_chunks, [c ^ INT_MIN for c in cands], False)
            return tuple(jnp.where(cnt >= kf, c, tu) for cnt, c, tu in zip(cnts, cands, tus))

        tus = lax.fori_loop(0, 32, bit_step, tuple(jnp.zeros((rs, LANES), I32) for _ in groups))
        return [tu ^ INT_MIN for tu in tus]

    def publish(thrs, n_above, n_reach):
        for rows, thr, n_gt, n_ge in zip(groups, thrs, n_above, n_reach):
            thr_ref[rows, :] = thr
            nties_ref[rows, :] = kf - n_gt
            nge_ref[rows, :] = n_ge

    cand_chunks = N_CAND * LANES // tk
    thrs = search(cand_ref, cand_chunks)
    inside = count(cand_ref, cand_chunks, thrs, False)
    publish(thrs, count(cand_ref, cand_chunks, thrs, True), inside)
    overall = count(key_ref, nkc, thrs, False)
    missed = [jnp.max(jnp.where(a != b, 1.0, 0.0)) for a, b in zip(inside, overall)]

    @pl.when(functools.reduce(jnp.maximum, missed) > 0.0)
    def _():
        full = search(key_ref, nkc)
        publish(full, count(key_ref, nkc, full, True), count(key_ref, nkc, full, False))

    m_ref[...] = jnp.full(m_ref.shape, NEG_BIG, F32)
    acc_ref[...] = jnp.zeros(acc_ref.shape, F32)
    tie_rows = jnp.max(jnp.where(nge_ref[...] != kf, 1.0, 0.0))

    def attend_chunk(kc, ties_before, ranked):
        k0 = pl.multiple_of(kc * tk, tk)
        keyc = key_ref[:, pl.ds(k0, tk)]
        thr_t = _rep(thr_ref[...], reps)
        if ranked:
            tri = jnp.where(lax.broadcasted_iota(I32, (tk, tk), 0) < lax.broadcasted_iota(I32, (tk, tk), 1),
                            1.0, 0.0).astype(BF16)
            eq = jnp.where(keyc == thr_t, 1.0, 0.0)
            rank = _dot(eq.astype(BF16), tri) + _rep(ties_before, reps)
            take = jnp.where(keyc > thr_t, 1.0, jnp.where(rank < _rep(nties_ref[...], reps), eq, 0.0))
            psum = eq[:, 0:LANES]
            for g in range(1, reps):
                psum = psum + eq[:, g * LANES:(g + 1) * LANES]
            ties_before = ties_before + jnp.sum(psum, axis=-1, keepdims=True)
            mb_ref[...] = jnp.where(causal(k0), jnp.where(take > 0.5, 0.0, NEG_BIG), NEG_BIG)
        else:
            mb_ref[...] = jnp.where(keyc >= thr_t, 0.0, NEG_BIG)
        mb = mb_ref[...]
        m_olds, m_news = [], []
        for hd in range(DSA_HEADS):
            s = _nt(q_ref[0, :, hd * LANES:(hd + 1) * LANES],
                    k_ref[0, pl.ds(k0, tk), hd * LANES:(hd + 1) * LANES]) + mb
            s_ref[hd] = s
            m_old = m_ref[hd]
            m_olds.append(m_old)
            m_news.append(jnp.maximum(m_old, jnp.max(s, axis=-1, keepdims=True)))
        for j in range(DSA_HEADS // 2):
            vp = v_ref[0, pl.ds(k0, tk), j * LANES:(j + 1) * LANES]
            vlane = lax.broadcasted_iota(I32, vp.shape, 1)
            for a in range(2):
                hd = 2 * j + a
                keep = (vlane < HEAD) if a == 0 else (vlane >= HEAD)
                va = jnp.where(vlane == _SUM_LANE[a], jnp.ones_like(vp), jnp.where(keep, vp, jnp.zeros_like(vp)))
                p = jnp.exp2((s_ref[hd] - _rep(m_news[hd], reps)).astype(BF16))
                acc_ref[hd] = jnp.exp2(m_olds[hd] - m_news[hd]) * acc_ref[hd] + _dot(p, va)
                m_ref[hd] = m_news[hd]
        return ties_before

    for ranked in (False, True):
        @pl.when((tie_rows > 0.0) == ranked)
        def _(ranked=ranked):
            lax.fori_loop(0, nkc, functools.partial(attend_chunk, ranked=ranked), jnp.zeros((tq, LANES), F32))

    lane = lax.broadcasted_iota(I32, (tq, LANES), 1)
    for j in range(DSA_HEADS // 2):
        oa, ob = acc_ref[2 * j], acc_ref[2 * j + 1]
        la = jnp.sum(jnp.where(lane == _SUM_LANE[0], oa, 0.0), axis=-1, keepdims=True)
        lb = jnp.sum(jnp.where(lane == _SUM_LANE[1], ob, 0.0), axis=-1, keepdims=True)
        o_ref[0, :, j * LANES:(j + 1) * LANES] = jnp.where(low, oa / la, ob / lb).astype(BF16)


def _dsa_attention(iq, iw, ikl, ikh, sq, sk, sv, tq, tk, rs, topk):
    B, S, W = sq.shape
    V = sv.shape[2]
    nq = S // tq
    assert (N_CAND * LANES) % tk == 0 and tq % CAND_ROWS == 0 and tq % rs == 0 and tk > topk
    qblk = lambda b, i: (b, i, 0)
    full = lambda b, i: (b, 0, 0)
    once = pl.Buffered(1)
    return pl.pallas_call(
        functools.partial(_dsa_kernel, tq=tq, tk=tk, rs=rs, topk=topk),
        grid=(B, nq),
        in_specs=[pl.BlockSpec((1, tq, iq.shape[2]), qblk),
                  pl.BlockSpec((1, tq, LANES), qblk),
                  pl.BlockSpec((1, S, LANES), full, pipeline_mode=once),
                  pl.BlockSpec((1, S, LANES), full, pipeline_mode=once),
                  pl.BlockSpec((1, tq, W), qblk),
                  pl.BlockSpec((1, S, W), full, pipeline_mode=once),
                  pl.BlockSpec((1, S, V), full, pipeline_mode=once)],
        out_specs=pl.BlockSpec((1, tq, V), qblk),
        out_shape=jax.ShapeDtypeStruct((B, S, V), BF16),
        scratch_shapes=[pltpu.VMEM((tq, S), I32),
                        pltpu.VMEM((tq, N_CAND * LANES), I32),
                        pltpu.VMEM((tq, LANES), I32),
                        pltpu.VMEM((tq, LANES), F32),
                        pltpu.VMEM((tq, LANES), F32),
                        pltpu.VMEM((tq, tk), F32),
                        pltpu.VMEM((DSA_HEADS, tq, tk), F32),
                        pltpu.VMEM((DSA_HEADS, tq, LANES), F32),
                        pltpu.VMEM((DSA_HEADS, tq, LANES), F32)],
        compiler_params=_params(("parallel", "arbitrary")),
        name="dsa_attention",
    )(iq, iw, ikl, ikh, sq, sk, sv)


def _mix_kernel(do_ref, so_ref, x_ref, mod_ref, g2_ref, wo1_ref, wo2_ref, wsg_ref, wsu_ref, wsd_ref,
                rwt_ref, base_ref, h2_ref, lg_ref):
    mix = _dot(do_ref[0], wo1_ref[...]) + _dot(so_ref[0], wo2_ref[...])
    mod = mod_ref[0]
    x1 = x_ref[0] + mod[2:3] * mix
    ms = jnp.mean(x1 * x1, axis=-1, keepdims=True)
    h2 = x1 * lax.rsqrt(ms + EPS) * g2_ref[...] * (1.0 + mod[4:5]) + mod[3:4]
    hb = h2.astype(BF16)
    gate = _dot(hb, wsg_ref[...])
    up = _dot(hb, wsu_ref[...])
    act = gate / (1.0 + jnp.exp(-gate)) * up
    shared = _dot(act.astype(BF16), wsd_ref[...])
    base_ref[0] = x1 + mod[5:6] * shared
    lg_ref[...] = _nt(rwt_ref[...], hb)
    hf = hb.astype(F32)
    tm = hf.shape[0]
    bits = lax.bitcast_convert_type(hf, I32)
    half = D_MODEL // 2
    packed = (bits[:, half:] & -65536) | lax.shift_right_logical(bits[:, :half], 16)
    for j in range(PACK_TILES):
        h2_ref[pl.ds(j, tm, stride=PACK_TILES), :] = packed[:, j * LANES:(j + 1) * LANES]


def _mix(diff_out, dsa_out, x, mod3, g2, wo1, wo2, wsg, wsu, wsd, rwt, tm):
    B, S, D = x.shape
    ns = S // tm
    T = B * S
    tok = lambda b, i: (b, i, 0)
    c2 = lambda b, i: (0, 0)
    return pl.pallas_call(
        _mix_kernel,
        grid=(B, ns),
        in_specs=[pl.BlockSpec((1, tm, SEG), tok), pl.BlockSpec((1, tm, SEG), tok),
                  pl.BlockSpec((1, tm, D), tok),
                  pl.BlockSpec((1, 6, D), lambda b, i: (b, 0, 0)),
                  pl.BlockSpec((1, D), c2),
                  pl.BlockSpec(wo1.shape, c2), pl.BlockSpec(wo2.shape, c2),
                  pl.BlockSpec(wsg.shape, c2), pl.BlockSpec(wsu.shape, c2), pl.BlockSpec(wsd.shape, c2),
                  pl.BlockSpec(rwt.shape, c2)],
        out_specs=[pl.BlockSpec((1, tm, D), tok),
                   pl.BlockSpec((tm * PACK_TILES, LANES), lambda b, i: (b * ns + i, 0)),
                   pl.BlockSpec((N_EXPERTS, tm), lambda b, i: (0, b * ns + i))],
        out_shape=[jax.ShapeDtypeStruct((B, S, D), F32),
                   jax.ShapeDtypeStruct((T * PACK_TILES, LANES), I32),
                   jax.ShapeDtypeStruct((N_EXPERTS, T), F32)],
        compiler_params=_params(("parallel", "parallel")),
        name="mix_shared_router",
    )(diff_out, dsa_out, x, mod3, g2, wo1, wo2, wsg, wsu, wsd, rwt)


def _first_max(v, idx, sentinel):
    m = jnp.max(v, axis=0, keepdims=True)
    i = jnp.min(jnp.where(v == m, idx, sentinel), axis=0, keepdims=True)
    return m, i


def _route_kernel(lg_ref, bias_ref, eidx_ref, gate_ref):
    lg = lg_ref[...]
    tt = lg.shape[1]
    scores = 1.0 / (1.0 + jnp.exp(-lg))
    biased = scores + bias_ref[...]
    gi = lax.broadcasted_iota(I32, (GROUP_SIZE, tt), 0).astype(F32)
    gscore = []
    for g in range(N_GROUPS):
        blk = biased[g * GROUP_SIZE:(g + 1) * GROUP_SIZE, :]
        m1, i1 = _first_max(blk, gi, float(GROUP_SIZE))
        m2 = jnp.max(jnp.where(gi == i1, -jnp.inf, blk), axis=0, keepdims=True)
        gscore.append(m1 + m2)
    gs = jnp.concatenate(gscore, axis=0)
    gidx = lax.broadcasted_iota(I32, (N_GROUPS, tt), 0).astype(F32)
    chosen = jnp.zeros((N_GROUPS, tt), F32)
    for _ in range(TOPK_GROUPS):
        _, ig = _first_max(gs, gidx, float(N_GROUPS))
        hit = gidx == ig
        chosen = jnp.where(hit, 1.0, chosen)
        gs = jnp.where(hit, -jnp.inf, gs)
    masked = jnp.concatenate(
        [jnp.where(chosen[g:g + 1, :] > 0.5, biased[g * GROUP_SIZE:(g + 1) * GROUP_SIZE, :], -jnp.inf)
         for g in range(N_GROUPS)], axis=0)
    ei = lax.broadcasted_iota(I32, (N_EXPERTS, tt), 0).astype(F32)
    ids, ws = [], []
    for _ in range(TOP_K):
        _, ie = _first_max(masked, ei, float(N_EXPERTS))
        hit = ei == ie
        ws.append(jnp.sum(jnp.where(hit, scores, 0.0), axis=0, keepdims=True))
        ids.append(ie)
        masked = jnp.where(hit, -jnp.inf, masked)
    w = jnp.concatenate(ws, axis=0)
    gate_ref[...] = w / jnp.sum(w, axis=0, keepdims=True) * ROUTED_SCALE
    eidx_ref[...] = jnp.concatenate(ids, axis=0).astype(I32)


def _route(logits_t, bias_col, tt):
    E, T = logits_t.shape
    return pl.pallas_call(
        _route_kernel,
        grid=(T // tt,),
        in_specs=[pl.BlockSpec((E, tt), lambda i: (0, i)),
                  pl.BlockSpec((E, 1), lambda i: (0, 0))],
        out_specs=[pl.BlockSpec((TOP_K, tt), lambda i: (0, i)),
                   pl.BlockSpec((TOP_K, tt), lambda i: (0, i))],
        out_shape=[jax.ShapeDtypeStruct((TOP_K, T), I32),
                   jax.ShapeDtypeStruct((TOP_K, T), F32)],
        compiler_params=_params(("parallel",)),
        name="route",
    )(logits_t, bias_col)


def _plan_kernel(eidx_ref, dest_ref, bexp_ref, nused_ref, cnt_col, cnt_row, slot_base, *, blk, nb_pad):
    ph = pl.program_id(0)
    i = pl.program_id(1)
    tt = eidx_ref.shape[1]
    eidx = eidx_ref[...]
    ei = lax.broadcasted_iota(I32, (N_EXPERTS, tt), 0)
    onehot = jnp.zeros((N_EXPERTS, tt), F32)
    for k in range(TOP_K):
        onehot = onehot + jnp.where(ei == eidx[k:k + 1, :], 1.0, 0.0)
    oh = onehot.astype(BF16)

    @pl.when((ph == 0) & (i == 0))
    def _():
        cnt_col[...] = jnp.zeros(cnt_col.shape, F32)
        cnt_row[...] = jnp.zeros(cnt_row.shape, F32)

    @pl.when(ph == 0)
    def _():
        cnt_col[...] += _dot(oh, jnp.ones((tt, LANES), BF16))
        cnt_row[...] += _nt(jnp.ones((8, tt), BF16), oh)

    @pl.when((ph == 1) & (i == 0))
    def _():
        inv = 1.0 / blk
        nb_col = jnp.floor((cnt_col[:, 0:1] + (blk - 1)) * inv)
        nb_row = jnp.floor((cnt_row[0:1, :] + (blk - 1)) * inv)
        r = lax.broadcasted_iota(I32, (N_EXPERTS, N_EXPERTS), 0)
        c = lax.broadcasted_iota(I32, (N_EXPERTS, N_EXPERTS), 1)
        bstart = jnp.sum(jnp.where(c < r, nb_row, 0.0), axis=-1, keepdims=True)
        bend = bstart + nb_col
        slot_base[...] = bstart * blk
        jb = lax.broadcasted_iota(I32, (N_EXPERTS, nb_pad), 1).astype(F32)
        be = jnp.sum(jnp.where(bend <= jb, 1.0, 0.0), axis=0, keepdims=True)
        bexp_ref[...] = jnp.minimum(be, N_EXPERTS - 1.0).astype(I32)
        nused_ref[...] = jnp.broadcast_to(jnp.sum(nb_row, axis=-1, keepdims=True), nused_ref.shape).astype(I32)

    @pl.when(ph == 1)
    def _():
        tri = jnp.where(lax.broadcasted_iota(I32, (tt, tt), 0) < lax.broadcasted_iota(I32, (tt, tt), 1),
                        1.0, 0.0).astype(BF16)
        slot = _dot(oh, tri) + slot_base[...]
        for k in range(TOP_K):
            dk = jnp.sum(jnp.where(ei == eidx[k:k + 1, :], slot, 0.0), axis=0, keepdims=True)
            dest_ref[k:k + 1, :] = dk.astype(I32)
        slot_base[...] += jnp.sum(onehot, axis=-1, keepdims=True)


def _plan(eidx, tt, blk, nb_pad):
    K, T = eidx.shape
    nt = T // tt
    return pl.pallas_call(
        functools.partial(_plan_kernel, blk=blk, nb_pad=nb_pad),
        grid=(2, nt),
        in_specs=[pl.BlockSpec((K, tt), lambda p, i: (0, i))],
        out_specs=[pl.BlockSpec((K, tt), lambda p, i: (0, i * p)),
                   pl.BlockSpec((1, nb_pad), lambda p, i: (0, 0)),
                   pl.BlockSpec((1, LANES), lambda p, i: (0, 0))],
        out_shape=[jax.ShapeDtypeStruct((K, T), I32),
                   jax.ShapeDtypeStruct((1, nb_pad), I32),
                   jax.ShapeDtypeStruct((1, LANES), I32)],
        scratch_shapes=[pltpu.VMEM((N_EXPERTS, LANES), F32),
                        pltpu.VMEM((8, N_EXPERTS), F32),
                        pltpu.VMEM((N_EXPERTS, 1), F32)],
        compiler_params=_params(("arbitrary", "arbitrary")),
        name="plan",
    )(eidx)


def _row(ref, r):
    return ref.at[pl.ds(pl.multiple_of(r * PACK_TILES, PACK_TILES), PACK_TILES), :]


def _dispatch_kernel(dest_ref, h_ref, xs_in_ref, xs_ref, sem):
    del xs_in_ref
    tt = h_ref.shape[0] // PACK_TILES

    def issue(t, c):
        for k in range(TOP_K):
            pltpu.make_async_copy(_row(h_ref, t), _row(xs_ref, dest_ref[k, t]), sem).start()
        return c

    lax.fori_loop(0, tt, issue, 0)

    def drain(t, c):
        for k in range(TOP_K):
            pltpu.make_async_copy(_row(h_ref, 0), _row(xs_ref, 0), sem).wait()
        return c

    lax.fori_loop(0, tt, drain, 0)


def _dispatch(dest, h2rows, xs_init, tt):
    T = h2rows.shape[0] // PACK_TILES
    return pl.pallas_call(
        _dispatch_kernel,
        grid=(T // tt,),
        in_specs=[pl.BlockSpec((TOP_K, tt), lambda i: (0, i), memory_space=pltpu.SMEM),
                  pl.BlockSpec((tt * PACK_TILES, LANES), lambda i: (i, 0)),
                  pl.BlockSpec(memory_space=pl.ANY)],
        out_specs=pl.BlockSpec(memory_space=pl.ANY),
        out_shape=jax.ShapeDtypeStruct(xs_init.shape, xs_init.dtype),
        scratch_shapes=[pltpu.SemaphoreType.DMA(())],
        input_output_aliases={2: 0},
        compiler_params=_params(("arbitrary",), has_side_effects=True),
        name="dispatch",
    )(dest, h2rows, xs_init)


def _experts_kernel(bexp_ref, nused_ref, xs_ref, wg_ref, wu_ref, wd_ref, y_ref, xb_ref, wgb_ref, wub_ref, wdb_ref):
    j = pl.program_id(0)

    @pl.when(j < nused_ref[0])
    def _():
        @pl.when((j == 0) | (bexp_ref[j] != bexp_ref[jnp.maximum(j - 1, 0)]))
        def _():
            wgb_ref[...] = wg_ref[0].astype(BF16)
            wub_ref[...] = wu_ref[0].astype(BF16)
            wdb_ref[...] = wd_ref[0].astype(BF16)

        blk = xb_ref.shape[0]
        half = D_MODEL // 2
        for c in range(PACK_TILES):
            w = xs_ref[pl.ds(c, blk, stride=PACK_TILES), :]
            cols = slice(c * LANES, (c + 1) * LANES)
            xb_ref[:, cols] = lax.bitcast_convert_type(w << 16, F32).astype(BF16)
            xb_ref[:, half + c * LANES:half + (c + 1) * LANES] = lax.bitcast_convert_type(w & -65536, F32).astype(BF16)
        xb = xb_ref[...]
        gate = _dot(xb, wgb_ref[...])
        up = _dot(xb, wub_ref[...])
        act = gate / (1.0 + jnp.exp(-gate)) * up
        y = _dot(act.astype(BF16), wdb_ref[...])
        bits = lax.bitcast_convert_type(y.astype(BF16).astype(F32), I32)
        packed = (bits[:, half:] & -65536) | lax.shift_right_logical(bits[:, :half], 16)
        for c in range(PACK_TILES):
            y_ref[pl.ds(c, blk, stride=PACK_TILES), :] = packed[:, c * LANES:(c + 1) * LANES]


def _experts(bexp, nused, xs, wg, wu, wd, blk, n_blocks):
    live = lambda j, be, nu: jnp.minimum(j, nu[0] - 1)
    row_spec = pl.BlockSpec((blk * PACK_TILES, LANES), lambda j, be, nu: (live(j, be, nu), 0))
    wspec = lambda w: pl.BlockSpec((1,) + w.shape[1:], lambda j, be, nu: (be[live(j, be, nu)], 0, 0))
    return pl.pallas_call(
        _experts_kernel,
        grid_spec=pltpu.PrefetchScalarGridSpec(
            num_scalar_prefetch=2,
            grid=(n_blocks,),
            in_specs=[row_spec, wspec(wg), wspec(wu), wspec(wd)],
            out_specs=row_spec,
            scratch_shapes=[pltpu.VMEM((blk, D_MODEL), BF16),
                            pltpu.VMEM(wg.shape[1:], BF16), pltpu.VMEM(wu.shape[1:], BF16),
                            pltpu.VMEM(wd.shape[1:], BF16)]),
        out_shape=jax.ShapeDtypeStruct(xs.shape, I32),
        compiler_params=_params(("arbitrary",)),
        name="experts",
    )(bexp, nused, xs, wg, wu, wd)


def _combine_kernel(dest_ref, gate_ref, base_ref, mod_ref, y_ref, o_ref, buf, sem):
    tt = base_ref.shape[1]

    def issue(t, c):
        for k in range(TOP_K):
            pltpu.make_async_copy(_row(y_ref, dest_ref[k, t]), _row(buf, k * tt + t), sem).start()
        return c

    lax.fori_loop(0, tt, issue, 0)

    def drain(t, c):
        for k in range(TOP_K):
            pltpu.make_async_copy(_row(y_ref, 0), _row(buf, 0), sem).wait()
        return c

    lax.fori_loop(0, tt, drain, 0)

    gates = gate_ref[...]
    g2 = mod_ref[0][5:6]
    half = D_MODEL // 2
    for j in range(PACK_TILES):
        lo = jnp.zeros((tt, LANES), F32)
        hi = jnp.zeros((tt, LANES), F32)
        for k in range(TOP_K):
            w = buf[pl.ds(k * tt * PACK_TILES + j, tt, stride=PACK_TILES), :]
            lo = lo + gates[:, k:k + 1] * lax.bitcast_convert_type(w << 16, F32)
            hi = hi + gates[:, k:k + 1] * lax.bitcast_convert_type(w & -65536, F32)
        for off, acc in ((0, lo), (half, hi)):
            cols = slice(off + j * LANES, off + (j + 1) * LANES)
            o_ref[0, :, cols] = base_ref[0, :, cols] + g2[:, cols] * acc


def _combine(dest, gates_tk, base, mod3, y, tt):
    B, S, D = base.shape
    ns = S // tt
    return pl.pallas_call(
        _combine_kernel,
        grid=(B, ns),
        in_specs=[pl.BlockSpec((TOP_K, tt), lambda b, i: (0, b * ns + i), memory_space=pltpu.SMEM),
                  pl.BlockSpec((tt, TOP_K), lambda b, i: (b * ns + i, 0)),
                  pl.BlockSpec((1, tt, D), lambda b, i: (b, i, 0)),
                  pl.BlockSpec((1, 6, D), lambda b, i: (b, 0, 0)),
                  pl.BlockSpec(memory_space=pl.ANY)],
        out_specs=pl.BlockSpec((1, tt, D), lambda b, i: (b, i, 0)),
        out_shape=jax.ShapeDtypeStruct((B, S, D), F32),
        scratch_shapes=[pltpu.VMEM((TOP_K * tt * PACK_TILES, LANES), I32),
                        pltpu.SemaphoreType.DMA(())],
        compiler_params=_params(("arbitrary", "arbitrary")),
        name="combine",
    )(dest, gates_tk, base, mod3, y)


def _alibi_q_features(n_heads, maps_per_head):
    slopes = 2.0 ** (-8.0 * jnp.arange(1, n_heads + 1, dtype=F32) / n_heads)
    c = jnp.repeat(slopes, maps_per_head) * LOG2E * POS_RADIX
    pieces = []
    rest = c
    for _ in range(N_SPLIT):
        p = rest.astype(BF16).astype(F32)
        pieces.append(p)
        rest = rest - p
    hi = jnp.stack(pieces, axis=1)
    feat = jnp.concatenate([hi, hi / POS_RADIX], axis=1)
    return jnp.pad(feat, ((0, 0), (HEAD, LANES - HEAD - 2 * N_SPLIT)))


def kernel(x, c, ada_w, ada_b, norm1_g, norm2_g, w_in, diff_q_norm_g, diff_k_norm_g, lam_q1, lam_k1, lam_q2, lam_k2, diff_subln_g, dsa_q_norm_g, dsa_k_norm_g, idx_k_norm_g, w_out, router_w, router_bias, exp_w_gate, exp_w_up, exp_w_down, shared_w_gate, shared_w_up, shared_w_down):
    B, S, D = x.shape
    assert D == D_MODEL and ada_w.shape[0] == 1 and S <= POS_RADIX * 128
    T = B * S
    topk = min(DSA_TOPK, S // 4)
    tm = min(512, S)
    tq_diff = min(512, S)
    tq_dsa = min(256, S)
    tk_dsa = min(512, S)
    rs_dsa = min(128, tq_dsa)
    tt_route = min(512, T)
    tt_move = min(512, S)
    blk = 1024
    n_blocks = (T * TOP_K) // blk + N_EXPERTS
    nb_pad = -(-n_blocks // LANES) * LANES

    n_main = 7 * SEG
    wm = w_in[0, :, :n_main].astype(BF16)
    wt = jnp.pad(w_in[0, :, n_main:], ((0, 0), (0, LANES - (IDX_DIM + IDX_HEADS)))).astype(BF16)
    tile8 = lambda g: jnp.tile(g[0], SEG // g.shape[1]).reshape(1, SEG)
    gik = jnp.pad(idx_k_norm_g[0], (0, LANES - IDX_DIM)).reshape(1, LANES)
    lamv = jnp.concatenate([lam_q1, lam_k1, lam_q2, lam_k2], axis=0)
    wo1 = w_out[0, :SEG].astype(BF16)
    wo2 = w_out[0, SEG:].astype(BF16)
    rwt = router_w[0].T.astype(BF16)
    wg, wu, wd = exp_w_gate[0], exp_w_up[0], exp_w_down[0]

    mod3 = _ada(c, ada_w[0], ada_b[0]).reshape(B, 6, D)

    dq, dk, dv, sq, sk, sv, iq, ikl, ikh, iw = _inproj(
        x, mod3, norm1_g, wm, wt, tile8(diff_q_norm_g), tile8(diff_k_norm_g),
        tile8(dsa_q_norm_g), tile8(dsa_k_norm_g), gik,
        _alibi_q_features(DIFF_HEADS, 2), _alibi_q_features(DSA_HEADS, 1), tm)

    diff_out = _diff_attention(dq, dk, dv, lamv, diff_subln_g, tq_diff)
    dsa_out = _dsa_attention(iq, iw, ikl, ikh, sq, sk, sv, tq_dsa, tk_dsa, rs_dsa, topk)

    base, h2rows, logits_t = _mix(diff_out, dsa_out, x, mod3, norm2_g, wo1, wo2,
                                  shared_w_gate[0].astype(BF16), shared_w_up[0].astype(BF16),
                                  shared_w_down[0].astype(BF16), rwt, tm)

    eidx, gates = _route(logits_t, router_bias[0].reshape(N_EXPERTS, 1), tt_route)
    dest, bexp, nused = _plan(eidx, tt_route, blk, nb_pad)

    xs = _dispatch(dest, h2rows, jnp.zeros((n_blocks * blk * PACK_TILES, LANES), I32), tt_move)
    y = _experts(bexp.reshape(nb_pad), nused[0, :1], xs, wg, wu, wd, blk, n_blocks)
    return _combine(dest, gates.T, base, mod3, y, tt_move)
```

```python
import functools
import math

import jax
import jax.numpy as jnp
from jax import lax
from jax.experimental import pallas as pl
from jax.experimental.pallas import tpu as pltpu

F32 = jnp.float32
BF16 = jnp.bfloat16
I32 = jnp.int32

D_MODEL = 1024
DIFF_HEADS = 4
DSA_HEADS = 8
IDX_HEADS = 8
IDX_DIM = 64
DSA_TOPK = 256
N_EXPERTS = 256
TOP_K = 8
N_GROUPS = 8
GROUP_SIZE = N_EXPERTS // N_GROUPS
TOPK_GROUPS = 4
ROUTED_SCALE = 2.5
EPS = 1e-6
LAM_INIT = 0.2

LANES = 128
PACK_TILES = D_MODEL // (2 * LANES)
SEG = 512
HEAD = 64
N_MAPS = SEG // HEAD
WIDE = N_MAPS * LANES
POS_RADIX = 64
N_SPLIT = 3
NEG_BIG = -1e30
INT_MIN = -2147483648
LOG2E = math.log2(math.e)
VMEM_LIMIT = 56 * 1024 * 1024

NT_DIMS = (((1,), (1,)), ((), ()))


def _nt(a, b):
    return lax.dot_general(a, b, NT_DIMS, preferred_element_type=F32)


def _dot(a, b):
    return jnp.dot(a, b, preferred_element_type=F32)


def _rep(x, reps):
    return jnp.concatenate([x] * reps, axis=1)


def _params(sem, vmem=VMEM_LIMIT, **kw):
    return pltpu.CompilerParams(dimension_semantics=sem, vmem_limit_bytes=vmem, **kw)


def _ada_kernel(c_ref, w_ref, b_ref, o_ref):
    c = c_ref[...]
    s = c / (1.0 + jnp.exp(-c))
    o_ref[...] = jnp.dot(s, w_ref[...], preferred_element_type=F32,
                         precision=lax.Precision.HIGHEST) + b_ref[...]


def _ada(c, w, b):
    B, D = c.shape
    N = w.shape[1]
    tn = D
    return pl.pallas_call(
        _ada_kernel,
        grid=(N // tn,),
        in_specs=[pl.BlockSpec((B, D), lambda j: (0, 0)),
                  pl.BlockSpec((D, tn), lambda j: (0, j)),
                  pl.BlockSpec((1, tn), lambda j: (0, j))],
        out_specs=pl.BlockSpec((B, tn), lambda j: (0, j)),
        out_shape=jax.ShapeDtypeStruct((B, N), F32),
        compiler_params=_params(("arbitrary",)),
        name="ada",
    )(c, w, b.reshape(1, N))


def _group_sumsq(z):
    n = z.shape[1]
    r = lax.broadcasted_iota(I32, (n, n), 0) // HEAD
    c = lax.broadcasted_iota(I32, (n, n), 1) // HEAD
    bd = jnp.where(r == c, 1.0, 0.0).astype(BF16)
    zz = z * z
    hi = zz.astype(BF16)
    lo = (zz - hi.astype(F32)).astype(BF16)
    return _dot(hi, bd) + _dot(lo, bd)


def _inproj_kernel(x_ref, mod_ref, g1_ref, wm_ref, wt_ref, gq_ref, gk_ref, gsq_ref, gsk_ref, gik_ref,
                   fdq_ref, fsq_ref,
                   dq_ref, dk_ref, dv_ref, sq_ref, sk_ref, sv_ref, iq_ref, ikl_ref, ikh_ref, iw_ref):
    x = x_ref[0]
    tm = x.shape[0]
    ms = jnp.mean(x * x, axis=-1, keepdims=True)
    y = x * lax.rsqrt(ms + EPS) * g1_ref[...]
    mod = mod_ref[0]
    h = y * (1.0 + mod[1:2]) + mod[0:1]
    hb = h.astype(BF16)

    lane = lax.broadcasted_iota(I32, (tm, LANES), 1)
    is_head = lane < HEAD
    kpos = pl.program_id(1) * tm + lax.broadcasted_iota(I32, (tm, LANES), 0)
    hi_digit = (kpos // POS_RADIX).astype(F32)
    lo_digit = (kpos % POS_RADIX).astype(F32)
    kfeat = jnp.where(lane < HEAD + N_SPLIT, hi_digit, jnp.where(lane < HEAD + 2 * N_SPLIT, lo_digit, 0.0))

    def plain(seg_idx, out_ref):
        out_ref[0] = _dot(hb, wm_ref[:, seg_idx * SEG:(seg_idx + 1) * SEG]).astype(BF16)

    def normed(seg_idx, g_ref, scale, feat_ref, out_ref):
        half = SEG // 2
        for i in range(2):
            lo = seg_idx * SEG + i * half
            z = _dot(hb, wm_ref[:, lo:lo + half])
            ss = _group_sumsq(z)
            zn = z * lax.rsqrt(ss * (1.0 / HEAD) + EPS) * (g_ref[:, i * half:(i + 1) * half] * scale)
            for g in range(half // LANES):
                zg = zn[:, g * LANES:(g + 1) * LANES]
                for odd in range(2):
                    idx = i * (half // HEAD) + 2 * g + odd
                    src = zg if odd == 0 else pltpu.roll(zg, HEAD, 1)
                    feat = kfeat if feat_ref is None else feat_ref[idx:idx + 1, :]
                    out_ref[0, :, idx * LANES:(idx + 1) * LANES] = jnp.where(is_head, src, feat).astype(BF16)

    normed(0, gq_ref, HEAD ** -0.5 * LOG2E, fdq_ref, dq_ref)
    normed(1, gk_ref, 1.0, None, dk_ref)
    plain(2, dv_ref)
    normed(3, gsq_ref, HEAD ** -0.5 * LOG2E, fsq_ref, sq_ref)
    normed(4, gsk_ref, 1.0, None, sk_ref)
    plain(5, sv_ref)
    plain(6, iq_ref)

    t = _dot(hb, wt_ref[...])
    ikraw = jnp.where(lane < IDX_DIM, t, 0.0)
    ss = jnp.sum(ikraw * ikraw, axis=-1, keepdims=True) * (1.0 / IDX_DIM)
    ikn = ikraw * lax.rsqrt(ss + EPS) * gik_ref[...]
    ikl_ref[0] = ikn.astype(BF16)
    ikh_ref[0] = pltpu.roll(ikn, IDX_DIM, 1).astype(BF16)
    iwraw = jnp.where((lane >= IDX_DIM) & (lane < IDX_DIM + IDX_HEADS), t, 0.0)
    iw_ref[0] = pltpu.roll(iwraw * (IDX_HEADS ** -0.5), LANES - IDX_DIM, 1) * (IDX_DIM ** -0.5)


def _inproj(x, mod3, g1, wm, wt, gq, gk, gsq, gsk, gik, fdq, fsq, tm):
    B, S, D = x.shape
    ns = S // tm
    tok = lambda b, i: (b, i, 0)
    const2 = lambda b, i: (0, 0)
    seg_spec = pl.BlockSpec((1, tm, SEG), tok)
    wide_spec = pl.BlockSpec((1, tm, WIDE), tok)
    lane_spec = pl.BlockSpec((1, tm, LANES), tok)
    seg_shape = jax.ShapeDtypeStruct((B, S, SEG), BF16)
    wide_shape = jax.ShapeDtypeStruct((B, S, WIDE), BF16)
    return pl.pallas_call(
        _inproj_kernel,
        grid=(B, ns),
        in_specs=[pl.BlockSpec((1, tm, D), tok),
                  pl.BlockSpec((1, 6, D), lambda b, i: (b, 0, 0)),
                  pl.BlockSpec((1, D), const2),
                  pl.BlockSpec(wm.shape, const2),
                  pl.BlockSpec(wt.shape, const2),
                  pl.BlockSpec((1, SEG), const2), pl.BlockSpec((1, SEG), const2),
                  pl.BlockSpec((1, SEG), const2), pl.BlockSpec((1, SEG), const2),
                  pl.BlockSpec((1, LANES), const2),
                  pl.BlockSpec((N_MAPS, LANES), const2), pl.BlockSpec((N_MAPS, LANES), const2)],
        out_specs=[wide_spec, wide_spec, seg_spec, wide_spec, wide_spec, seg_spec, seg_spec,
                   lane_spec, lane_spec, lane_spec],
        out_shape=[wide_shape, wide_shape, seg_shape, wide_shape, wide_shape, seg_shape, seg_shape,
                   jax.ShapeDtypeStruct((B, S, LANES), BF16), jax.ShapeDtypeStruct((B, S, LANES), BF16),
                   jax.ShapeDtypeStruct((B, S, LANES), F32)],
        compiler_params=_params(("parallel", "parallel")),
        name="inproj",
    )(x, mod3, g1, wm, wt, gq, gk, gsq, gsk, gik, fdq, fsq)


def _diff_kernel(q_ref, k_ref, v_ref, lam_ref, g_ref, o_ref, s_ref, m_ref, l_ref, acc_ref, *, tq):
    qi = pl.program_id(1)
    ki = pl.program_id(2)
    n_maps = 2 * DIFF_HEADS
    reps = tq // LANES

    @pl.when(ki == 0)
    def _():
        m_ref[...] = jnp.full(m_ref.shape, NEG_BIG, F32)
        l_ref[...] = jnp.zeros(l_ref.shape, F32)
        acc_ref[...] = jnp.zeros(acc_ref.shape, F32)

    def step(diag):
        if diag:
            row = lax.broadcasted_iota(I32, (tq, tq), 0)
            col = lax.broadcasted_iota(I32, (tq, tq), 1)
            causal_bias = jnp.where(col <= row, 0.0, NEG_BIG)
        m_olds, m_news = [], []
        for idx in range(n_maps):
            s = _nt(q_ref[0, :, idx * LANES:(idx + 1) * LANES], k_ref[0, :, idx * LANES:(idx + 1) * LANES])
            if diag:
                s = s + causal_bias
            s_ref[idx] = s
            m_old = m_ref[idx]
            m_olds.append(m_old)
            m_news.append(jnp.maximum(m_old, jnp.max(s, axis=-1, keepdims=True)))
        for idx in range(n_maps):
            h = idx // 2
            p = jnp.exp2(s_ref[idx] - _rep(m_news[idx], reps))
            alpha = jnp.exp2(m_olds[idx] - m_news[idx])
            psum = p[:, 0:LANES]
            for g in range(1, reps):
                psum = psum + p[:, g * LANES:(g + 1) * LANES]
            l_ref[idx] = alpha * l_ref[idx] + psum
            acc_ref[idx] = alpha * acc_ref[idx] + _dot(p.astype(BF16), v_ref[0, :, h * LANES:(h + 1) * LANES])
            m_ref[idx] = m_news[idx]

    @pl.when(ki < qi)
    def _():
        step(False)

    @pl.when(ki == qi)
    def _():
        step(True)
        lv = lam_ref[...]
        lam = (jnp.exp(jnp.sum(lv[0:1] * lv[1:2], axis=-1, keepdims=True))
               - jnp.exp(jnp.sum(lv[2:3] * lv[3:4], axis=-1, keepdims=True)) + LAM_INIT)
        for h in range(DIFF_HEADS):
            o1 = acc_ref[2 * h] / jnp.sum(l_ref[2 * h], axis=-1, keepdims=True)
            o2 = acc_ref[2 * h + 1] / jnp.sum(l_ref[2 * h + 1], axis=-1, keepdims=True)
            o = o1 - lam * o2
            ms = jnp.mean(o * o, axis=-1, keepdims=True)
            on = o * lax.rsqrt(ms + EPS) * g_ref[...]
            o_ref[0, :, h * LANES:(h + 1) * LANES] = (on * (1.0 - LAM_INIT)).astype(BF16)


def _diff_attention(dq, dk, dv, lamv, subln_g, tq):
    B, S, W = dq.shape
    V = dv.shape[2]
    nq = S // tq
    n_maps = 2 * DIFF_HEADS
    return pl.pallas_call(
        functools.partial(_diff_kernel, tq=tq),
        grid=(B, nq, nq),
        in_specs=[pl.BlockSpec((1, tq, W), lambda b, i, j: (b, i, 0)),
                  pl.BlockSpec((1, tq, W), lambda b, i, j: (b, jnp.minimum(i, j), 0)),
                  pl.BlockSpec((1, tq, V), lambda b, i, j: (b, jnp.minimum(i, j), 0)),
                  pl.BlockSpec(lamv.shape, lambda b, i, j: (0, 0)),
                  pl.BlockSpec((1, LANES), lambda b, i, j: (0, 0))],
        out_specs=pl.BlockSpec((1, tq, V), lambda b, i, j: (b, i, 0)),
        out_shape=jax.ShapeDtypeStruct((B, S, V), BF16),
        scratch_shapes=[pltpu.VMEM((n_maps, tq, tq), F32),
                        pltpu.VMEM((n_maps, tq, LANES), F32),
                        pltpu.VMEM((n_maps, tq, LANES), F32),
                        pltpu.VMEM((n_maps, tq, LANES), F32)],
        compiler_params=_params(("parallel", "parallel", "arbitrary")),
        name="diff_attention",
    )(dq, dk, dv, lamv, subln_g)


def _score_key(v):
    bits = lax.bitcast_convert_type(v, I32)
    return bits ^ ((bits >> 31) & 0x7FFFFFFF)


def _key_score(k):
    return lax.bitcast_convert_type(k ^ ((k >> 31) & 0x7FFFFFFF), F32)


_SUM_LANE = (LANES - 1, 0)
N_CAND = 12
CAND_ROWS = 32


def _dsa_kernel(iq_ref, iw_ref, ikl_ref, ikh_ref, q_ref, k_ref, v_ref, o_ref,
                key_ref, cand_ref, thr_ref, nties_ref, nge_ref, mb_ref, s_ref, m_ref, acc_ref, *, tq, tk, rs, topk):
    qi = pl.program_id(1)
    q0 = qi * tq
    nkc = (q0 + tq + tk - 1) // tk
    def causal(k0):
        row = q0 + lax.broadcasted_iota(I32, (tq, tk), 0)
        return lax.broadcasted_iota(I32, (tq, tk), 1) + k0 <= row

    low =lax.broadcasted_iota(I32, (tq, LANES), 1) < HEAD
    reps = tk // LANES
    iw = iw_ref[0]

    def score_chunk(kc, carry):
        k0 = pl.multiple_of(kc * tk, tk)
        ikl = ikl_ref[0, pl.ds(k0, tk), :]
        ikh = ikh_ref[0, pl.ds(k0, tk), :]
        sc = jnp.zeros((tq, tk), F32)
        for j in range(IDX_HEADS // 2):
            iqp = iq_ref[0, :, j * LANES:(j + 1) * LANES]
            sc = sc + iw[:, 2 * j:2 * j + 1] * jnp.maximum(_nt(iqp, ikl), 0.0)
            sc = sc + iw[:, 2 * j + 1:2 * j + 2] * jnp.maximum(_nt(iqp, ikh), 0.0)
        sc = jnp.where(sc == 0.0, 0.0, sc)
        sc = jnp.where(causal(k0), sc, -jnp.inf)
        key_ref[:, pl.ds(k0, tk)] = _score_key(sc)
        return carry

    lax.fori_loop(0, nkc, score_chunk, 0)

    groups = [slice(r0, r0 + rs) for r0 in range(0, tq, rs)]
    kf = float(topk)

    def candidates(g, carry):
        rows = pl.ds(pl.multiple_of(g * CAND_ROWS, CAND_ROWS), CAND_ROWS)

        def insert_chunk(kc, best):
            k0 = pl.multiple_of(kc * tk, tk)
            x = _key_score(key_ref[rows, pl.ds(k0, tk)])
            best = list(best)
            for lg in range(reps):
                v = x[:, lg * LANES:(lg + 1) * LANES]
                for i in range(N_CAND):
                    best[i], v = jnp.maximum(best[i], v), jnp.minimum(best[i], v)
            return tuple(best)

        lowest = jnp.full((CAND_ROWS, LANES), -jnp.inf, F32)
        best = lax.fori_loop(0, nkc, insert_chunk, (lowest,) * N_CAND)
        for i in range(N_CAND):
            cand_ref[rows, i * LANES:(i + 1) * LANES] = _score_key(best[i])
        return carry

    lax.fori_loop(0, tq // CAND_ROWS, candidates, 0)

    def count(ref, n_chunks, thrs, strict):
        accs = []
        for rows, thr in zip(groups, thrs):
            thr_t = _rep(thr, reps)

            def body(kc, acc, rows=rows, thr_t=thr_t):
                k0 = kc * tk if isinstance(kc, int) else pl.multiple_of(kc * tk, tk)
                keyc = ref[rows, pl.ds(k0, tk)]
                hit = jnp.where((keyc > thr_t) if strict else (keyc >= thr_t), 1.0, 0.0)
                for g in range(reps):
                    acc = acc + hit[:, g * LANES:(g + 1) * LANES]
                return acc

            acc = jnp.zeros((rs, LANES), F32)
            if isinstance(n_chunks, int):
                for kc in range(n_chunks):
                    acc = body(kc, acc)
            else:
                acc = lax.fori_loop(0, n_chunks, body, acc)
            accs.append(acc)
        return [jnp.broadcast_to(jnp.sum(acc, axis=-1, keepdims=True), (rs, LANES)) for acc in accs]

    def search(ref, n_chunks):
        def bit_step(i, tus):
            bit = jnp.left_shift(jnp.int32(1), 31 - i)
            cands = [tu | bit for tu in tus]
            cnts = count(ref, n_chunks, [c ^ INT_MIN for c in cands], False)
            return tuple(jnp.where(cnt >= kf, c, tu) for cnt, c, tu in zip(cnts, cands, tus))

        tus = lax.fori_loop(0, 32, bit_step, tuple(jnp.zeros((rs, LANES), I32) for _ in groups))
        return [tu ^ INT_MIN for tu in tus]

    def publish(thrs, n_above, n_reach):
        for rows, thr, n_gt, n_ge in zip(groups, thrs, n_above, n_reach):
            thr_ref[rows, :] = thr
            nties_ref[rows, :] = kf - n_gt
            nge_ref[rows, :] = n_ge

    cand_chunks = N_CAND * LANES // tk
    thrs = search(cand_ref, cand_chunks)
    inside = count(cand_ref, cand_chunks, thrs, False)
    publish(thrs, count(cand_ref, cand_chunks, thrs, True), inside)
    overall = count(key_ref, nkc, thrs, False)
    missed = [jnp.max(jnp.where(a != b, 1.0, 0.0)) for a, b in zip(inside, overall)]

    @pl.when(functools.reduce(jnp.maximum, missed) > 0.0)
    def _():
        full = search(key_ref, nkc)
        publish(full, count(key_ref, nkc, full, True), count(key_ref, nkc, full, False))

    m_ref[...] = jnp.full(m_ref.shape, NEG_BIG, F32)
    acc_ref[...] = jnp.zeros(acc_ref.shape, F32)
    tie_rows = jnp.max(jnp.where(nge_ref[...] != kf, 1.0, 0.0))

    def attend_chunk(kc, ties_before, ranked):
        k0 = pl.multiple_of(kc * tk, tk)
        keyc = key_ref[:, pl.ds(k0, tk)]
        thr_t = _rep(thr_ref[...], reps)
        if ranked:
            tri = jnp.where(lax.broadcasted_iota(I32, (tk, tk), 0) < lax.broadcasted_iota(I32, (tk, tk), 1),
                            1.0, 0.0).astype(BF16)
            eq = jnp.where(keyc == thr_t, 1.0, 0.0)
            rank = _dot(eq.astype(BF16), tri) + _rep(ties_before, reps)
            take = jnp.where(keyc > thr_t, 1.0, jnp.where(rank < _rep(nties_ref[...], reps), eq, 0.0))
            psum = eq[:, 0:LANES]
            for g in range(1, reps):
                psum = psum + eq[:, g * LANES:(g + 1) * LANES]
            ties_before = ties_before + jnp.sum(psum, axis=-1, keepdims=True)
            mb_ref[...] = jnp.where(causal(k0), jnp.where(take > 0.5, 0.0, NEG_BIG), NEG_BIG)
        else:
            mb_ref[...] = jnp.where(keyc >= thr_t, 0.0, NEG_BIG)
        mb = mb_ref[...]
        m_olds, m_news = [], []
        for hd in range(DSA_HEADS):
            s = _nt(q_ref[0, :, hd * LANES:(hd + 1) * LANES],
                    k_ref[0, pl.ds(k0, tk), hd * LANES:(hd + 1) * LANES]) + mb
            s_ref[hd] = s
            m_old = m_ref[hd]
            m_olds.append(m_old)
            m_news.append(jnp.maximum(m_old, jnp.max(s, axis=-1, keepdims=True)))
        for j in range(DSA_HEADS // 2):
            vp = v_ref[0, pl.ds(k0, tk), j * LANES:(j + 1) * LANES]
            vlane = lax.broadcasted_iota(I32, vp.shape, 1)
            for a in range(2):
                hd = 2 * j + a
                keep = (vlane < HEAD) if a == 0 else (vlane >= HEAD)
                va = jnp.where(vlane == _SUM_LANE[a], jnp.ones_like(vp), jnp.where(keep, vp, jnp.zeros_like(vp)))
                p = jnp.exp2((s_ref[hd] - _rep(m_news[hd], reps)).astype(BF16))
                acc_ref[hd] = jnp.exp2(m_olds[hd] - m_news[hd]) * acc_ref[hd] + _dot(p, va)
                m_ref[hd] = m_news[hd]
        return ties_before

    for ranked in (False, True):
        @pl.when((tie_rows > 0.0) == ranked)
        def _(ranked=ranked):
            lax.fori_loop(0, nkc, functools.partial(attend_chunk, ranked=ranked), jnp.zeros((tq, LANES), F32))

    lane = lax.broadcasted_iota(I32, (tq, LANES), 1)
    for j in range(DSA_HEADS // 2):
        oa, ob = acc_ref[2 * j], acc_ref[2 * j + 1]
        la = jnp.sum(jnp.where(lane == _SUM_LANE[0], oa, 0.0), axis=-1, keepdims=True)
        lb = jnp.sum(jnp.where(lane == _SUM_LANE[1], ob, 0.0), axis=-1, keepdims=True)
        o_ref[0, :, j * LANES:(j + 1) * LANES] = jnp.where(low, oa / la, ob / lb).astype(BF16)


def _dsa_attention(iq, iw, ikl, ikh, sq, sk, sv, tq, tk, rs, topk):
    B, S, W = sq.shape
    V = sv.shape[2]
    nq = S // tq
    assert (N_CAND * LANES) % tk == 0 and tq % CAND_ROWS == 0 and tq % rs == 0 and tk > topk
    qblk = lambda b, i: (b, i, 0)
    full = lambda b, i: (b, 0, 0)
    once = pl.Buffered(1)
    return pl.pallas_call(
        functools.partial(_dsa_kernel, tq=tq, tk=tk, rs=rs, topk=topk),
        grid=(B, nq),
        in_specs=[pl.BlockSpec((1, tq, iq.shape[2]), qblk),
                  pl.BlockSpec((1, tq, LANES), qblk),
                  pl.BlockSpec((1, S, LANES), full, pipeline_mode=once),
                  pl.BlockSpec((1, S, LANES), full, pipeline_mode=once),
                  pl.BlockSpec((1, tq, W), qblk),
                  pl.BlockSpec((1, S, W), full, pipeline_mode=once),
                  pl.BlockSpec((1, S, V), full, pipeline_mode=once)],
        out_specs=pl.BlockSpec((1, tq, V), qblk),
        out_shape=jax.ShapeDtypeStruct((B, S, V), BF16),
        scratch_shapes=[pltpu.VMEM((tq, S), I32),
                        pltpu.VMEM((tq, N_CAND * LANES), I32),
                        pltpu.VMEM((tq, LANES), I32),
                        pltpu.VMEM((tq, LANES), F32),
                        pltpu.VMEM((tq, LANES), F32),
                        pltpu.VMEM((tq, tk), F32),
                        pltpu.VMEM((DSA_HEADS, tq, tk), F32),
                        pltpu.VMEM((DSA_HEADS, tq, LANES), F32),
                        pltpu.VMEM((DSA_HEADS, tq, LANES), F32)],
        compiler_params=_params(("parallel", "arbitrary")),
        name="dsa_attention",
    )(iq, iw, ikl, ikh, sq, sk, sv)


def _mix_kernel(do_ref, so_ref, x_ref, mod_ref, g2_ref, wo1_ref, wo2_ref, wsg_ref, wsu_ref, wsd_ref,
                rwt_ref, base_ref, h2_ref, lg_ref):
    mix = _dot(do_ref[0], wo1_ref[...]) + _dot(so_ref[0], wo2_ref[...])
    mod = mod_ref[0]
    x1 = x_ref[0] + mod[2:3] * mix
    ms = jnp.mean(x1 * x1, axis=-1, keepdims=True)
    h2 = x1 * lax.rsqrt(ms + EPS) * g2_ref[...] * (1.0 + mod[4:5]) + mod[3:4]
    hb = h2.astype(BF16)
    gate = _dot(hb, wsg_ref[...])
    up = _dot(hb, wsu_ref[...])
    act = gate / (1.0 + jnp.exp(-gate)) * up
    shared = _dot(act.astype(BF16), wsd_ref[...])
    base_ref[0] = x1 + mod[5:6] * shared
    lg_ref[...] = _nt(rwt_ref[...], hb)
    hf = hb.astype(F32)
    tm = hf.shape[0]
    bits = lax.bitcast_convert_type(hf, I32)
    half = D_MODEL // 2
    packed = (bits[:, half:] & -65536) | lax.shift_right_logical(bits[:, :half], 16)
    for j in range(PACK_TILES):
        h2_ref[pl.ds(j, tm, stride=PACK_TILES), :] = packed[:, j * LANES:(j + 1) * LANES]


def _mix(diff_out, dsa_out, x, mod3, g2, wo1, wo2, wsg, wsu, wsd, rwt, tm):
    B, S, D = x.shape
    ns = S // tm
    T = B * S
    tok = lambda b, i: (b, i, 0)
    c2 = lambda b, i: (0, 0)
    return pl.pallas_call(
        _mix_kernel,
        grid=(B, ns),
        in_specs=[pl.BlockSpec((1, tm, SEG), tok), pl.BlockSpec((1, tm, SEG), tok),
                  pl.BlockSpec((1, tm, D), tok),
                  pl.BlockSpec((1, 6, D), lambda b, i: (b, 0, 0)),
                  pl.BlockSpec((1, D), c2),
                  pl.BlockSpec(wo1.shape, c2), pl.BlockSpec(wo2.shape, c2),
                  pl.BlockSpec(wsg.shape, c2), pl.BlockSpec(wsu.shape, c2), pl.BlockSpec(wsd.shape, c2),
                  pl.BlockSpec(rwt.shape, c2)],
        out_specs=[pl.BlockSpec((1, tm, D), tok),
                   pl.BlockSpec((tm * PACK_TILES, LANES), lambda b, i: (b * ns + i, 0)),
                   pl.BlockSpec((N_EXPERTS, tm), lambda b, i: (0, b * ns + i))],
        out_shape=[jax.ShapeDtypeStruct((B, S, D), F32),
                   jax.ShapeDtypeStruct((T * PACK_TILES, LANES), I32),
                   jax.ShapeDtypeStruct((N_EXPERTS, T), F32)],
        compiler_params=_params(("parallel", "parallel")),
        name="mix_shared_router",
    )(diff_out, dsa_out, x, mod3, g2, wo1, wo2, wsg, wsu, wsd, rwt)


def _first_max(v, idx, sentinel):
    m = jnp.max(v, axis=0, keepdims=True)
    i = jnp.min(jnp.where(v == m, idx, sentinel), axis=0, keepdims=True)
    return m, i


def _route_kernel(lg_ref, bias_ref, eidx_ref, gate_ref):
    lg = lg_ref[...]
    tt = lg.shape[1]
    scores = 1.0 / (1.0 + jnp.exp(-lg))
    biased = scores + bias_ref[...]
    gi = lax.broadcasted_iota(I32, (GROUP_SIZE, tt), 0).astype(F32)
    gscore = []
    for g in range(N_GROUPS):
        blk = biased[g * GROUP_SIZE:(g + 1) * GROUP_SIZE, :]
        m1, i1 = _first_max(blk, gi, float(GROUP_SIZE))
        m2 = jnp.max(jnp.where(gi == i1, -jnp.inf, blk), axis=0, keepdims=True)
        gscore.append(m1 + m2)
    gs = jnp.concatenate(gscore, axis=0)
    gidx = lax.broadcasted_iota(I32, (N_GROUPS, tt), 0).astype(F32)
    chosen = jnp.zeros((N_GROUPS, tt), F32)
    for _ in range(TOPK_GROUPS):
        _, ig = _first_max(gs, gidx, float(N_GROUPS))
        hit = gidx == ig
        chosen = jnp.where(hit, 1.0, chosen)
        gs = jnp.where(hit, -jnp.inf, gs)
    masked = jnp.concatenate(
        [jnp.where(chosen[g:g + 1, :] > 0.5, biased[g * GROUP_SIZE:(g + 1) * GROUP_SIZE, :], -jnp.inf)
         for g in range(N_GROUPS)], axis=0)
    ei = lax.broadcasted_iota(I32, (N_EXPERTS, tt), 0).astype(F32)
    ids, ws = [], []
    for _ in range(TOP_K):
        _, ie = _first_max(masked, ei, float(N_EXPERTS))
        hit = ei == ie
        ws.append(jnp.sum(jnp.where(hit, scores, 0.0), axis=0, keepdims=True))
        ids.append(ie)
        masked = jnp.where(hit, -jnp.inf, masked)
    w = jnp.concatenate(ws, axis=0)
    gate_ref[...] = w / jnp.sum(w, axis=0, keepdims=True) * ROUTED_SCALE
    eidx_ref[...] = jnp.concatenate(ids, axis=0).astype(I32)


def _route(logits_t, bias_col, tt):
    E, T = logits_t.shape
    return pl.pallas_call(
        _route_kernel,
        grid=(T // tt,),
        in_specs=[pl.BlockSpec((E, tt), lambda i: (0, i)),
                  pl.BlockSpec((E, 1), lambda i: (0, 0))],
        out_specs=[pl.BlockSpec((TOP_K, tt), lambda i: (0, i)),
                   pl.BlockSpec((TOP_K, tt), lambda i: (0, i))],
        out_shape=[jax.ShapeDtypeStruct((TOP_K, T), I32),
                   jax.ShapeDtypeStruct((TOP_K, T), F32)],
        compiler_params=_params(("parallel",)),
        name="route",
    )(logits_t, bias_col)


def _plan_kernel(eidx_ref, dest_ref, bexp_ref, nused_ref, cnt_col, cnt_row, slot_base, *, blk, nb_pad):
    ph = pl.program_id(0)
    i = pl.program_id(1)
    tt = eidx_ref.shape[1]
    eidx = eidx_ref[...]
    ei = lax.broadcasted_iota(I32, (N_EXPERTS, tt), 0)
    onehot = jnp.zeros((N_EXPERTS, tt), F32)
    for k in range(TOP_K):
        onehot = onehot + jnp.where(ei == eidx[k:k + 1, :], 1.0, 0.0)
    oh = onehot.astype(BF16)

    @pl.when((ph == 0) & (i == 0))
    def _():
        cnt_col[...] = jnp.zeros(cnt_col.shape, F32)
        cnt_row[...] = jnp.zeros(cnt_row.shape, F32)

    @pl.when(ph == 0)
    def _():
        cnt_col[...] += _dot(oh, jnp.ones((tt, LANES), BF16))
        cnt_row[...] += _nt(jnp.ones((8, tt), BF16), oh)

    @pl.when((ph == 1) & (i == 0))
    def _():
        inv = 1.0 / blk
        nb_col = jnp.floor((cnt_col[:, 0:1] + (blk - 1)) * inv)
        nb_row = jnp.floor((cnt_row[0:1, :] + (blk - 1)) * inv)
        r = lax.broadcasted_iota(I32, (N_EXPERTS, N_EXPERTS), 0)
        c = lax.broadcasted_iota(I32, (N_EXPERTS, N_EXPERTS), 1)
        bstart = jnp.sum(jnp.where(c < r, nb_row, 0.0), axis=-1, keepdims=True)
        bend = bstart + nb_col
        slot_base[...] = bstart * blk
        jb = lax.broadcasted_iota(I32, (N_EXPERTS, nb_pad), 1).astype(F32)
        be = jnp.sum(jnp.where(bend <= jb, 1.0, 0.0), axis=0, keepdims=True)
        bexp_ref[...] = jnp.minimum(be, N_EXPERTS - 1.0).astype(I32)
        nused_ref[...] = jnp.broadcast_to(jnp.sum(nb_row, axis=-1, keepdims=True), nused_ref.shape).astype(I32)

    @pl.when(ph == 1)
    def _():
        tri = jnp.where(lax.broadcasted_iota(I32, (tt, tt), 0) < lax.broadcasted_iota(I32, (tt, tt), 1),
                        1.0, 0.0).astype(BF16)
        slot = _dot(oh, tri) + slot_base[...]
        for k in range(TOP_K):
            dk = jnp.sum(jnp.where(ei == eidx[k:k + 1, :], slot, 0.0), axis=0, keepdims=True)
            dest_ref[k:k + 1, :] = dk.astype(I32)
        slot_base[...] += jnp.sum(onehot, axis=-1, keepdims=True)


def _plan(eidx, tt, blk, nb_pad):
    K, T = eidx.shape
    nt = T // tt
    return pl.pallas_call(
        functools.partial(_plan_kernel, blk=blk, nb_pad=nb_pad),
        grid=(2, nt),
        in_specs=[pl.BlockSpec((K, tt), lambda p, i: (0, i))],
        out_specs=[pl.BlockSpec((K, tt), lambda p, i: (0, i * p)),
                   pl.BlockSpec((1, nb_pad), lambda p, i: (0, 0)),
                   pl.BlockSpec((1, LANES), lambda p, i: (0, 0))],
        out_shape=[jax.ShapeDtypeStruct((K, T), I32),
                   jax.ShapeDtypeStruct((1, nb_pad), I32),
                   jax.ShapeDtypeStruct((1, LANES), I32)],
        scratch_shapes=[pltpu.VMEM((N_EXPERTS, LANES), F32),
                        pltpu.VMEM((8, N_EXPERTS), F32),
                        pltpu.VMEM((N_EXPERTS, 1), F32)],
        compiler_params=_params(("arbitrary", "arbitrary")),
        name="plan",
    )(eidx)


def _row(ref, r):
    return ref.at[pl.ds(pl.multiple_of(r * PACK_TILES, PACK_TILES), PACK_TILES), :]


def _dispatch_kernel(dest_ref, h_ref, xs_in_ref, xs_ref, sem):
    del xs_in_ref
    tt = h_ref.shape[0] // PACK_TILES

    def issue(t, c):
        for k in range(TOP_K):
            pltpu.make_async_copy(_row(h_ref, t), _row(xs_ref, dest_ref[k, t]), sem).start(priority=k % 2)
        return c

    lax.fori_loop(0, tt, issue, 0)

    def drain(t, c):
        for k in range(TOP_K):
            pltpu.make_async_copy(_row(h_ref, 0), _row(xs_ref, 0), sem).wait()
        return c

    lax.fori_loop(0, tt, drain, 0)


def _dispatch(dest, h2rows, xs_init, tt):
    T = h2rows.shape[0] // PACK_TILES
    return pl.pallas_call(
        _dispatch_kernel,
        grid=(T // tt,),
        in_specs=[pl.BlockSpec((TOP_K, tt), lambda i: (0, i), memory_space=pltpu.SMEM),
                  pl.BlockSpec((tt * PACK_TILES, LANES), lambda i: (i, 0)),
                  pl.BlockSpec(memory_space=pl.ANY)],
        out_specs=pl.BlockSpec(memory_space=pl.ANY),
        out_shape=jax.ShapeDtypeStruct(xs_init.shape, xs_init.dtype),
        scratch_shapes=[pltpu.SemaphoreType.DMA(())],
        input_output_aliases={2: 0},
        compiler_params=_params(("arbitrary",), has_side_effects=True),
        name="dispatch",
    )(dest, h2rows, xs_init)


def _experts_kernel(bexp_ref, nused_ref, xs_ref, wg_ref, wu_ref, wd_ref, y_ref, xb_ref, wgb_ref, wub_ref, wdb_ref):
    j = pl.program_id(0)

    @pl.when(j < nused_ref[0])
    def _():
        @pl.when((j == 0) | (bexp_ref[j] != bexp_ref[jnp.maximum(j - 1, 0)]))
        def _():
            wgb_ref[...] = wg_ref[0].astype(BF16)
            wub_ref[...] = wu_ref[0].astype(BF16)
            wdb_ref[...] = wd_ref[0].astype(BF16)

        blk = xb_ref.shape[0]
        half = D_MODEL // 2
        for c in range(PACK_TILES):
            w = xs_ref[pl.ds(c, blk, stride=PACK_TILES), :]
            cols = slice(c * LANES, (c + 1) * LANES)
            xb_ref[:, cols] = lax.bitcast_convert_type(w << 16, F32).astype(BF16)
            xb_ref[:, half + c * LANES:half + (c + 1) * LANES] = lax.bitcast_convert_type(w & -65536, F32).astype(BF16)
        xb = xb_ref[...]
        gate = _dot(xb, wgb_ref[...])
        up = _dot(xb, wub_ref[...])
        act = gate / (1.0 + jnp.exp(-gate)) * up
        y = _dot(act.astype(BF16), wdb_ref[...])
        bits = lax.bitcast_convert_type(y.astype(BF16).astype(F32), I32)
        packed = (bits[:, half:] & -65536) | lax.shift_right_logical(bits[:, :half], 16)
        for c in range(PACK_TILES):
            y_ref[pl.ds(c, blk, stride=PACK_TILES), :] = packed[:, c * LANES:(c + 1) * LANES]


def _experts(bexp, nused, xs, wg, wu, wd, blk, n_blocks):
    live = lambda j, be, nu: jnp.minimum(j, nu[0] - 1)
    row_spec = pl.BlockSpec((blk * PACK_TILES, LANES), lambda j, be, nu: (live(j, be, nu), 0))
    wspec = lambda w: pl.BlockSpec((1,) + w.shape[1:], lambda j, be, nu: (be[live(j, be, nu)], 0, 0))
    return pl.pallas_call(
        _experts_kernel,
        grid_spec=pltpu.PrefetchScalarGridSpec(
            num_scalar_prefetch=2,
            grid=(n_blocks,),
            in_specs=[row_spec, wspec(wg), wspec(wu), wspec(wd)],
            out_specs=row_spec,
            scratch_shapes=[pltpu.VMEM((blk, D_MODEL), BF16),
                            pltpu.VMEM(wg.shape[1:], BF16), pltpu.VMEM(wu.shape[1:], BF16),
                            pltpu.VMEM(wd.shape[1:], BF16)]),
        out_shape=jax.ShapeDtypeStruct(xs.shape, I32),
        compiler_params=_params(("arbitrary",)),
        name="experts",
    )(bexp, nused, xs, wg, wu, wd)


def _combine_kernel(dest_ref, gate_ref, base_ref, mod_ref, y_ref, o_ref, buf, sem):
    tt = base_ref.shape[1]

    def issue(t, c):
        for k in range(TOP_K):
            pltpu.make_async_copy(_row(y_ref, dest_ref[k, t]), _row(buf, k * tt + t), sem).start(priority=k % 2)
        return c

    lax.fori_loop(0, tt, issue, 0)

    def drain(t, c):
        for k in range(TOP_K):
            pltpu.make_async_copy(_row(y_ref, 0), _row(buf, 0), sem).wait()
        return c

    lax.fori_loop(0, tt, drain, 0)

    gates = gate_ref[...]
    g2 = mod_ref[0][5:6]
    half = D_MODEL // 2
    for j in range(PACK_TILES):
        lo = jnp.zeros((tt, LANES), F32)
        hi = jnp.zeros((tt, LANES), F32)
        for k in range(TOP_K):
            w = buf[pl.ds(k * tt * PACK_TILES + j, tt, stride=PACK_TILES), :]
            lo = lo + gates[:, k:k + 1] * lax.bitcast_convert_type(w << 16, F32)
            hi = hi + gates[:, k:k + 1] * lax.bitcast_convert_type(w & -65536, F32)
        for off, acc in ((0, lo), (half, hi)):
            cols = slice(off + j * LANES, off + (j + 1) * LANES)
            o_ref[0, :, cols] = base_ref[0, :, cols] + g2[:, cols] * acc


def _combine(dest, gates_tk, base, mod3, y, tt):
    B, S, D = base.shape
    ns = S // tt
    return pl.pallas_call(
        _combine_kernel,
        grid=(B, ns),
        in_specs=[pl.BlockSpec((TOP_K, tt), lambda b, i: (0, b * ns + i), memory_space=pltpu.SMEM),
                  pl.BlockSpec((tt, TOP_K), lambda b, i: (b * ns + i, 0)),
                  pl.BlockSpec((1, tt, D), lambda b, i: (b, i, 0)),
                  pl.BlockSpec((1, 6, D), lambda b, i: (b, 0, 0)),
                  pl.BlockSpec(memory_space=pl.ANY)],
        out_specs=pl.BlockSpec((1, tt, D), lambda b, i: (b, i, 0)),
        out_shape=jax.ShapeDtypeStruct((B, S, D), F32),
        scratch_shapes=[pltpu.VMEM((TOP_K * tt * PACK_TILES, LANES), I32),
                        pltpu.SemaphoreType.DMA(())],
        compiler_params=_params(("arbitrary", "arbitrary")),
        name="combine",
    )(dest, gates_tk, base, mod3, y)


def _alibi_q_features(n_heads, maps_per_head):
    slopes = 2.0 ** (-8.0 * jnp.arange(1, n_heads + 1, dtype=F32) / n_heads)
    c = jnp.repeat(slopes, maps_per_head) * LOG2E * POS_RADIX
    pieces = []
    rest = c
    for _ in range(N_SPLIT):
        p = rest.astype(BF16).astype(F32)
        pieces.append(p)
        rest = rest - p
    hi = jnp.stack(pieces, axis=1)
    feat = jnp.concatenate([hi, hi / POS_RADIX], axis=1)
    return jnp.pad(feat, ((0, 0), (HEAD, LANES - HEAD - 2 * N_SPLIT)))


def kernel(x, c, ada_w, ada_b, norm1_g, norm2_g, w_in, diff_q_norm_g, diff_k_norm_g, lam_q1, lam_k1, lam_q2, lam_k2, diff_subln_g, dsa_q_norm_g, dsa_k_norm_g, idx_k_norm_g, w_out, router_w, router_bias, exp_w_gate, exp_w_up, exp_w_down, shared_w_gate, shared_w_up, shared_w_down):
    B, S, D = x.shape
    assert D == D_MODEL and ada_w.shape[0] == 1 and S <= POS_RADIX * 128
    T = B * S
    topk = min(DSA_TOPK, S // 4)
    tm = min(512, S)
    tq_diff = min(512, S)
    tq_dsa = min(256, S)
    tk_dsa = min(512, S)
    rs_dsa = min(128, tq_dsa)
    tt_route = min(512, T)
    tt_move = min(512, S)
    blk = 1024
    n_blocks = (T * TOP_K) // blk + N_EXPERTS
    nb_pad = -(-n_blocks // LANES) * LANES

    n_main = 7 * SEG
    wm = w_in[0, :, :n_main].astype(BF16)
    wt = jnp.pad(w_in[0, :, n_main:], ((0, 0), (0, LANES - (IDX_DIM + IDX_HEADS)))).astype(BF16)
    tile8 = lambda g: jnp.tile(g[0], SEG // g.shape[1]).reshape(1, SEG)
    gik = jnp.pad(idx_k_norm_g[0], (0, LANES - IDX_DIM)).reshape(1, LANES)
    lamv = jnp.concatenate([lam_q1, lam_k1, lam_q2, lam_k2], axis=0)
    wo1 = w_out[0, :SEG].astype(BF16)
    wo2 = w_out[0, SEG:].astype(BF16)
    rwt = router_w[0].T.astype(BF16)
    wg, wu, wd = exp_w_gate[0], exp_w_up[0], exp_w_down[0]

    mod3 = _ada(c, ada_w[0], ada_b[0]).reshape(B, 6, D)

    dq, dk, dv, sq, sk, sv, iq, ikl, ikh, iw = _inproj(
        x, mod3, norm1_g, wm, wt, tile8(diff_q_norm_g), tile8(diff_k_norm_g),
        tile8(dsa_q_norm_g), tile8(dsa_k_norm_g), gik,
        _alibi_q_features(DIFF_HEADS, 2), _alibi_q_features(DSA_HEADS, 1), tm)

    diff_out = _diff_attention(dq, dk, dv, lamv, diff_subln_g, tq_diff)
    dsa_out = _dsa_attention(iq, iw, ikl, ikh, sq, sk, sv, tq_dsa, tk_dsa, rs_dsa, topk)

    base, h2rows, logits_t = _mix(diff_out, dsa_out, x, mod3, norm2_g, wo1, wo2,
                                  shared_w_gate[0].astype(BF16), shared_w_up[0].astype(BF16),
                                  shared_w_down[0].astype(BF16), rwt, tm)

    eidx, gates = _route(logits_t, router_bias[0].reshape(N_EXPERTS, 1), tt_route)
    dest, bexp, nused = _plan(eidx, tt_route, blk, nb_pad)

    xs = _dispatch(dest, h2rows, jnp.zeros((n_blocks * blk * PACK_TILES, LANES), I32), tt_move)
    y = _experts(bexp.reshape(nb_pad), nused[0, :1], xs, wg, wu, wd, blk, n_blocks)
    return _combine(dest, gates.T, base, mod3, y, tt_move)
```
